```python
import jax, jax.numpy as jnp
from jax import lax
import numpy as np

D_MODEL = 4096
BATCH = 1
SEQ = 8192
DEPTH = 1
DEC_BATCH = 32
DEC_SEQ = 32
PAST_LEN = 2048

CHUNK = 64
MIX_WIDTH = D_MODEL
HEAD_DIM = 64
CONV_CH = MIX_WIDTH // 2
N_HEADS = (MIX_WIDTH - CONV_CH) // HEAD_DIM
N_KV_HEADS = 8
GQA_GROUP = N_HEADS // N_KV_HEADS
ATT_W = N_HEADS * HEAD_DIM
KV_W = N_KV_HEADS * HEAD_DIM
IN_WIDTH = 3 * CONV_CH + ATT_W + 2 * KV_W
SPLITS = (CONV_CH, 2 * CONV_CH, 3 * CONV_CH, 3 * CONV_CH + ATT_W, 3 * CONV_CH + ATT_W + KV_W)
CONV_WIDTH = 3
WINDOW = 128
WINDOW_CHUNKS = WINDOW // CHUNK
BAND = (WINDOW_CHUNKS + 1) * CHUNK
ROPE_THETA = 10000.0
N_EXPERTS = 32
TOP_K = 4
D_FF = D_MODEL
SWIGLU_LIMIT = 7.0
SWIGLU_ALPHA = 1.702
MOE_BLOCK = 128
NORM_EPS = 1e-5

kernel_name = "hybrid_conv_swa_moe_stream_step"


def rms_norm(x, g):
    xf = x.astype(jnp.float32)
    y = xf * lax.rsqrt(jnp.mean(xf * xf, axis=-1, keepdims=True) + NORM_EPS)
    return (y * g.astype(jnp.float32)).astype(x.dtype)


def rope(x, pos):
    half = HEAD_DIM // 2
    inv_freq = ROPE_THETA ** (-jnp.arange(half, dtype=jnp.float32) / half)
    ang = pos.astype(jnp.float32)[:, None] * inv_freq[None, :]
    cos = jnp.cos(ang)[None, :, None, :]
    sin = jnp.sin(ang)[None, :, None, :]
    xf = x.astype(jnp.float32)
    x1, x2 = xf[..., :half], xf[..., half:]
    return jnp.concatenate([x1 * cos - x2 * sin, x2 * cos + x1 * sin], axis=-1).astype(x.dtype)


def short_conv(u, prev, conv_w, conv_b):
    t = u.shape[1]
    full = jnp.concatenate([prev.astype(u.dtype), u], axis=1)
    y = conv_b + full[:, 0:t] * conv_w[0]
    for j in range(1, CONV_WIDTH):
        y = y + full[:, j:j + t] * conv_w[j]
    return y, full[:, t:]


def sink_softmax(s, sink):
    m = jnp.maximum(jnp.max(s, axis=-1, keepdims=True), sink)
    e = jnp.exp(s - m)
    return e / (jnp.sum(e, axis=-1, keepdims=True) + jnp.exp(sink - m))


def swa_banded(q, k, v, sinks):
    b, t = q.shape[0], q.shape[1]
    n = t // CHUNK
    qb = q.reshape(b, n, CHUNK, N_KV_HEADS, GQA_GROUP, HEAD_DIM).astype(jnp.float32)
    pad = ((0, 0), (WINDOW, 0), (0, 0), (0, 0))
    kp = jnp.pad(k, pad).reshape(b, n + WINDOW_CHUNKS, CHUNK, N_KV_HEADS, HEAD_DIM)
    vp = jnp.pad(v, pad).reshape(b, n + WINDOW_CHUNKS, CHUNK, N_KV_HEADS, HEAD_DIM)
    kb = jnp.concatenate([kp[:, j:j + n] for j in range(WINDOW_CHUNKS + 1)], axis=2)
    vb = jnp.concatenate([vp[:, j:j + n] for j in range(WINDOW_CHUNKS + 1)], axis=2)
    key_chunk = jnp.arange(n)[:, None] - WINDOW_CHUNKS + (jnp.arange(BAND) // CHUNK)[None, :]
    valid = (key_chunk >= 0)[None, :, None, None, None, :]
    s = jnp.einsum('bncvgd,bnkvd->bnvgck', qb, kb.astype(jnp.float32)) * (HEAD_DIM ** -0.5)
    s = jnp.where(valid, s, -jnp.inf)
    sink = sinks.astype(jnp.float32).reshape(N_KV_HEADS, GQA_GROUP)[None, None, :, :, None, None]
    p = sink_softmax(s, sink)
    o = jnp.einsum('bnvgck,bnkvd->bncvgd', p, vb.astype(jnp.float32))
    return o.reshape(b, t, ATT_W).astype(q.dtype)


def swa_cached(q, k_new, v_new, k_cache, v_cache, sinks):
    b, t = q.shape[0], q.shape[1]
    kk = jnp.concatenate([k_cache.astype(k_new.dtype), k_new], axis=1).astype(jnp.float32)
    vv = jnp.concatenate([v_cache.astype(v_new.dtype), v_new], axis=1).astype(jnp.float32)
    qs = q.reshape(b, t, N_KV_HEADS, GQA_GROUP, HEAD_DIM).astype(jnp.float32)
    s = jnp.einsum('btvgd,bkvd->bvgtk', qs, kk) * (HEAD_DIM ** -0.5)
    sink = sinks.astype(jnp.float32).reshape(N_KV_HEADS, GQA_GROUP)[None, :, :, None, None]
    p = sink_softmax(s, sink)
    o = jnp.einsum('bvgtk,bkvd->btvgd', p, vv)
    return o.reshape(b, t, ATT_W).astype(q.dtype)


def moe_ffn(h, w_router, b_router, w_gate_up, b_gate_up, w_down, b_down):
    n_tok, d = h.shape
    n_assign = n_tok * TOP_K
    logits = h.astype(jnp.float32) @ w_router.astype(jnp.float32) + b_router.astype(jnp.float32)
    top_logits, top_idx = lax.top_k(logits, TOP_K)
    gates = jax.nn.softmax(top_logits, axis=-1)
    expert_of = top_idx.reshape(-1).astype(jnp.int32)
    token_of = jnp.repeat(jnp.arange(n_tok, dtype=jnp.int32), TOP_K)
    gate_of = gates.reshape(-1)
    order = jnp.argsort(expert_of)
    e_sorted = expert_of[order]
    counts = jnp.bincount(expert_of, length=N_EXPERTS).astype(jnp.int32)
    padded = (counts + MOE_BLOCK - 1) // MOE_BLOCK * MOE_BLOCK
    pad_end = jnp.cumsum(padded)
    pad_start = pad_end - padded
    start = jnp.cumsum(counts) - counts
    dest = pad_start[e_sorted] + jnp.arange(n_assign, dtype=jnp.int32) - start[e_sorted]
    n_blocks = -(-n_assign // MOE_BLOCK) + N_EXPERTS
    n_rows = n_blocks * MOE_BLOCK
    row_token = jnp.full((n_rows,), n_tok, jnp.int32).at[dest].set(token_of[order])
    row_gate = jnp.zeros((n_rows,), jnp.float32).at[dest].set(gate_of[order])
    block_start = jnp.arange(n_blocks, dtype=jnp.int32) * MOE_BLOCK
    block_expert = jnp.minimum(jnp.searchsorted(pad_end, block_start, side='right'), N_EXPERTS - 1)
    h_pad = jnp.concatenate([h, jnp.zeros((1, d), h.dtype)], axis=0)
    x_blocks = h_pad[row_token].reshape(n_blocks, MOE_BLOCK, d)

    def expert_block(args):
        xb, e = args
        gu = xb @ w_gate_up[e] + b_gate_up[e]
        g = jnp.minimum(gu[:, :D_FF], SWIGLU_LIMIT)
        u = jnp.clip(gu[:, D_FF:], -SWIGLU_LIMIT, SWIGLU_LIMIT)
        act = (u + 1) * g * jax.nn.sigmoid(SWIGLU_ALPHA * g)
        return act @ w_down[e] + b_down[e]

    y_blocks = lax.map(expert_block, (x_blocks, block_expert))
    y_rows = y_blocks.reshape(n_rows, d) * row_gate[:, None].astype(h.dtype)
    return jax.ops.segment_sum(y_rows, row_token, num_segments=n_tok + 1)[:n_tok]


def layer(x, c, pos, conv_prev, k_cache, v_cache,
          w_ada, b_ada, norm_mix_g, norm_ffn_g, w_in, b_in, conv_w, conv_b, sinks,
          out_norm_conv_g, out_norm_attn_g, w_out, b_out,
          w_router, b_router, w_gate_up, b_gate_up, w_down, b_down):
    b, t, d = x.shape
    mod = jax.nn.silu(c) @ w_ada + b_ada
    sh1, sc1, g1, sh2, sc2, g2 = jnp.split(mod[:, None, :], 6, axis=-1)
    h = rms_norm(x, norm_mix_g) * (1 + sc1) + sh1
    proj = h @ w_in + b_in
    b_gate, c_gate, xc, q, k, v = jnp.split(proj, SPLITS, axis=-1)
    conv_out, conv_state = short_conv(c_gate * xc, conv_prev, conv_w, conv_b)
    y_conv = b_gate * conv_out
    q = rope(q.reshape(b, t, N_HEADS, HEAD_DIM), pos)
    k = rope(k.reshape(b, t, N_KV_HEADS, HEAD_DIM), pos)
    v = v.reshape(b, t, N_KV_HEADS, HEAD_DIM)
    if k_cache is None:
        y_att = swa_banded(q, k, v, sinks)
        k_rows, v_rows = k[:, -WINDOW:], v[:, -WINDOW:]
    else:
        y_att = swa_cached(q, k, v, k_cache, v_cache, sinks)
        k_rows, v_rows = k, v
    mixed = jnp.concatenate([rms_norm(y_conv, out_norm_conv_g), rms_norm(y_att, out_norm_attn_g)], axis=-1)
    x = x + g1 * (mixed @ w_out + b_out)
    h2 = rms_norm(x, norm_ffn_g) * (1 + sc2) + sh2
    ffn = moe_ffn(h2.reshape(b * t, d), w_router, b_router, w_gate_up, b_gate_up, w_down, b_down)
    x = x + g2 * ffn.reshape(b, t, d)
    return x, conv_state, k_rows, v_rows


def setup_inputs(seed: int = 0) -> dict:
    key = jax.random.key(seed)
    ks = jax.random.split(key, 27)
    f32 = jnp.float32
    nrm = lambda k, shape, s: jax.random.normal(k, shape, f32) * s
    return {
        'x_prompt': nrm(ks[0], (BATCH, SEQ, D_MODEL), 1.0),
        'x_sample': nrm(ks[1], (DEC_BATCH, DEC_SEQ, D_MODEL), 1.0),
        'cache_k': nrm(ks[2], (DEPTH, DEC_BATCH, WINDOW, N_KV_HEADS, HEAD_DIM), 1.0),
        'cache_v': nrm(ks[3], (DEPTH, DEC_BATCH, WINDOW, N_KV_HEADS, HEAD_DIM), 1.0),
        'state_conv': nrm(ks[4], (DEPTH, DEC_BATCH, CONV_WIDTH - 1, CONV_CH), 1.0),
        'c_prompt': nrm(ks[5], (BATCH, D_MODEL), 1.0),
        'c_sample': nrm(ks[6], (DEC_BATCH, D_MODEL), 1.0),
        'w_ada': nrm(ks[7], (D_MODEL, 6 * D_MODEL), 0.5 * D_MODEL ** -0.5),
        'b_ada': nrm(ks[8], (6 * D_MODEL,), 0.02),
        'norm_mix_g': 1.0 + nrm(ks[9], (D_MODEL,), 0.02),
        'norm_ffn_g': 1.0 + nrm(ks[10], (D_MODEL,), 0.02),
        'norm_final_g': 1.0 + nrm(ks[11], (D_MODEL,), 0.02),
        'w_in': nrm(ks[12], (D_MODEL, IN_WIDTH), D_MODEL ** -0.5),
        'b_in': nrm(ks[13], (IN_WIDTH,), 0.02),
        'conv_w': nrm(ks[14], (CONV_WIDTH, CONV_CH), CONV_WIDTH ** -0.5),
        'conv_b': nrm(ks[15], (CONV_CH,), 0.02),
        'sinks': nrm(ks[16], (N_HEADS,), 1.0),
        'out_norm_conv_g': 1.0 + nrm(ks[17], (CONV_CH,), 0.02),
        'out_norm_attn_g': 1.0 + nrm(ks[18], (ATT_W,), 0.02),
        'w_out': nrm(ks[19], (MIX_WIDTH, D_MODEL), MIX_WIDTH ** -0.5),
        'b_out': nrm(ks[20], (D_MODEL,), 0.02),
        'w_router': nrm(ks[21], (D_MODEL, N_EXPERTS), D_MODEL ** -0.5),
        'b_router': nrm(ks[22], (N_EXPERTS,), 0.01),
        'w_gate_up': nrm(ks[23], (N_EXPERTS, D_MODEL, 2 * D_FF), D_MODEL ** -0.5),
        'b_gate_up': nrm(ks[24], (N_EXPERTS, 2 * D_FF), 0.02),
        'w_down': nrm(ks[25], (N_EXPERTS, D_FF, D_MODEL), D_FF ** -0.5),
        'b_down': nrm(ks[26], (N_EXPERTS, D_MODEL), 0.02),
    }


def reference(x_prompt, x_sample, cache_k, cache_v, state_conv, c_prompt, c_sample,
              w_ada, b_ada, norm_mix_g, norm_ffn_g, norm_final_g, w_in, b_in, conv_w, conv_b,
              sinks, out_norm_conv_g, out_norm_attn_g, w_out, b_out,
              w_router, b_router, w_gate_up, b_gate_up, w_down, b_down):
    layer_weights = (w_ada, b_ada, norm_mix_g, norm_ffn_g, w_in, b_in, conv_w, conv_b, sinks,
                     out_norm_conv_g, out_norm_attn_g, w_out, b_out,
                     w_router, b_router, w_gate_up, b_gate_up, w_down, b_down)
    pos_prompt = jnp.arange(x_prompt.shape[1], dtype=jnp.int32)
    pos_sample = PAST_LEN + jnp.arange(x_sample.shape[1], dtype=jnp.int32)
    xp, xs = x_prompt, x_sample
    kp_l, vp_l, cp_l, ks_l, vs_l, cs_l = [], [], [], [], [], []
    for l in range(DEPTH):
        conv_zero = jnp.zeros((xp.shape[0], CONV_WIDTH - 1, CONV_CH), xp.dtype)
        xp, cp, kp, vp = layer(xp, c_prompt, pos_prompt, conv_zero, None, None, *layer_weights)
        xs, cs, k_s, v_s = layer(xs, c_sample, pos_sample, state_conv[l], cache_k[l], cache_v[l], *layer_weights)
        kp_l.append(kp); vp_l.append(vp); cp_l.append(cp)
        ks_l.append(k_s); vs_l.append(v_s); cs_l.append(cs)
    y_prompt = rms_norm(xp, norm_final_g)
    y_sample = rms_norm(xs, norm_final_g)
    return (y_prompt, y_sample, jnp.stack(kp_l), jnp.stack(vp_l), jnp.stack(cp_l),
            jnp.stack(ks_l), jnp.stack(vs_l), jnp.stack(cs_l))
```

```python
import functools

import jax
import jax.numpy as jnp
from jax import lax
from jax.experimental import pallas as pl
from jax.experimental.pallas import tpu as pltpu

F32 = jnp.float32
BF16 = jnp.bfloat16

CHUNK = 64
TOP_K = 4
PAST_LEN = 2048
ROPE_THETA = 10000.0
NORM_EPS = 1e-5
SWIGLU_LIMIT = 7.0
SWIGLU_ALPHA = 1.702

LANES = 128
SUBLANES = 8
VMEM_LIMIT_BYTES = 58 * 1024 * 1024

ROW_TILE = 512
ADA_COL_TILE = 512
INPROJ_COL_TILE = 1024
OUTPROJ_COL_TILE = 1024
CONV_ROW_TILE = 256
ATTN_CHUNKS_PER_STEP = 4
FFN_ROW_TILE = 256
MOE_ROW_TILE = 512
MOE_SUB_ROWS = 256
MOE_FF_TILE = 512
MOE_OUT_TILE = 512
GATHER_ROWS = 256
COMBINE_ROWS = 128


def _tile(pref, dim):
    t = min(pref, dim)
    while dim % t:
        t -= 1
    return t


def _params(sem):
    return pltpu.CompilerParams(dimension_semantics=sem, vmem_limit_bytes=VMEM_LIMIT_BYTES)


def _rms(x):
    return x * lax.rsqrt(jnp.mean(x * x, axis=-1, keepdims=True) + NORM_EPS)


def _adaln_kernel(c_ref, w_ref, b_ref, o_ref):
    c = c_ref[...]
    a = (c * jax.nn.sigmoid(c)).astype(BF16)
    o_ref[...] = jnp.dot(a, w_ref[...].astype(BF16), preferred_element_type=F32) + b_ref[...]


def _adaln(c, w_ada, b_ada):
    nb, d = c.shape
    n = w_ada.shape[1]
    tn = _tile(ADA_COL_TILE, n)
    return pl.pallas_call(
        _adaln_kernel,
        grid=(n // tn,),
        in_specs=[pl.BlockSpec((nb, d), lambda j: (0, 0)),
                  pl.BlockSpec((d, tn), lambda j: (0, j)),
                  pl.BlockSpec((1, tn), lambda j: (0, j))],
        out_specs=pl.BlockSpec((nb, tn), lambda j: (0, j)),
        out_shape=jax.ShapeDtypeStruct((nb, n), F32),
        compiler_params=_params(("arbitrary",)),
        name="adaln",
    )(c, w_ada, b_ada.reshape(1, n))


def _inproj_kernel(x_ref, sh_ref, sc_ref, g_ref, w_ref, b_ref, o_ref, h_ref):
    bb, tt, d = x_ref.shape

    @pl.when(pl.program_id(2) == 0)
    def _():
        y = _rms(x_ref[...]) * g_ref[...]
        h = y * (1.0 + sc_ref[...]) + sh_ref[...]
        h_ref[...] = h.reshape(bb * tt, d).astype(BF16)

    acc = jnp.dot(h_ref[...], w_ref[...], preferred_element_type=F32) + b_ref[...]
    o_ref[...] = acc.reshape(o_ref.shape)


def _row_blocks(b, t):
    if t >= ROW_TILE:
        return 1, _tile(ROW_TILE, t)
    return _tile(max(ROW_TILE // t, 1), b), t


def _inproj(x, mod, g, w_bf, b_in):
    b, t, d = x.shape
    n = w_bf.shape[1]
    bb, tt = _row_blocks(b, t)
    tn = _tile(INPROJ_COL_TILE, n)
    return pl.pallas_call(
        _inproj_kernel,
        grid=(b // bb, t // tt, n // tn),
        in_specs=[pl.BlockSpec((bb, tt, d), lambda i, s, j: (i, s, 0)),
                  pl.BlockSpec((bb, 1, d), lambda i, s, j: (i, 0, 0)),
                  pl.BlockSpec((bb, 1, d), lambda i, s, j: (i, 0, 1)),
                  pl.BlockSpec((1, d), lambda i, s, j: (0, 0)),
                  pl.BlockSpec((d, tn), lambda i, s, j: (0, j)),
                  pl.BlockSpec((1, tn), lambda i, s, j: (0, j))],
        out_specs=pl.BlockSpec((bb, tt, tn), lambda i, s, j: (i, s, j)),
        out_shape=jax.ShapeDtypeStruct((b, t, n), F32),
        scratch_shapes=[pltpu.VMEM((bb * tt, d), BF16)],
        compiler_params=_params(("arbitrary", "arbitrary", "arbitrary")),
        name="inproj",
    )(x, mod, mod, g.reshape(1, d), w_bf, b_in.reshape(1, n))


def _conv_kernel(bg_ref, cg_ref, xc_ref, cgp_ref, xcp_ref, st_ref, w_ref, b_ref, g_ref, y_ref, so_ref):
    tt = cg_ref.shape[0]
    u = cg_ref[...] * xc_ref[...]
    halo = cgp_ref[...] * xcp_ref[...]
    st = st_ref[...]
    first = pl.program_id(1) == 0
    um2 = jnp.where(first, st[0:1], halo[SUBLANES - 2:SUBLANES - 1])
    um1 = jnp.where(first, st[1:2], halo[SUBLANES - 1:SUBLANES])
    row = lax.broadcasted_iota(jnp.int32, u.shape, 0)
    u1 = jnp.where(row == 0, um1, pltpu.roll(u, 1, 0))
    u2 = jnp.where(row == 0, um2, jnp.where(row == 1, um1, pltpu.roll(u, 2, 0)))
    w = w_ref[...]
    conv = b_ref[...] + u2 * w[0:1]
    conv = conv + u1 * w[1:2]
    conv = conv + u * w[2:3]
    y = _rms(bg_ref[...] * conv) * g_ref[...]
    y_ref[...] = y.astype(BF16)
    so_ref[...] = u[tt - 2:tt]


def _conv_mix(proj, state, conv_w, conv_b, g):
    b, t, _ = proj.shape
    c = conv_w.shape[1]
    assert conv_w.shape[0] == 3
    tt = _tile(CONV_ROW_TILE, t)
    hs = tt // SUBLANES
    halo_map = lambda col: (lambda i, s: (i, jnp.maximum(s * hs - 1, 0), col))
    return pl.pallas_call(
        _conv_kernel,
        grid=(b, t // tt),
        in_specs=[pl.BlockSpec((None, tt, c), lambda i, s: (i, s, 0)),
                  pl.BlockSpec((None, tt, c), lambda i, s: (i, s, 1)),
                  pl.BlockSpec((None, tt, c), lambda i, s: (i, s, 2)),
                  pl.BlockSpec((None, SUBLANES, c), halo_map(1)),
                  pl.BlockSpec((None, SUBLANES, c), halo_map(2)),
                  pl.BlockSpec((None, 2, c), lambda i, s: (i, 0, 0)),
                  pl.BlockSpec((3, c), lambda i, s: (0, 0)),
                  pl.BlockSpec((1, c), lambda i, s: (0, 0)),
                  pl.BlockSpec((1, c), lambda i, s: (0, 0))],
        out_specs=[pl.BlockSpec((None, tt, c), lambda i, s: (i, s, 0)),
                   pl.BlockSpec((None, 2, c), lambda i, s: (i, 0, 0))],
        out_shape=[jax.ShapeDtypeStruct((b, t, c), BF16),
                   jax.ShapeDtypeStruct((b, 2, c), F32)],
        compiler_params=_params(("arbitrary", "arbitrary")),
        name="conv_mix",
    )(proj, proj, proj, proj, proj, state, conv_w, conv_b.reshape(1, c), g.reshape(1, c))


def _rope_tables(pos, head_dim):
    half = head_dim // 2
    inv_freq = ROPE_THETA ** (-jnp.arange(half, dtype=F32) / half)
    ang = pos.astype(F32)[:, None] * inv_freq[None, :]
    cos, sin = jnp.cos(ang), jnp.sin(ang)
    reps = LANES // head_dim
    return (jnp.tile(jnp.concatenate([cos, cos], axis=-1), (1, reps)),
            jnp.tile(jnp.concatenate([-sin, sin], axis=-1), (1, reps)))


def _rope(x, cos, sin, head_dim):
    width = x.shape[1]
    half = head_dim // 2
    reps = width // LANES
    cos = jnp.concatenate([cos] * reps, axis=1)
    sin = jnp.concatenate([sin] * reps, axis=1)
    lane = lax.broadcasted_iota(jnp.int32, x.shape, 1)
    swapped = jnp.where(lane % head_dim < half,
                        pltpu.roll(x, width - half, 1), pltpu.roll(x, half, 1))
    return x * cos + swapped * sin


def _attend(q, kb, vb, sinks_ref, mask, n_kv, head_dim):
    rows, width = q.shape
    grp = width // (n_kv * head_dim)
    scale = head_dim ** -0.5
    outs = []
    for v in range(n_kv):
        heads = [v * grp + g for g in range(grp)]
        qs = jnp.concatenate([q[:, h * head_dim:(h + 1) * head_dim] for h in heads], axis=0)
        kv_cols = slice(v * head_dim, (v + 1) * head_dim)
        s = lax.dot_general(qs.astype(BF16), kb[:, kv_cols], (((1,), (1,)), ((), ())),
                            preferred_element_type=F32) * scale
        if mask is not None:
            s = jnp.where(mask, s, -jnp.inf)
        sink = jnp.concatenate([jnp.full((rows, 1), sinks_ref[h], F32) for h in heads], axis=0)
        m = jnp.maximum(jnp.max(s, axis=-1, keepdims=True), sink)
        e = jnp.exp(s - m)
        den = jnp.sum(e, axis=-1, keepdims=True) + jnp.exp(sink - m)
        o = jnp.dot(e.astype(BF16), vb[:, kv_cols], preferred_element_type=F32) / den
        outs.extend(o[g * rows:(g + 1) * rows] for g in range(grp))
    return jnp.concatenate(outs, axis=1)


def _swa_banded_kernel(sinks_ref, q_ref, k_ref, v_ref, cos_ref, sin_ref, g_ref,
                       y_ref, kr_ref, kbuf, vbuf, *, n_kv, head_dim, window):
    rb = q_ref.shape[0]
    step = pl.program_id(1)

    @pl.when(step == 0)
    def _():
        kbuf[0:window] = jnp.zeros((window, kbuf.shape[1]), BF16)
        vbuf[0:window] = jnp.zeros((window, vbuf.shape[1]), BF16)

    @pl.when(step > 0)
    def _():
        kbuf[0:window] = kbuf[rb:rb + window]
        vbuf[0:window] = vbuf[rb:rb + window]

    cos, sin = cos_ref[...], sin_ref[...]
    k = _rope(k_ref[...], cos, sin, head_dim)
    kr_ref[...] = k
    kbuf[window:window + rb] = k.astype(BF16)
    vbuf[window:window + rb] = v_ref[...].astype(BF16)
    q = _rope(q_ref[...], cos, sin, head_dim)

    band = window + CHUNK
    grp = q.shape[1] // (n_kv * head_dim)
    for c in range(rb // CHUNK):
        lo = c * CHUNK
        key_pos = step * rb + lo + lax.broadcasted_iota(jnp.int32, (grp * CHUNK, band), 1)
        y = _attend(q[lo:lo + CHUNK], kbuf[lo:lo + band], vbuf[lo:lo + band], sinks_ref,
                    key_pos >= window, n_kv, head_dim)
        y_ref[lo:lo + CHUNK] = (_rms(y) * g_ref[...]).astype(BF16)


def _swa_banded(proj, sinks, g, n_kv, head_dim, window, col_q):
    b, t, _ = proj.shape
    att_w = sinks.shape[0] * head_dim
    kv_w = n_kv * head_dim
    rb = _tile(ATTN_CHUNKS_PER_STEP * CHUNK, t)
    assert window % CHUNK == 0 and t % CHUNK == 0
    cos, sin = _rope_tables(jnp.arange(t, dtype=jnp.int32), head_dim)
    col_k = (col_q + att_w) // kv_w
    kern = functools.partial(_swa_banded_kernel, n_kv=n_kv, head_dim=head_dim, window=window)
    return pl.pallas_call(
        kern,
        grid=(b, t // rb),
        in_specs=[pl.BlockSpec(memory_space=pltpu.SMEM),
                  pl.BlockSpec((None, rb, att_w), lambda i, s: (i, s, col_q // att_w)),
                  pl.BlockSpec((None, rb, kv_w), lambda i, s: (i, s, col_k)),
                  pl.BlockSpec((None, rb, kv_w), lambda i, s: (i, s, col_k + 1)),
                  pl.BlockSpec((rb, LANES), lambda i, s: (s, 0)),
                  pl.BlockSpec((rb, LANES), lambda i, s: (s, 0)),
                  pl.BlockSpec((1, att_w), lambda i, s: (0, 0))],
        out_specs=[pl.BlockSpec((None, rb, att_w), lambda i, s: (i, s, 0)),
                   pl.BlockSpec((None, rb, kv_w), lambda i, s: (i, s, 0))],
        out_shape=[jax.ShapeDtypeStruct((b, t, att_w), BF16),
                   jax.ShapeDtypeStruct((b, t, kv_w), F32)],
        scratch_shapes=[pltpu.VMEM((window + rb, kv_w), BF16),
                        pltpu.VMEM((window + rb, kv_w), BF16)],
        compiler_params=_params(("arbitrary", "arbitrary")),
        name="swa_banded",
    )(sinks, proj, proj, proj, cos, sin, g.reshape(1, att_w))


def _swa_cached_kernel(sinks_ref, q_ref, k_ref, v_ref, ck_ref, cv_ref, cos_ref, sin_ref, g_ref,
                       y_ref, kr_ref, *, n_kv, head_dim):
    cos, sin = cos_ref[...], sin_ref[...]
    k = _rope(k_ref[...], cos, sin, head_dim)
    kr_ref[...] = k
    q = _rope(q_ref[...], cos, sin, head_dim)
    kb = jnp.concatenate([ck_ref[...].astype(BF16), k.astype(BF16)], axis=0)
    vb = jnp.concatenate([cv_ref[...].astype(BF16), v_ref[...].astype(BF16)], axis=0)
    y = _attend(q, kb, vb, sinks_ref, None, n_kv, head_dim)
    y_ref[...] = (_rms(y) * g_ref[...]).astype(BF16)


def _swa_cached(proj, cache_k, cache_v, sinks, g, n_kv, head_dim, col_q):
    b, t, _ = proj.shape
    window = cache_k.shape[1]
    att_w = sinks.shape[0] * head_dim
    kv_w = n_kv * head_dim
    cos, sin = _rope_tables(PAST_LEN + jnp.arange(t, dtype=jnp.int32), head_dim)
    col_k = (col_q + att_w) // kv_w
    kern = functools.partial(_swa_cached_kernel, n_kv=n_kv, head_dim=head_dim)
    return pl.pallas_call(
        kern,
        grid=(b,),
        in_specs=[pl.BlockSpec(memory_space=pltpu.SMEM),
                  pl.BlockSpec((None, t, att_w), lambda i: (i, 0, col_q // att_w)),
                  pl.BlockSpec((None, t, kv_w), lambda i: (i, 0, col_k)),
                  pl.BlockSpec((None, t, kv_w), lambda i: (i, 0, col_k + 1)),
                  pl.BlockSpec((None, window, kv_w), lambda i: (i, 0, 0)),
                  pl.BlockSpec((None, window, kv_w), lambda i: (i, 0, 0)),
                  pl.BlockSpec((t, LANES), lambda i: (0, 0)),
                  pl.BlockSpec((t, LANES), lambda i: (0, 0)),
                  pl.BlockSpec((1, att_w), lambda i: (0, 0))],
        out_specs=[pl.BlockSpec((None, t, att_w), lambda i: (i, 0, 0)),
                   pl.BlockSpec((None, t, kv_w), lambda i: (i, 0, 0))],
        out_shape=[jax.ShapeDtypeStruct((b, t, att_w), BF16),
                   jax.ShapeDtypeStruct((b, t, kv_w), F32)],
        compiler_params=_params(("arbitrary",)),
        name="swa_cached",
    )(sinks, proj, proj, proj, cache_k, cache_v, cos, sin, g.reshape(1, att_w))


def _outproj_kernel(yc_ref, ya_ref, wt_ref, wb_ref, b_ref, x_ref, g1_ref, o_ref):
    bb, tt, c = yc_ref.shape
    acc = jnp.dot(yc_ref[...].reshape(bb * tt, c), wt_ref[...], preferred_element_type=F32)
    acc = acc + jnp.dot(ya_ref[...].reshape(bb * tt, c), wb_ref[...], preferred_element_type=F32)
    acc = acc + b_ref[...]
    o_ref[...] = x_ref[...] + g1_ref[...] * acc.reshape(o_ref.shape)


def _outproj(yc, ya, w_bf, b_out, x, mod):
    b, t, d = x.shape
    c = yc.shape[2]
    assert 2 * c == w_bf.shape[0]
    bb, tt = _row_blocks(b, t)
    tn = _tile(OUTPROJ_COL_TILE, d)
    return pl.pallas_call(
        _outproj_kernel,
        grid=(b // bb, t // tt, d // tn),
        in_specs=[pl.BlockSpec((bb, tt, c), lambda i, s, j: (i, s, 0)),
                  pl.BlockSpec((bb, tt, c), lambda i, s, j: (i, s, 0)),
                  pl.BlockSpec((c, tn), lambda i, s, j: (0, j)),
                  pl.BlockSpec((c, tn), lambda i, s, j: (1, j)),
                  pl.BlockSpec((1, tn), lambda i, s, j: (0, j)),
                  pl.BlockSpec((bb, tt, tn), lambda i, s, j: (i, s, j)),
                  pl.BlockSpec((bb, 1, tn), lambda i, s, j: (i, 0, 2 * (d // tn) + j))],
        out_specs=pl.BlockSpec((bb, tt, tn), lambda i, s, j: (i, s, j)),
        out_shape=jax.ShapeDtypeStruct((b, t, d), F32),
        compiler_params=_params(("arbitrary", "arbitrary", "arbitrary")),
        name="outproj",
    )(yc, ya, w_bf, w_bf, b_out.reshape(1, d), x, mod)


def _router_kernel(x_ref, sh_ref, sc_ref, g_ref, wr_ref, br_ref, h_ref, idx_ref, gate_ref):
    bb, tt, d = x_ref.shape
    n_exp = wr_ref.shape[1]
    y = _rms(x_ref[...]) * g_ref[...]
    h = (y * (1.0 + sc_ref[...]) + sh_ref[...]).reshape(bb * tt, d)
    h_ref[...] = h
    logits = jnp.dot(h, wr_ref[...], preferred_element_type=F32,
                     precision=lax.Precision.HIGHEST) + br_ref[...]
    col = lax.broadcasted_iota(jnp.int32, logits.shape, 1).astype(F32)
    work = logits
    vals, idxs = [], []
    for _ in range(TOP_K):
        m = jnp.max(work, axis=-1, keepdims=True)
        sel = jnp.min(jnp.where(work == m, col, float(n_exp)), axis=-1, keepdims=True)
        vals.append(m)
        idxs.append(sel)
        work = jnp.where(col == sel, -jnp.inf, work)
    exps = [jnp.exp(v - vals[0]) for v in vals]
    den = exps[0]
    for e in exps[1:]:
        den = den + e
    idx_ref[...] = jnp.concatenate(idxs, axis=1).astype(jnp.int32)
    gate_ref[...] = jnp.concatenate([e / den for e in exps], axis=1)


def _router(x1, mod, g, w_router, b_router):
    b, t, d = x1.shape
    n_exp = w_router.shape[1]
    rows_pref = FFN_ROW_TILE
    if t >= rows_pref:
        bb, tt = 1, _tile(rows_pref, t)
    else:
        bb, tt = _tile(max(rows_pref // t, 1), b), t
    rows = bb * tt
    n_tok = b * t
    flat = lambda i, s: (i * (t // tt) + s, 0)
    return pl.pallas_call(
        _router_kernel,
        grid=(b // bb, t // tt),
        in_specs=[pl.BlockSpec((bb, tt, d), lambda i, s: (i, s, 0)),
                  pl.BlockSpec((bb, 1, d), lambda i, s: (i, 0, 3)),
                  pl.BlockSpec((bb, 1, d), lambda i, s: (i, 0, 4)),
                  pl.BlockSpec((1, d), lambda i, s: (0, 0)),
                  pl.BlockSpec((d, n_exp), lambda i, s: (0, 0)),
                  pl.BlockSpec((1, n_exp), lambda i, s: (0, 0))],
        out_specs=[pl.BlockSpec((rows, d), flat),
                   pl.BlockSpec((rows, TOP_K), flat),
                   pl.BlockSpec((rows, TOP_K), flat)],
        out_shape=[jax.ShapeDtypeStruct((n_tok, d), F32),
                   jax.ShapeDtypeStruct((n_tok, TOP_K), jnp.int32),
                   jax.ShapeDtypeStruct((n_tok, TOP_K), F32)],
        compiler_params=_params(("arbitrary", "arbitrary")),
        name="router",
    )(x1, mod, mod, g.reshape(1, d), w_router, b_router.reshape(1, n_exp))


def _gather_kernel(nused_ref, idx_ref, src_ref, o_ref, idx_smem, buf, sem_idx, sem_rows):
    rows = buf.shape[0]

    @pl.when(pl.program_id(0) < nused_ref[0])
    def _():
        cp = pltpu.make_async_copy(idx_ref.at[0, 0], idx_smem, sem_idx)
        cp.start()
        cp.wait()

        def issue(r, carry):
            pltpu.make_async_copy(src_ref.at[pl.ds(idx_smem[r], 1)], buf.at[pl.ds(r, 1)], sem_rows).start()
            return carry

        lax.fori_loop(0, rows, issue, 0, unroll=8)
        pltpu.make_async_copy(src_ref.at[pl.ds(0, rows)], buf, sem_rows).wait()
        o_ref[...] = buf[...].astype(BF16)


def _gather_rows(src, row_src, n_used_blocks, rows):
    p = row_src.shape[0]
    d = src.shape[1]
    nblk = p // rows
    clamp = lambda i, nu: jnp.minimum(i, nu[0] - 1)
    return pl.pallas_call(
        _gather_kernel,
        grid_spec=pltpu.PrefetchScalarGridSpec(
            num_scalar_prefetch=1,
            grid=(nblk,),
            in_specs=[pl.BlockSpec((1, 1, rows), lambda i, nu: (clamp(i, nu), 0, 0)),
                      pl.BlockSpec(memory_space=pl.ANY)],
            out_specs=pl.BlockSpec((rows, d), lambda i, nu: (clamp(i, nu), 0)),
            scratch_shapes=[pltpu.SMEM((rows,), jnp.int32),
                            pltpu.VMEM((rows, d), F32),
                            pltpu.SemaphoreType.DMA(()),
                            pltpu.SemaphoreType.DMA(())]),
        out_shape=jax.ShapeDtypeStruct((p, d), BF16),
        compiler_params=_params(("arbitrary",)),
        name="moe_gather",
    )(n_used_blocks, row_src.reshape(nblk, 1, rows), src)


def _moe_up_kernel(te_ref, rows_ref, nused_ref, x_ref, wg_ref, wu_ref, bg_ref, bu_ref, o_ref, wg_bf, wu_bf):
    i = pl.program_id(1)
    tm = x_ref.shape[0]

    @pl.when(i < nused_ref[0])
    def _():
        prev = te_ref[jnp.maximum(i - 1, 0)]

        @pl.when((i == 0) | (te_ref[i] != prev))
        def _():
            wg_bf[...] = wg_ref[...].astype(BF16)
            wu_bf[...] = wu_ref[...].astype(BF16)

        for h in range(tm // MOE_SUB_ROWS):
            rs = pl.ds(h * MOE_SUB_ROWS, MOE_SUB_ROWS)

            @pl.when(rows_ref[i] > h * MOE_SUB_ROWS)
            def _():
                x = x_ref[rs, :]
                g = jnp.dot(x, wg_bf[...], preferred_element_type=F32) + bg_ref[...]
                u = jnp.dot(x, wu_bf[...], preferred_element_type=F32) + bu_ref[...]
                g = jnp.minimum(g, SWIGLU_LIMIT)
                u = jnp.clip(u, -SWIGLU_LIMIT, SWIGLU_LIMIT)
                o_ref[rs, :] = ((u + 1.0) * g * jax.nn.sigmoid(SWIGLU_ALPHA * g)).astype(BF16)

            @pl.when(rows_ref[i] <= h * MOE_SUB_ROWS)
            def _():
                o_ref[rs, :] = jnp.zeros((MOE_SUB_ROWS, o_ref.shape[1]), BF16)


def _moe_up(xs, w_gate_up, b_gate_up, tile_expert, tile_rows, n_used, tm):
    p, d = xs.shape
    n_exp, _, f2 = w_gate_up.shape
    f = f2 // 2
    tf = _tile(MOE_FF_TILE, f)
    nj = f // tf
    cl = lambda i, nu: jnp.minimum(i, nu[0] - 1)
    return pl.pallas_call(
        _moe_up_kernel,
        grid_spec=pltpu.PrefetchScalarGridSpec(
            num_scalar_prefetch=3,
            grid=(nj, p // tm),
            in_specs=[pl.BlockSpec((tm, d), lambda j, i, te, tr, nu: (cl(i, nu), 0)),
                      pl.BlockSpec((None, d, tf), lambda j, i, te, tr, nu: (te[cl(i, nu)], 0, j)),
                      pl.BlockSpec((None, d, tf), lambda j, i, te, tr, nu: (te[cl(i, nu)], 0, nj + j)),
                      pl.BlockSpec((None, 1, tf), lambda j, i, te, tr, nu: (te[cl(i, nu)], 0, j)),
                      pl.BlockSpec((None, 1, tf), lambda j, i, te, tr, nu: (te[cl(i, nu)], 0, nj + j))],
            out_specs=pl.BlockSpec((tm, tf), lambda j, i, te, tr, nu: (cl(i, nu), j)),
            scratch_shapes=[pltpu.VMEM((d, tf), BF16), pltpu.VMEM((d, tf), BF16)]),
        out_shape=jax.ShapeDtypeStruct((p, f), BF16),
        compiler_params=_params(("arbitrary", "arbitrary")),
        name="moe_up",
    )(tile_expert, tile_rows, n_used, xs, w_gate_up, w_gate_up,
      b_gate_up.reshape(n_exp, 1, f2), b_gate_up.reshape(n_exp, 1, f2))


def _moe_down_kernel(te_ref, rows_ref, nused_ref, a_ref, w_ref, b_ref, o_ref, w_bf):
    i = pl.program_id(1)
    tm = a_ref.shape[0]

    @pl.when(i < nused_ref[0])
    def _():
        prev = te_ref[jnp.maximum(i - 1, 0)]

        @pl.when((i == 0) | (te_ref[i] != prev))
        def _():
            w_bf[...] = w_ref[...].astype(BF16)

        for h in range(tm // MOE_SUB_ROWS):
            rs = pl.ds(h * MOE_SUB_ROWS, MOE_SUB_ROWS)

            @pl.when(rows_ref[i] > h * MOE_SUB_ROWS)
            def _():
                o_ref[rs, :] = jnp.dot(a_ref[rs, :], w_bf[...], preferred_element_type=F32) + b_ref[...]

            @pl.when(rows_ref[i] <= h * MOE_SUB_ROWS)
            def _():
                o_ref[rs, :] = jnp.zeros((MOE_SUB_ROWS, o_ref.shape[1]), F32)


def _moe_down(act, w_down, b_down, tile_expert, tile_rows, n_used, tm):
    p, f = act.shape
    n_exp, _, d = w_down.shape
    tn = _tile(MOE_OUT_TILE, d)
    cl = lambda i, nu: jnp.minimum(i, nu[0] - 1)
    return pl.pallas_call(
        _moe_down_kernel,
        grid_spec=pltpu.PrefetchScalarGridSpec(
            num_scalar_prefetch=3,
            grid=(d // tn, p // tm),
            in_specs=[pl.BlockSpec((tm, f), lambda j, i, te, tr, nu: (cl(i, nu), 0)),
                      pl.BlockSpec((None, f, tn), lambda j, i, te, tr, nu: (te[cl(i, nu)], 0, j)),
                      pl.BlockSpec((None, 1, tn), lambda j, i, te, tr, nu: (te[cl(i, nu)], 0, j))],
            out_specs=pl.BlockSpec((tm, tn), lambda j, i, te, tr, nu: (cl(i, nu), j)),
            scratch_shapes=[pltpu.VMEM((f, tn), BF16)]),
        out_shape=jax.ShapeDtypeStruct((p, d), F32),
        compiler_params=_params(("arbitrary", "arbitrary")),
        name="moe_down",
    )(tile_expert, tile_rows, n_used, act, w_down, b_down.reshape(n_exp, 1, d))


def _combine_kernel(pos_ref, y_ref, gate_ref, x_ref, g2_ref, gf_ref, o_ref, pos_smem, buf, sem_idx, sem_rows):
    bb, tt, d = x_ref.shape
    rows = bb * tt
    cp = pltpu.make_async_copy(pos_ref.at[0, 0], pos_smem, sem_idx)
    cp.start()
    cp.wait()

    def issue(r, carry):
        pltpu.make_async_copy(y_ref.at[pl.ds(pos_smem[r], 1)], buf.at[pl.ds(r, 1)], sem_rows).start()
        return carry

    lax.fori_loop(0, TOP_K * rows, issue, 0, unroll=8)
    pltpu.make_async_copy(y_ref.at[pl.ds(0, TOP_K * rows)], buf, sem_rows).wait()
    gates = gate_ref[...]
    ffn = gates[:, 0:1] * buf[0:rows]
    for k in range(1, TOP_K):
        ffn = ffn + gates[:, k:k + 1] * buf[k * rows:(k + 1) * rows]
    x2 = x_ref[...] + g2_ref[...] * ffn.reshape(bb, tt, d)
    o_ref[...] = _rms(x2) * gf_ref[...]


def _combine(y_sorted, pos_tiles, gates, x1, mod, gf, tok_offset):
    b, t, d = x1.shape
    rows_pref = COMBINE_ROWS
    if t >= rows_pref:
        bb, tt = 1, _tile(rows_pref, t)
    else:
        bb, tt = _tile(max(rows_pref // t, 1), b), t
    rows = bb * tt
    assert tok_offset % rows == 0
    blk0 = tok_offset // rows
    flat = lambda i, s: (blk0 + i * (t // tt) + s, 0)
    return pl.pallas_call(
        _combine_kernel,
        grid=(b // bb, t // tt),
        in_specs=[pl.BlockSpec((1, 1, TOP_K * rows), lambda i, s: (blk0 + i * (t // tt) + s, 0, 0)),
                  pl.BlockSpec(memory_space=pl.ANY),
                  pl.BlockSpec((rows, TOP_K), flat),
                  pl.BlockSpec((bb, tt, d), lambda i, s: (i, s, 0)),
                  pl.BlockSpec((bb, 1, d), lambda i, s: (i, 0, 5)),
                  pl.BlockSpec((1, d), lambda i, s: (0, 0))],
        out_specs=pl.BlockSpec((bb, tt, d), lambda i, s: (i, s, 0)),
        out_shape=jax.ShapeDtypeStruct((b, t, d), F32),
        scratch_shapes=[pltpu.SMEM((TOP_K * rows,), jnp.int32),
                        pltpu.VMEM((TOP_K * rows, d), F32),
                        pltpu.SemaphoreType.DMA(()),
                        pltpu.SemaphoreType.DMA(())],
        compiler_params=_params(("arbitrary", "arbitrary")),
        name="moe_combine",
    ), rows


def _routing(top_idx, n_exp, tm):
    n_tok = top_idx.shape[0]
    n_assign = n_tok * TOP_K
    e_flat = top_idx.reshape(n_assign)
    onehot = (e_flat[:, None] == jnp.arange(n_exp, dtype=jnp.int32)[None, :]).astype(jnp.int32)
    csum = jnp.cumsum(onehot, axis=0)
    rank = jnp.take_along_axis(csum, e_flat[:, None], axis=1)[:, 0] - 1
    counts = csum[-1]
    padded = (counts + tm - 1) // tm * tm
    pad_end = jnp.cumsum(padded)
    pad_start = pad_end - padded
    pos = pad_start[e_flat] + rank
    n_tiles = n_assign // tm + n_exp
    p = n_tiles * tm
    token_of = jnp.arange(n_assign, dtype=jnp.int32) // TOP_K
    row_token = jnp.zeros((p,), jnp.int32).at[pos].set(token_of)
    tile_start = jnp.arange(n_tiles, dtype=jnp.int32) * tm
    tile_expert = jnp.minimum(jnp.searchsorted(pad_end, tile_start, side="right"), n_exp - 1).astype(jnp.int32)
    tile_rows = jnp.clip(pad_start[tile_expert] + counts[tile_expert] - tile_start, 0, tm).astype(jnp.int32)
    n_used = (pad_end[-1] // tm).astype(jnp.int32).reshape(1)
    return pos.astype(jnp.int32), row_token, tile_expert, tile_rows, n_used


def kernel(x_prompt, x_sample, cache_k, cache_v, state_conv, c_prompt, c_sample, w_ada, b_ada, norm_mix_g, norm_ffn_g, norm_final_g, w_in, b_in, conv_w, conv_b, sinks, out_norm_conv_g, out_norm_attn_g, w_out, b_out, w_router, b_router, w_gate_up, b_gate_up, w_down, b_down):
    d = x_prompt.shape[2]
    depth, dec_b, window, n_kv, head_dim = cache_k.shape
    assert depth == 1
    conv_ch = conv_w.shape[1]
    n_exp = w_router.shape[1]
    bp, tp, _ = x_prompt.shape
    bs, ts, _ = x_sample.shape
    col_q = 3 * conv_ch
    kv_w = n_kv * head_dim

    mod = _adaln(jnp.concatenate([c_prompt, c_sample], axis=0), w_ada, b_ada)
    mod_p = mod[:bp].reshape(bp, 1, 6 * d)
    mod_s = mod[bp:].reshape(bs, 1, 6 * d)
    w_in_bf = w_in.astype(BF16)
    w_out_bf = w_out.astype(BF16)

    def mixer(x, modx, conv_prev, ck, cv):
        proj = _inproj(x, modx, norm_mix_g, w_in_bf, b_in)
        yc, conv_state = _conv_mix(proj, conv_prev, conv_w, conv_b, out_norm_conv_g)
        if ck is None:
            ya, k_rot = _swa_banded(proj, sinks, out_norm_attn_g, n_kv, head_dim, window, col_q)
        else:
            ya, k_rot = _swa_cached(proj, ck, cv, sinks, out_norm_attn_g, n_kv, head_dim, col_q)
        v_rows = proj[:, :, col_q + sinks.shape[0] * head_dim + kv_w:]
        x1 = _outproj(yc, ya, w_out_bf, b_out, x, modx)
        h2, top_idx, gates = _router(x1, modx, norm_ffn_g, w_router, b_router)
        return x1, h2, top_idx, gates, conv_state, k_rot, v_rows

    zero_state = jnp.zeros((bp, 2, conv_ch), F32)
    x1p, h2p, idxp, gatep, conv_p, kp, vp = mixer(x_prompt, mod_p, zero_state, None, None)
    x1s, h2s, idxs, gates_s, conv_s, ksr, vsr = mixer(
        x_sample, mod_s, state_conv[0],
        cache_k[0].reshape(dec_b, window, kv_w), cache_v[0].reshape(dec_b, window, kv_w))

    n_p, n_s = bp * tp, bs * ts
    h2 = jnp.concatenate([h2p, h2s], axis=0)
    top_idx = jnp.concatenate([idxp, idxs], axis=0)
    gates = jnp.concatenate([gatep, gates_s], axis=0)
    tm = MOE_ROW_TILE
    pos, row_token, tile_expert, tile_rows, n_used = _routing(top_idx, n_exp, tm)
    xs = _gather_rows(h2, row_token, n_used * (tm // GATHER_ROWS), GATHER_ROWS)
    act = _moe_up(xs, w_gate_up, b_gate_up, tile_expert, tile_rows, n_used, tm)
    y_sorted = _moe_down(act, w_down, b_down, tile_expert, tile_rows, n_used, tm)

    def finish(x1, modx, tok_offset):
        call, rows = _combine(y_sorted, None, gates, x1, modx, norm_final_g, tok_offset)
        n_tok = pos.shape[0] // TOP_K
        pos_tiles = pos.reshape(n_tok // rows, rows, TOP_K).transpose(0, 2, 1).reshape(n_tok // rows, 1, TOP_K * rows)
        return call(pos_tiles, y_sorted, gates, x1, modx, norm_final_g.reshape(1, d))

    y_prompt = finish(x1p, mod_p, 0)
    y_sample = finish(x1s, mod_s, n_p)

    k_prompt = kp[:, tp - window:].reshape(1, bp, window, n_kv, head_dim)
    v_prompt = vp[:, tp - window:].reshape(1, bp, window, n_kv, head_dim)
    k_sample = ksr.reshape(1, bs, ts, n_kv, head_dim)
    v_sample = vsr.reshape(1, bs, ts, n_kv, head_dim)
    return (y_prompt, y_sample, k_prompt, v_prompt, conv_p[None], k_sample, v_sample, conv_s[None])
```

```python
import functools

import jax
import jax.numpy as jnp
from jax import lax
from jax.experimental import pallas as pl
from jax.experimental.pallas import tpu as pltpu

F32 = jnp.float32
BF16 = jnp.bfloat16
U32 = jnp.uint32

CHUNK = 64
TOP_K = 4
PAST_LEN = 2048
ROPE_THETA = 10000.0
NORM_EPS = 1e-5
SWIGLU_LIMIT = 7.0
SWIGLU_ALPHA = 1.702

LANES = 128
SUBLANES = 8
VMEM_LIMIT_BYTES = 58 * 1024 * 1024

ROW_TILE = 512
ADA_COL_TILE = 512
INPROJ_COL_TILE = 1024
OUTPROJ_COL_TILE = 1024
CONV_ROW_TILE = 256
ATTN_CHUNKS_PER_STEP = 4
FFN_ROW_TILE = 256
MOE_ROWS = 256
MOE_FF_TILE = 512
MOE_OUT_TILE = 1024
COMBINE_ROWS = 128


def _tile(pref, dim):
    t = min(pref, dim)
    while dim % t:
        t -= 1
    return t


def _params(sem):
    return pltpu.CompilerParams(dimension_semantics=sem, vmem_limit_bytes=VMEM_LIMIT_BYTES)


def _rms(x):
    return x * lax.rsqrt(jnp.mean(x * x, axis=-1, keepdims=True) + NORM_EPS)


def _token_blocks(b, t, rows_pref):
    if t >= rows_pref:
        return 1, _tile(rows_pref, t)
    return _tile(max(rows_pref // t, 1), b), t


def _adaln_kernel(c_ref, w_ref, b_ref, o_ref):
    c = c_ref[...]
    a = (c * jax.nn.sigmoid(c)).astype(BF16)
    o_ref[...] = jnp.dot(a, w_ref[...].astype(BF16), preferred_element_type=F32) + b_ref[...]


def _adaln(c, w_ada, b_ada):
    nb, d = c.shape
    n = w_ada.shape[1]
    tn = _tile(ADA_COL_TILE, n)
    return pl.pallas_call(
        _adaln_kernel,
        grid=(n // tn,),
        in_specs=[pl.BlockSpec((nb, d), lambda j: (0, 0)),
                  pl.BlockSpec((d, tn), lambda j: (0, j)),
                  pl.BlockSpec((1, tn), lambda j: (0, j))],
        out_specs=pl.BlockSpec((nb, tn), lambda j: (0, j)),
        out_shape=jax.ShapeDtypeStruct((nb, n), F32),
        compiler_params=_params(("arbitrary",)),
        name="adaln",
    )(c, w_ada, b_ada.reshape(1, n))


def _inproj_kernel(x_ref, sh_ref, sc_ref, g_ref, w_ref, b_ref, o_ref, h_ref):
    bb, tt, d = x_ref.shape

    @pl.when(pl.program_id(2) == 0)
    def _():
        y = _rms(x_ref[...]) * g_ref[...]
        h = y * (1.0 + sc_ref[...]) + sh_ref[...]
        h_ref[...] = h.reshape(bb * tt, d).astype(BF16)

    acc = jnp.dot(h_ref[...], w_ref[...], preferred_element_type=F32) + b_ref[...]
    o_ref[...] = acc.reshape(o_ref.shape)


def _inproj(x, mod, g, w_bf, b_in):
    b, t, d = x.shape
    n = w_bf.shape[1]
    bb, tt = _token_blocks(b, t, ROW_TILE)
    tn = _tile(INPROJ_COL_TILE, n)
    return pl.pallas_call(
        _inproj_kernel,
        grid=(b // bb, t // tt, n // tn),
        in_specs=[pl.BlockSpec((bb, tt, d), lambda i, s, j: (i, s, 0)),
                  pl.BlockSpec((bb, 1, d), lambda i, s, j: (i, 0, 0)),
                  pl.BlockSpec((bb, 1, d), lambda i, s, j: (i, 0, 1)),
                  pl.BlockSpec((1, d), lambda i, s, j: (0, 0)),
                  pl.BlockSpec((d, tn), lambda i, s, j: (0, j)),
                  pl.BlockSpec((1, tn), lambda i, s, j: (0, j))],
        out_specs=pl.BlockSpec((bb, tt, tn), lambda i, s, j: (i, s, j)),
        out_shape=jax.ShapeDtypeStruct((b, t, n), F32),
        scratch_shapes=[pltpu.VMEM((bb * tt, d), BF16)],
        compiler_params=_params(("arbitrary", "arbitrary", "arbitrary")),
        name="inproj",
    )(x, mod, mod, g.reshape(1, d), w_bf, b_in.reshape(1, n))


def _conv_kernel(bg_ref, cg_ref, xc_ref, cgp_ref, xcp_ref, st_ref, w_ref, b_ref, g_ref, y_ref, so_ref):
    tt = cg_ref.shape[0]
    u = cg_ref[...] * xc_ref[...]
    halo = cgp_ref[...] * xcp_ref[...]
    st = st_ref[...]
    first = pl.program_id(1) == 0
    um2 = jnp.where(first, st[0:1], halo[SUBLANES - 2:SUBLANES - 1])
    um1 = jnp.where(first, st[1:2], halo[SUBLANES - 1:SUBLANES])
    row = lax.broadcasted_iota(jnp.int32, u.shape, 0)
    u1 = jnp.where(row == 0, um1, pltpu.roll(u, 1, 0))
    u2 = jnp.where(row == 0, um2, jnp.where(row == 1, um1, pltpu.roll(u, 2, 0)))
    w = w_ref[...]
    conv = b_ref[...] + u2 * w[0:1]
    conv = conv + u1 * w[1:2]
    conv = conv + u * w[2:3]
    y = _rms(bg_ref[...] * conv) * g_ref[...]
    y_ref[...] = y.astype(BF16)
    so_ref[...] = u[tt - 2:tt]


def _conv_mix(proj, state, conv_w, conv_b, g):
    b, t, _ = proj.shape
    c = conv_w.shape[1]
    assert conv_w.shape[0] == 3
    tt = _tile(CONV_ROW_TILE, t)
    hs = tt // SUBLANES
    halo_map = lambda col: (lambda i, s: (i, jnp.maximum(s * hs - 1, 0), col))
    return pl.pallas_call(
        _conv_kernel,
        grid=(b, t // tt),
        in_specs=[pl.BlockSpec((None, tt, c), lambda i, s: (i, s, 0)),
                  pl.BlockSpec((None, tt, c), lambda i, s: (i, s, 1)),
                  pl.BlockSpec((None, tt, c), lambda i, s: (i, s, 2)),
                  pl.BlockSpec((None, SUBLANES, c), halo_map(1)),
                  pl.BlockSpec((None, SUBLANES, c), halo_map(2)),
                  pl.BlockSpec((None, 2, c), lambda i, s: (i, 0, 0)),
                  pl.BlockSpec((3, c), lambda i, s: (0, 0)),
                  pl.BlockSpec((1, c), lambda i, s: (0, 0)),
                  pl.BlockSpec((1, c), lambda i, s: (0, 0))],
        out_specs=[pl.BlockSpec((None, tt, c), lambda i, s: (i, s, 0)),
                   pl.BlockSpec((None, 2, c), lambda i, s: (i, 0, 0))],
        out_shape=[jax.ShapeDtypeStruct((b, t, c), BF16),
                   jax.ShapeDtypeStruct((b, 2, c), F32)],
        compiler_params=_params(("arbitrary", "arbitrary")),
        name="conv_mix",
    )(proj, proj, proj, proj, proj, state, conv_w, conv_b.reshape(1, c), g.reshape(1, c))


def _head_order(n_heads, n_kv):
    grp = n_heads // n_kv
    assert n_kv % 2 == 0
    return [(2 * p + half) * grp + g for p in range(n_kv // 2) for g in range(grp) for half in (0, 1)]


def _permute_blocks(x, order, width, axis, start):
    n = len(order) * width
    take = lambda lo, hi: lax.slice_in_dim(x, lo, hi, axis=axis)
    parts = [take(0, start)] + [take(start + h * width, start + (h + 1) * width) for h in order]
    parts.append(take(start + n, x.shape[axis]))
    return jnp.concatenate(parts, axis=axis)


def _rope_tables(pos, head_dim):
    half = head_dim // 2
    inv_freq = ROPE_THETA ** (-jnp.arange(half, dtype=F32) / half)
    ang = pos.astype(F32)[:, None] * inv_freq[None, :]
    cos, sin = jnp.cos(ang), jnp.sin(ang)
    reps = LANES // head_dim
    return (jnp.tile(jnp.concatenate([cos, cos], axis=-1), (1, reps)),
            jnp.tile(jnp.concatenate([-sin, sin], axis=-1), (1, reps)))


def _rope(x, cos, sin, head_dim):
    width = x.shape[1]
    half = head_dim // 2
    reps = width // LANES
    cos = jnp.concatenate([cos] * reps, axis=1)
    sin = jnp.concatenate([sin] * reps, axis=1)
    lane = lax.broadcasted_iota(jnp.int32, x.shape, 1)
    swapped = jnp.where(lane % head_dim < half,
                        pltpu.roll(x, width - half, 1), pltpu.roll(x, half, 1))
    return x * cos + swapped * sin


def _pair_blocks(slab):
    keys = slab.shape[0]
    lo = lax.broadcasted_iota(jnp.int32, (CHUNK, LANES), 1) < LANES // 2
    zero = jnp.zeros((CHUNK, LANES), slab.dtype)
    parts = []
    for c in range(keys // CHUNK):
        blk = slab[c * CHUNK:(c + 1) * CHUNK]
        parts += [jnp.where(lo, blk, zero), jnp.where(lo, zero, blk)]
    return jnp.concatenate(parts, axis=0)


def _attend_pair(qs, kblk, vblk, sink_vec, key_ok):
    head_dim = LANES // 2
    n_chunks = kblk.shape[0] // LANES
    lo = lax.broadcasted_iota(jnp.int32, (1, LANES), 1) < head_dim
    s = lax.dot_general(qs.astype(BF16), kblk, (((1,), (1,)), ((), ())),
                        preferred_element_type=F32) * (head_dim ** -0.5)
    tiles = [jnp.where(key_ok[c], s[:, c * LANES:(c + 1) * LANES], -jnp.inf) for c in range(n_chunks)]

    def per_half(t, reduce_fn, fill):
        a = reduce_fn(jnp.where(lo, t, fill), axis=-1, keepdims=True)
        b = reduce_fn(jnp.where(lo, fill, t), axis=-1, keepdims=True)
        return jnp.where(lo, a, b)

    tmax = tiles[0]
    for t in tiles[1:]:
        tmax = jnp.maximum(tmax, t)
    m = jnp.maximum(per_half(tmax, jnp.max, -jnp.inf), sink_vec)
    es = [jnp.exp(t - m) for t in tiles]
    esum = es[0]
    for e in es[1:]:
        esum = esum + e
    den = per_half(esum, jnp.sum, 0.0) + jnp.exp(sink_vec - m)
    e_all = jnp.concatenate([e.astype(BF16) for e in es], axis=1)
    return jnp.dot(e_all, vblk, preferred_element_type=F32) / den


def _attend(q, kslabs, vslabs, sinks_ref, key_ok, grp):
    rows = q.shape[0]
    head_dim = LANES // 2
    lo = lax.broadcasted_iota(jnp.int32, (1, LANES), 1) < head_dim
    outs = []
    for p in range(len(kslabs)):
        qs = jnp.concatenate([q[:, (p * grp + g) * LANES:(p * grp + g + 1) * LANES] for g in range(grp)], axis=0)
        sink_vec = jnp.concatenate(
            [jnp.broadcast_to(jnp.where(lo, sinks_ref[2 * p * grp + g], sinks_ref[(2 * p + 1) * grp + g]),
                              (rows, LANES)) for g in range(grp)], axis=0)
        o = _attend_pair(qs, kslabs[p], vslabs[p], sink_vec, key_ok)
        outs.extend(o[g * rows:(g + 1) * rows] for g in range(grp))
    return jnp.concatenate(outs, axis=1)


def _swa_banded_kernel(sinks_ref, q_ref, k_ref, v_ref, cos_ref, sin_ref, g_ref,
                       y_ref, kr_ref, kbuf, vbuf, *, grp, window):
    rb = q_ref.shape[0]
    head_dim = LANES // 2
    n_pairs = kbuf.shape[1] // LANES
    step = pl.program_id(1)

    @pl.when(step == 0)
    def _():
        kbuf[0:window] = jnp.zeros((window, kbuf.shape[1]), BF16)
        vbuf[0:window] = jnp.zeros((window, vbuf.shape[1]), BF16)

    @pl.when(step > 0)
    def _():
        kbuf[0:window] = kbuf[rb:rb + window]
        vbuf[0:window] = vbuf[rb:rb + window]

    cos, sin = cos_ref[...], sin_ref[...]
    k = _rope(k_ref[...], cos, sin, head_dim)
    kr_ref[...] = k
    kbuf[window:window + rb] = k.astype(BF16)
    vbuf[window:window + rb] = v_ref[...].astype(BF16)
    q = _rope(q_ref[...], cos, sin, head_dim)

    n_buf_chunks = (window + rb) // CHUNK
    kblocks = [[_pair_blocks(kbuf[c * CHUNK:(c + 1) * CHUNK, p * LANES:(p + 1) * LANES])
                for c in range(n_buf_chunks)] for p in range(n_pairs)]
    vblocks = [[_pair_blocks(vbuf[c * CHUNK:(c + 1) * CHUNK, p * LANES:(p + 1) * LANES])
                for c in range(n_buf_chunks)] for p in range(n_pairs)]
    band_chunks = window // CHUNK + 1
    lane_key = lax.broadcasted_iota(jnp.int32, (1, LANES), 1) % CHUNK
    for c in range(rb // CHUNK):
        lo_row = c * CHUNK
        key_ok = [step * rb + (c + b) * CHUNK + lane_key >= window for b in range(band_chunks)]
        ks = [jnp.concatenate(kblocks[p][c:c + band_chunks], axis=0) for p in range(n_pairs)]
        vs = [jnp.concatenate(vblocks[p][c:c + band_chunks], axis=0) for p in range(n_pairs)]
        y = _attend(q[lo_row:lo_row + CHUNK], ks, vs, sinks_ref, key_ok, grp)
        y_ref[lo_row:lo_row + CHUNK] = (_rms(y) * g_ref[...]).astype(BF16)


def _swa_banded(proj, sinks, g, n_kv, head_dim, window, col_q):
    b, t, _ = proj.shape
    assert 2 * head_dim == LANES and CHUNK == head_dim
    n_heads = sinks.shape[0]
    att_w = n_heads * head_dim
    kv_w = n_kv * head_dim
    rb = _tile(ATTN_CHUNKS_PER_STEP * CHUNK, t)
    assert window % CHUNK == 0 and t % CHUNK == 0 and rb >= window
    cos, sin = _rope_tables(jnp.arange(t, dtype=jnp.int32), head_dim)
    col_k = (col_q + att_w) // kv_w
    kern = functools.partial(_swa_banded_kernel, grp=n_heads // n_kv, window=window)
    return pl.pallas_call(
        kern,
        grid=(b, t // rb),
        in_specs=[pl.BlockSpec(memory_space=pltpu.SMEM),
                  pl.BlockSpec((None, rb, att_w), lambda i, s: (i, s, col_q // att_w)),
                  pl.BlockSpec((None, rb, kv_w), lambda i, s: (i, s, col_k)),
                  pl.BlockSpec((None, rb, kv_w), lambda i, s: (i, s, col_k + 1)),
                  pl.BlockSpec((rb, LANES), lambda i, s: (s, 0)),
                  pl.BlockSpec((rb, LANES), lambda i, s: (s, 0)),
                  pl.BlockSpec((1, att_w), lambda i, s: (0, 0))],
        out_specs=[pl.BlockSpec((None, rb, att_w), lambda i, s: (i, s, 0)),
                   pl.BlockSpec((None, rb, kv_w), lambda i, s: (i, s, 0))],
        out_shape=[jax.ShapeDtypeStruct((b, t, att_w), BF16),
                   jax.ShapeDtypeStruct((b, t, kv_w), F32)],
        scratch_shapes=[pltpu.VMEM((window + rb, kv_w), BF16),
                        pltpu.VMEM((window + rb, kv_w), BF16)],
        compiler_params=_params(("arbitrary", "arbitrary")),
        name="swa_banded",
    )(sinks, proj, proj, proj, cos, sin, g.reshape(1, att_w))


def _swa_cached_kernel(sinks_ref, q_ref, k_ref, v_ref, ck_ref, cv_ref, cos_ref, sin_ref, g_ref,
                       y_ref, kr_ref, *, grp):
    head_dim = LANES // 2
    t, kv_w = k_ref.shape
    window = ck_ref.shape[0]
    n_pairs = kv_w // LANES
    cos, sin = cos_ref[...], sin_ref[...]
    k = _rope(k_ref[...], cos, sin, head_dim)
    kr_ref[...] = k
    q = _rope(q_ref[...], cos, sin, head_dim)
    n_keys = window + t
    pad = -n_keys % CHUNK
    tail = [jnp.zeros((pad, kv_w), BF16)] if pad else []
    kb = jnp.concatenate([ck_ref[...].astype(BF16), k.astype(BF16)] + tail, axis=0)
    vb = jnp.concatenate([cv_ref[...].astype(BF16), v_ref[...].astype(BF16)] + tail, axis=0)
    ks = [_pair_blocks(kb[:, p * LANES:(p + 1) * LANES]) for p in range(n_pairs)]
    vs = [_pair_blocks(vb[:, p * LANES:(p + 1) * LANES]) for p in range(n_pairs)]
    lane_key = lax.broadcasted_iota(jnp.int32, (1, LANES), 1) % CHUNK
    key_ok = [c * CHUNK + lane_key < n_keys for c in range((n_keys + pad) // CHUNK)]
    y = _attend(q, ks, vs, sinks_ref, key_ok, grp)
    y_ref[...] = (_rms(y) * g_ref[...]).astype(BF16)


def _swa_cached(proj, cache_k, cache_v, sinks, g, n_kv, head_dim, col_q):
    b, t, _ = proj.shape
    assert 2 * head_dim == LANES and CHUNK == head_dim
    window = cache_k.shape[1]
    n_heads = sinks.shape[0]
    att_w = n_heads * head_dim
    kv_w = n_kv * head_dim
    cos, sin = _rope_tables(PAST_LEN + jnp.arange(t, dtype=jnp.int32), head_dim)
    col_k = (col_q + att_w) // kv_w
    kern = functools.partial(_swa_cached_kernel, grp=n_heads // n_kv)
    return pl.pallas_call(
        kern,
        grid=(b,),
        in_specs=[pl.BlockSpec(memory_space=pltpu.SMEM),
                  pl.BlockSpec((None, t, att_w), lambda i: (i, 0, col_q // att_w)),
                  pl.BlockSpec((None, t, kv_w), lambda i: (i, 0, col_k)),
                  pl.BlockSpec((None, t, kv_w), lambda i: (i, 0, col_k + 1)),
                  pl.BlockSpec((None, window, kv_w), lambda i: (i, 0, 0)),
                  pl.BlockSpec((None, window, kv_w), lambda i: (i, 0, 0)),
                  pl.BlockSpec((t, LANES), lambda i: (0, 0)),
                  pl.BlockSpec((t, LANES), lambda i: (0, 0)),
                  pl.BlockSpec((1, att_w), lambda i: (0, 0))],
        out_specs=[pl.BlockSpec((None, t, att_w), lambda i: (i, 0, 0)),
                   pl.BlockSpec((None, t, kv_w), lambda i: (i, 0, 0))],
        out_shape=[jax.ShapeDtypeStruct((b, t, att_w), BF16),
                   jax.ShapeDtypeStruct((b, t, kv_w), F32)],
        compiler_params=_params(("arbitrary",)),
        name="swa_cached",
    )(sinks, proj, proj, proj, cache_k, cache_v, cos, sin, g.reshape(1, att_w))


def _outproj_kernel(yc_ref, ya_ref, wt_ref, wb_ref, b_ref, x_ref, g1_ref, o_ref):
    bb, tt, c = yc_ref.shape
    acc = jnp.dot(yc_ref[...].reshape(bb * tt, c), wt_ref[...], preferred_element_type=F32)
    acc = acc + jnp.dot(ya_ref[...].reshape(bb * tt, c), wb_ref[...], preferred_element_type=F32)
    acc = acc + b_ref[...]
    o_ref[...] = x_ref[...] + g1_ref[...] * acc.reshape(o_ref.shape)


def _outproj(yc, ya, w_bf, b_out, x, mod):
    b, t, d = x.shape
    c = yc.shape[2]
    assert 2 * c == w_bf.shape[0]
    bb, tt = _token_blocks(b, t, ROW_TILE)
    tn = _tile(OUTPROJ_COL_TILE, d)
    return pl.pallas_call(
        _outproj_kernel,
        grid=(b // bb, t // tt, d // tn),
        in_specs=[pl.BlockSpec((bb, tt, c), lambda i, s, j: (i, s, 0)),
                  pl.BlockSpec((bb, tt, c), lambda i, s, j: (i, s, 0)),
                  pl.BlockSpec((c, tn), lambda i, s, j: (0, j)),
                  pl.BlockSpec((c, tn), lambda i, s, j: (1, j)),
                  pl.BlockSpec((1, tn), lambda i, s, j: (0, j)),
                  pl.BlockSpec((bb, tt, tn), lambda i, s, j: (i, s, j)),
                  pl.BlockSpec((bb, 1, tn), lambda i, s, j: (i, 0, 2 * (d // tn) + j))],
        out_specs=pl.BlockSpec((bb, tt, tn), lambda i, s, j: (i, s, j)),
        out_shape=jax.ShapeDtypeStruct((b, t, d), F32),
        compiler_params=_params(("arbitrary", "arbitrary", "arbitrary")),
        name="outproj",
    )(yc, ya, w_bf, w_bf, b_out.reshape(1, d), x, mod)


def _router_kernel(x_ref, sh_ref, sc_ref, g_ref, wr_ref, br_ref, c0_ref,
                   hp_ref, idx_ref, gate_ref, rank_ref, cnt_ref, carry):
    bb, tt, d = x_ref.shape
    rows = bb * tt
    n_exp = wr_ref.shape[1]

    @pl.when((pl.program_id(0) == 0) & (pl.program_id(1) == 0))
    def _():
        carry[...] = c0_ref[...]

    y = _rms(x_ref[...]) * g_ref[...]
    h = (y * (1.0 + sc_ref[...]) + sh_ref[...]).reshape(rows, d)
    bits = lax.bitcast_convert_type(h.astype(BF16).astype(F32), U32)
    hp_ref[...] = (bits[:, d // 2:] & jnp.uint32(0xFFFF0000)) | (bits[:, :d // 2] >> 16)

    logits = jnp.dot(h, wr_ref[...], preferred_element_type=F32,
                     precision=lax.Precision.HIGHEST) + br_ref[...]
    col = lax.broadcasted_iota(jnp.int32, logits.shape, 1).astype(F32)
    work = logits
    vals, idxs = [], []
    for _ in range(TOP_K):
        m = jnp.max(work, axis=-1, keepdims=True)
        sel = jnp.min(jnp.where(work == m, col, float(n_exp)), axis=-1, keepdims=True)
        vals.append(m)
        idxs.append(sel)
        work = jnp.where(col == sel, -jnp.inf, work)
    exps = [jnp.exp(v - vals[0]) for v in vals]
    den = exps[0]
    for e in exps[1:]:
        den = den + e
    idx_ref[...] = jnp.concatenate(idxs, axis=1).astype(jnp.int32)
    gate_ref[...] = jnp.concatenate([e / den for e in exps], axis=1)

    onehots = [(col == sel).astype(F32) for sel in idxs]
    cnt = onehots[0]
    for oh in onehots[1:]:
        cnt = cnt + oh
    earlier = (lax.broadcasted_iota(jnp.int32, (rows, rows), 0)
               > lax.broadcasted_iota(jnp.int32, (rows, rows), 1)).astype(BF16)
    before = jnp.dot(earlier, cnt.astype(BF16), preferred_element_type=F32) + carry[...]
    ranks = [jnp.sum(oh * before, axis=-1, keepdims=True) for oh in onehots]
    rank_ref[...] = jnp.concatenate(ranks, axis=1).astype(jnp.int32)
    carry[...] = carry[...] + jnp.sum(cnt, axis=0, keepdims=True)
    cnt_ref[...] = carry[...]


def _router(x1, mod, g, w_router, b_router, counts_before):
    b, t, d = x1.shape
    n_exp = w_router.shape[1]
    bb, tt = _token_blocks(b, t, FFN_ROW_TILE)
    rows = bb * tt
    n_tok = b * t
    flat = lambda i, s: (i * (t // tt) + s, 0)
    return pl.pallas_call(
        _router_kernel,
        grid=(b // bb, t // tt),
        in_specs=[pl.BlockSpec((bb, tt, d), lambda i, s: (i, s, 0)),
                  pl.BlockSpec((bb, 1, d), lambda i, s: (i, 0, 3)),
                  pl.BlockSpec((bb, 1, d), lambda i, s: (i, 0, 4)),
                  pl.BlockSpec((1, d), lambda i, s: (0, 0)),
                  pl.BlockSpec((d, n_exp), lambda i, s: (0, 0)),
                  pl.BlockSpec((1, n_exp), lambda i, s: (0, 0)),
                  pl.BlockSpec((1, n_exp), lambda i, s: (0, 0))],
        out_specs=[pl.BlockSpec((rows, d // 2), flat),
                   pl.BlockSpec((rows, TOP_K), flat),
                   pl.BlockSpec((rows, TOP_K), flat),
                   pl.BlockSpec((rows, TOP_K), flat),
                   pl.BlockSpec((1, n_exp), lambda i, s: (0, 0))],
        out_shape=[jax.ShapeDtypeStruct((n_tok, d // 2), U32),
                   jax.ShapeDtypeStruct((n_tok, TOP_K), jnp.int32),
                   jax.ShapeDtypeStruct((n_tok, TOP_K), F32),
                   jax.ShapeDtypeStruct((n_tok, TOP_K), jnp.int32),
                   jax.ShapeDtypeStruct((1, n_exp), F32)],
        scratch_shapes=[pltpu.VMEM((1, n_exp), F32)],
        compiler_params=_params(("arbitrary", "arbitrary")),
        name="router",
    )(x1, mod, mod, g.reshape(1, d), w_router, b_router.reshape(1, n_exp), counts_before)


def _issue_row_copies(idx_tile_ref, idx_smem, sem_idx, src_ref, dst_ref, sem):
    cp = pltpu.make_async_copy(idx_tile_ref.at[0, 0], idx_smem, sem_idx)
    cp.start()
    cp.wait()

    def body(r, carry):
        pltpu.make_async_copy(src_ref.at[pl.ds(idx_smem[r], 1)], dst_ref.at[pl.ds(r, 1)], sem).start()
        return carry

    lax.fori_loop(0, idx_smem.shape[0], body, 0, unroll=8)


def _wait_row_copies(src_ref, dst_ref, sem):
    pltpu.make_async_copy(src_ref.at[pl.ds(0, dst_ref.shape[0])], dst_ref, sem).wait()


def _gather_kernel(nused_ref, idx0_ref, idxn_ref, src_ref, o_ref, idx_smem, buf, sem_idx, sem_rows):
    i = pl.program_id(0)
    n = nused_ref[0]
    half = src_ref.shape[1]

    @pl.when(i == 0)
    def _():
        _issue_row_copies(idx0_ref, idx_smem, sem_idx, src_ref, buf.at[0], sem_rows.at[0])

    @pl.when(i + 1 < n)
    def _():
        nxt = (i + 1) % 2
        _issue_row_copies(idxn_ref, idx_smem, sem_idx, src_ref, buf.at[nxt], sem_rows.at[nxt])

    @pl.when(i < n)
    def _():
        slot = i % 2
        _wait_row_copies(src_ref, buf.at[slot], sem_rows.at[slot])
        w = buf[slot]
        o_ref[:, :half] = lax.bitcast_convert_type(w << 16, F32).astype(BF16)
        o_ref[:, half:] = lax.bitcast_convert_type(w & jnp.uint32(0xFFFF0000), F32).astype(BF16)


def _gather_rows(src_packed, row_src, n_used, rows):
    p = row_src.shape[0]
    half = src_packed.shape[1]
    nblk = p // rows
    clamp = lambda i, nu: jnp.minimum(i, nu[0] - 1)
    return pl.pallas_call(
        _gather_kernel,
        grid_spec=pltpu.PrefetchScalarGridSpec(
            num_scalar_prefetch=1,
            grid=(nblk,),
            in_specs=[pl.BlockSpec((1, 1, rows), lambda i, nu: (0, 0, 0)),
                      pl.BlockSpec((1, 1, rows), lambda i, nu: (clamp(i + 1, nu), 0, 0)),
                      pl.BlockSpec(memory_space=pl.ANY)],
            out_specs=pl.BlockSpec((rows, 2 * half), lambda i, nu: (clamp(i, nu), 0)),
            scratch_shapes=[pltpu.SMEM((rows,), jnp.int32),
                            pltpu.VMEM((2, rows, half), U32),
                            pltpu.SemaphoreType.DMA(()),
                            pltpu.SemaphoreType.DMA((2,))]),
        out_shape=jax.ShapeDtypeStruct((p, 2 * half), BF16),
        compiler_params=_params(("arbitrary",)),
        name="moe_gather",
    )(n_used, row_src.reshape(nblk, 1, rows), row_src.reshape(nblk, 1, rows), src_packed)


def _stream_expert_tiles(first_ref, ntile_ref, ntot_ref, src_hbm, dst_hbm, xbuf, obuf, sem_in, sem_out, compute):
    j, e = pl.program_id(0), pl.program_id(1)
    nj, ne = pl.num_programs(0), pl.num_programs(1)
    rows = xbuf.shape[1]
    tn = obuf.shape[2]
    n_total = ntot_ref[0]

    def x_copy(t, slot):
        r0 = pl.multiple_of(t * rows, rows)
        return pltpu.make_async_copy(src_hbm.at[pl.ds(r0, rows)], xbuf.at[slot], sem_in.at[slot])

    def o_copy(t, slot):
        r0 = pl.multiple_of(t * rows, rows)
        c0 = pl.multiple_of(j * tn, tn)
        return pltpu.make_async_copy(obuf.at[slot], dst_hbm.at[pl.ds(r0, rows), pl.ds(c0, tn)], sem_out.at[slot])

    @pl.when((j == 0) & (e == 0))
    def _():
        x_copy(0, 0).start()

    def body(t, carry):
        seq = j * n_total + t
        slot = seq % 2
        x_copy(t, slot).wait()
        wraps = t + 1 == n_total

        @pl.when(jnp.logical_not(wraps & (j == nj - 1)))
        def _():
            x_copy(jnp.where(wraps, 0, t + 1), 1 - slot).start()

        @pl.when(seq >= 2)
        def _():
            o_copy(t, slot).wait()

        obuf[slot] = compute(xbuf[slot])
        o_copy(t, slot).start()
        return carry

    t0 = first_ref[e]
    lax.fori_loop(t0, t0 + ntile_ref[e], body, 0)

    @pl.when((j == nj - 1) & (e == ne - 1))
    def _():
        n_seq = nj * n_total

        @pl.when(n_seq >= 2)
        def _():
            o_copy(0, n_seq % 2).wait()

        o_copy(0, (n_seq - 1) % 2).wait()


def _moe_up_kernel(first_ref, ntile_ref, eff_ref, ntot_ref, x_hbm, wg_ref, wu_ref, bg_ref, bu_ref, act_hbm,
                   xbuf, obuf, wg_bf, wu_bf, sem_in, sem_out):
    @pl.when(ntile_ref[pl.program_id(1)] > 0)
    def _():
        wg_bf[...] = wg_ref[...].astype(BF16)
        wu_bf[...] = wu_ref[...].astype(BF16)

    def compute(x):
        g = jnp.dot(x, wg_bf[...], preferred_element_type=F32) + bg_ref[...]
        u = jnp.dot(x, wu_bf[...], preferred_element_type=F32) + bu_ref[...]
        g = jnp.minimum(g, SWIGLU_LIMIT)
        u = jnp.clip(u, -SWIGLU_LIMIT, SWIGLU_LIMIT)
        return ((u + 1.0) * g * jax.nn.sigmoid(SWIGLU_ALPHA * g)).astype(BF16)

    _stream_expert_tiles(first_ref, ntile_ref, ntot_ref, x_hbm, act_hbm, xbuf, obuf, sem_in, sem_out, compute)


def _moe_up(xs, w_gate_up, b_gate_up, tile_first, tile_count, eff_expert, n_tiles):
    p, d = xs.shape
    n_exp, _, f2 = w_gate_up.shape
    f = f2 // 2
    tf = _tile(MOE_FF_TILE, f)
    nj = f // tf
    return pl.pallas_call(
        _moe_up_kernel,
        grid_spec=pltpu.PrefetchScalarGridSpec(
            num_scalar_prefetch=4,
            grid=(nj, n_exp),
            in_specs=[pl.BlockSpec(memory_space=pl.ANY),
                      pl.BlockSpec((None, d, tf), lambda j, e, fi, nt, ef, ntot: (ef[e], 0, j)),
                      pl.BlockSpec((None, d, tf), lambda j, e, fi, nt, ef, ntot: (ef[e], 0, nj + j)),
                      pl.BlockSpec((None, 1, tf), lambda j, e, fi, nt, ef, ntot: (ef[e], 0, j)),
                      pl.BlockSpec((None, 1, tf), lambda j, e, fi, nt, ef, ntot: (ef[e], 0, nj + j))],
            out_specs=pl.BlockSpec(memory_space=pl.ANY),
            scratch_shapes=[pltpu.VMEM((2, MOE_ROWS, d), BF16),
                            pltpu.VMEM((2, MOE_ROWS, tf), BF16),
                            pltpu.VMEM((d, tf), BF16),
                            pltpu.VMEM((d, tf), BF16),
                            pltpu.SemaphoreType.DMA((2,)),
                            pltpu.SemaphoreType.DMA((2,))]),
        out_shape=jax.ShapeDtypeStruct((p, f), BF16),
        compiler_params=_params(("arbitrary", "arbitrary")),
        name="moe_up",
    )(tile_first, tile_count, eff_expert, n_tiles, xs, w_gate_up, w_gate_up,
      b_gate_up.reshape(n_exp, 1, f2), b_gate_up.reshape(n_exp, 1, f2))


def _moe_down_kernel(first_ref, ntile_ref, eff_ref, ntot_ref, a_hbm, w_ref, b_ref, y_hbm,
                     xbuf, obuf, w_bf, sem_in, sem_out):
    @pl.when(ntile_ref[pl.program_id(1)] > 0)
    def _():
        w_bf[...] = w_ref[...].astype(BF16)

    def compute(a):
        return jnp.dot(a, w_bf[...], preferred_element_type=F32) + b_ref[...]

    _stream_expert_tiles(first_ref, ntile_ref, ntot_ref, a_hbm, y_hbm, xbuf, obuf, sem_in, sem_out, compute)


def _moe_down(act, w_down, b_down, tile_first, tile_count, eff_expert, n_tiles):
    p, f = act.shape
    n_exp, _, d = w_down.shape
    tn = _tile(MOE_OUT_TILE, d)
    return pl.pallas_call(
        _moe_down_kernel,
        grid_spec=pltpu.PrefetchScalarGridSpec(
            num_scalar_prefetch=4,
            grid=(d // tn, n_exp),
            in_specs=[pl.BlockSpec(memory_space=pl.ANY),
                      pl.BlockSpec((None, f, tn), lambda j, e, fi, nt, ef, ntot: (ef[e], 0, j)),
                      pl.BlockSpec((None, 1, tn), lambda j, e, fi, nt, ef, ntot: (ef[e], 0, j))],
            out_specs=pl.BlockSpec(memory_space=pl.ANY),
            scratch_shapes=[pltpu.VMEM((2, MOE_ROWS, f), BF16),
                            pltpu.VMEM((2, MOE_ROWS, tn), F32),
                            pltpu.VMEM((f, tn), BF16),
                            pltpu.SemaphoreType.DMA((2,)),
                            pltpu.SemaphoreType.DMA((2,))]),
        out_shape=jax.ShapeDtypeStruct((p, d), F32),
        compiler_params=_params(("arbitrary", "arbitrary")),
        name="moe_down",
    )(tile_first, tile_count, eff_expert, n_tiles, act, w_down, b_down.reshape(n_exp, 1, d))


def _combine_kernel(pos0_ref, posn_ref, y_ref, gate_ref, x_ref, g2_ref, gf_ref, o_ref,
                    pos_smem, buf, sem_idx, sem_rows):
    bb, tt, d = x_ref.shape
    rows = bb * tt
    i = pl.program_id(0)

    @pl.when(i == 0)
    def _():
        _issue_row_copies(pos0_ref, pos_smem, sem_idx, y_ref, buf.at[0], sem_rows.at[0])

    @pl.when(i + 1 < pl.num_programs(0))
    def _():
        nxt = (i + 1) % 2
        _issue_row_copies(posn_ref, pos_smem, sem_idx, y_ref, buf.at[nxt], sem_rows.at[nxt])

    slot = i % 2
    _wait_row_copies(y_ref, buf.at[slot], sem_rows.at[slot])
    gates = gate_ref[...]
    ffn = gates[:, 0:1] * buf[slot, 0:rows]
    for k in range(1, TOP_K):
        ffn = ffn + gates[:, k:k + 1] * buf[slot, k * rows:(k + 1) * rows]
    x2 = x_ref[...] + g2_ref[...] * ffn.reshape(bb, tt, d)
    o_ref[...] = _rms(x2) * gf_ref[...]


def _combine(y_sorted, pos, gates, x1, mod, gf, tok_offset):
    b, t, d = x1.shape
    bb, tt = _token_blocks(b, t, COMBINE_ROWS)
    rows = bb * tt
    assert tok_offset % rows == 0
    blk0 = tok_offset // rows
    n_tok = pos.shape[0]
    nblk = (b // bb) * (t // tt)
    ts = t // tt
    pos_tiles = pos.reshape(n_tok // rows, rows, TOP_K).transpose(0, 2, 1).reshape(n_tok // rows, 1, TOP_K * rows)
    return pl.pallas_call(
        _combine_kernel,
        grid=(nblk,),
        in_specs=[pl.BlockSpec((1, 1, TOP_K * rows), lambda i: (blk0, 0, 0)),
                  pl.BlockSpec((1, 1, TOP_K * rows), lambda i: (blk0 + jnp.minimum(i + 1, nblk - 1), 0, 0)),
                  pl.BlockSpec(memory_space=pl.ANY),
                  pl.BlockSpec((rows, TOP_K), lambda i: (blk0 + i, 0)),
                  pl.BlockSpec((bb, tt, d), lambda i: (i // ts, i % ts, 0)),
                  pl.BlockSpec((bb, 1, d), lambda i: (i // ts, 0, 5)),
                  pl.BlockSpec((1, d), lambda i: (0, 0))],
        out_specs=pl.BlockSpec((bb, tt, d), lambda i: (i // ts, i % ts, 0)),
        out_shape=jax.ShapeDtypeStruct((b, t, d), F32),
        scratch_shapes=[pltpu.SMEM((TOP_K * rows,), jnp.int32),
                        pltpu.VMEM((2, TOP_K * rows, d), F32),
                        pltpu.SemaphoreType.DMA(()),
                        pltpu.SemaphoreType.DMA((2,))],
        compiler_params=_params(("arbitrary",)),
        name="moe_combine",
    )(pos_tiles, pos_tiles, y_sorted, gates, x1, mod, gf.reshape(1, d))


def _routing(top_idx, rank, counts, rows):
    n_tok = top_idx.shape[0]
    n_exp = counts.shape[0]
    n_assign = n_tok * TOP_K
    tile_count = (counts + rows - 1) // rows
    tile_end = jnp.cumsum(tile_count)
    tile_first = tile_end - tile_count
    pos = (tile_first * rows)[top_idx] + rank
    p = n_assign + n_exp * rows
    token_of = jnp.arange(n_assign, dtype=jnp.int32) // TOP_K
    row_token = jnp.zeros((p,), jnp.int32).at[pos.reshape(-1)].set(token_of)
    ids = jnp.where(tile_count > 0, jnp.arange(n_exp, dtype=jnp.int32), -1)
    eff = lax.cummax(ids)
    eff = jnp.where(eff < 0, jnp.argmax(tile_count > 0).astype(jnp.int32), eff)
    i32 = lambda a: a.astype(jnp.int32)
    return i32(pos), row_token, i32(tile_first), i32(tile_count), i32(eff), i32(tile_end[-1:])


def kernel(x_prompt, x_sample, cache_k, cache_v, state_conv, c_prompt, c_sample, w_ada, b_ada, norm_mix_g, norm_ffn_g, norm_final_g, w_in, b_in, conv_w, conv_b, sinks, out_norm_conv_g, out_norm_attn_g, w_out, b_out, w_router, b_router, w_gate_up, b_gate_up, w_down, b_down):
    d = x_prompt.shape[2]
    depth, dec_b, window, n_kv, head_dim = cache_k.shape
    assert depth == 1
    conv_ch = conv_w.shape[1]
    n_exp = w_router.shape[1]
    n_heads = sinks.shape[0]
    bp, tp, _ = x_prompt.shape
    bs, ts, _ = x_sample.shape
    col_q = 3 * conv_ch
    kv_w = n_kv * head_dim
    att_w = n_heads * head_dim

    mod = _adaln(jnp.concatenate([c_prompt, c_sample], axis=0), w_ada, b_ada)
    mod_p = mod[:bp].reshape(bp, 1, 6 * d)
    mod_s = mod[bp:].reshape(bs, 1, 6 * d)
    order = _head_order(n_heads, n_kv)
    w_in_bf = _permute_blocks(w_in, order, head_dim, 1, col_q).astype(BF16)
    b_in_p = _permute_blocks(b_in, order, head_dim, 0, col_q)
    w_out_bf = _permute_blocks(w_out, order, head_dim, 0, conv_ch).astype(BF16)
    attn_g = _permute_blocks(out_norm_attn_g, order, head_dim, 0, 0)

    def mixer(x, modx, conv_prev, ck, cv, counts_before):
        proj = _inproj(x, modx, norm_mix_g, w_in_bf, b_in_p)
        yc, conv_state = _conv_mix(proj, conv_prev, conv_w, conv_b, out_norm_conv_g)
        if ck is None:
            ya, k_rot = _swa_banded(proj, sinks, attn_g, n_kv, head_dim, window, col_q)
        else:
            ya, k_rot = _swa_cached(proj, ck, cv, sinks, attn_g, n_kv, head_dim, col_q)
        v_rows = proj[:, :, col_q + att_w + kv_w:]
        x1 = _outproj(yc, ya, w_out_bf, b_out, x, modx)
        routed = _router(x1, modx, norm_ffn_g, w_router, b_router, counts_before)
        return x1, routed, conv_state, k_rot, v_rows

    zero_state = jnp.zeros((bp, 2, conv_ch), F32)
    x1p, (hpp, idxp, gatep, rankp, cntp), conv_p, kp, vp = mixer(
        x_prompt, mod_p, zero_state, None, None, jnp.zeros((1, n_exp), F32))
    x1s, (hps, idxs, gates_s, ranks, cnt), conv_s, ksr, vsr = mixer(
        x_sample, mod_s, state_conv[0],
        cache_k[0].reshape(dec_b, window, kv_w), cache_v[0].reshape(dec_b, window, kv_w), cntp)

    n_p = bp * tp
    h2_packed = jnp.concatenate([hpp, hps], axis=0)
    top_idx = jnp.concatenate([idxp, idxs], axis=0)
    gates = jnp.concatenate([gatep, gates_s], axis=0)
    rank = jnp.concatenate([rankp, ranks], axis=0)
    pos, row_token, tile_first, tile_count, eff, n_tiles = _routing(
        top_idx, rank, cnt[0].astype(jnp.int32), MOE_ROWS)
    xs = _gather_rows(h2_packed, row_token, n_tiles, MOE_ROWS)
    act = _moe_up(xs, w_gate_up, b_gate_up, tile_first, tile_count, eff, n_tiles)
    y_sorted = _moe_down(act, w_down, b_down, tile_first, tile_count, eff, n_tiles)
    y_prompt = _combine(y_sorted, pos, gates, x1p, mod_p, norm_final_g, 0)
    y_sample = _combine(y_sorted, pos, gates, x1s, mod_s, norm_final_g, n_p)

    k_prompt = kp[:, tp - window:].reshape(1, bp, window, n_kv, head_dim)
    v_prompt = vp[:, tp - window:].reshape(1, bp, window, n_kv, head_dim)
    k_sample = ksr.reshape(1, bs, ts, n_kv, head_dim)
    v_sample = vsr.reshape(1, bs, ts, n_kv, head_dim)
    return (y_prompt, y_sample, k_prompt, v_prompt, conv_p[None], k_sample, v_sample, conv_s[None])
```

```python
import functools

import jax
import jax.numpy as jnp
from jax import lax
from jax.experimental import pallas as pl
from jax.experimental.pallas import tpu as pltpu

F32 = jnp.float32
BF16 = jnp.bfloat16
U32 = jnp.uint32

CHUNK = 64
TOP_K = 4
PAST_LEN = 2048
ROPE_THETA = 10000.0
NORM_EPS = 1e-5
SWIGLU_LIMIT = 7.0
SWIGLU_ALPHA = 1.702

LANES = 128
SUBLANES = 8
VMEM_LIMIT_BYTES = 58 * 1024 * 1024

ROW_TILE = 512
ADA_COL_TILE = 512
INPROJ_COL_TILE = 1024
OUTPROJ_COL_TILE = 1024
CONV_ROW_TILE = 256
ATTN_CHUNKS_PER_STEP = 4
FFN_ROW_TILE = 256
MOE_ROWS = 256
MOE_FF_TILE = 512
MOE_OUT_TILE = 1024
COMBINE_ROWS = 128
ROW_COPY_UNROLL = 8
TILE_COPY_PRIORITY = 1


def _tile(pref, dim):
    t = min(pref, dim)
    while dim % t:
        t -= 1
    return t


def _params(sem):
    return pltpu.CompilerParams(dimension_semantics=sem, vmem_limit_bytes=VMEM_LIMIT_BYTES)


def _rms(x):
    return x * lax.rsqrt(jnp.mean(x * x, axis=-1, keepdims=True) + NORM_EPS)


def _token_blocks(b, t, rows_pref):
    if t >= rows_pref:
        return 1, _tile(rows_pref, t)
    return _tile(max(rows_pref // t, 1), b), t


def _adaln_kernel(c_ref, w_ref, b_ref, o_ref):
    c = c_ref[...]
    a = (c * jax.nn.sigmoid(c)).astype(BF16)
    o_ref[...] = jnp.dot(a, w_ref[...].astype(BF16), preferred_element_type=F32) + b_ref[...]


def _adaln(c, w_ada, b_ada):
    nb, d = c.shape
    n = w_ada.shape[1]
    tn = _tile(ADA_COL_TILE, n)
    return pl.pallas_call(
        _adaln_kernel,
        grid=(n // tn,),
        in_specs=[pl.BlockSpec((nb, d), lambda j: (0, 0)),
                  pl.BlockSpec((d, tn), lambda j: (0, j)),
                  pl.BlockSpec((1, tn), lambda j: (0, j))],
        out_specs=pl.BlockSpec((nb, tn), lambda j: (0, j)),
        out_shape=jax.ShapeDtypeStruct((nb, n), F32),
        compiler_params=_params(("arbitrary",)),
        name="adaln",
    )(c, w_ada, b_ada.reshape(1, n))


def _inproj_kernel(x_ref, sh_ref, sc_ref, g_ref, w_ref, b_ref, o_ref, h_ref):
    bb, tt, d = x_ref.shape

    @pl.when(pl.program_id(2) == 0)
    def _():
        y = _rms(x_ref[...]) * g_ref[...]
        h = y * (1.0 + sc_ref[...]) + sh_ref[...]
        h_ref[...] = h.reshape(bb * tt, d).astype(BF16)

    acc = jnp.dot(h_ref[...], w_ref[...], preferred_element_type=F32) + b_ref[...]
    o_ref[...] = acc.reshape(o_ref.shape)


def _inproj(x, mod, g, w_bf, b_in):
    b, t, d = x.shape
    n = w_bf.shape[1]
    bb, tt = _token_blocks(b, t, ROW_TILE)
    tn = _tile(INPROJ_COL_TILE, n)
    return pl.pallas_call(
        _inproj_kernel,
        grid=(b // bb, t // tt, n // tn),
        in_specs=[pl.BlockSpec((bb, tt, d), lambda i, s, j: (i, s, 0)),
                  pl.BlockSpec((bb, 1, d), lambda i, s, j: (i, 0, 0)),
                  pl.BlockSpec((bb, 1, d), lambda i, s, j: (i, 0, 1)),
                  pl.BlockSpec((1, d), lambda i, s, j: (0, 0)),
                  pl.BlockSpec((d, tn), lambda i, s, j: (0, j)),
                  pl.BlockSpec((1, tn), lambda i, s, j: (0, j))],
        out_specs=pl.BlockSpec((bb, tt, tn), lambda i, s, j: (i, s, j)),
        out_shape=jax.ShapeDtypeStruct((b, t, n), F32),
        scratch_shapes=[pltpu.VMEM((bb * tt, d), BF16)],
        compiler_params=_params(("arbitrary", "arbitrary", "arbitrary")),
        name="inproj",
    )(x, mod, mod, g.reshape(1, d), w_bf, b_in.reshape(1, n))


def _conv_kernel(bg_ref, cg_ref, xc_ref, cgp_ref, xcp_ref, st_ref, w_ref, b_ref, g_ref, y_ref, so_ref):
    tt = cg_ref.shape[0]
    u = cg_ref[...] * xc_ref[...]
    halo = cgp_ref[...] * xcp_ref[...]
    st = st_ref[...]
    first = pl.program_id(1) == 0
    um2 = jnp.where(first, st[0:1], halo[SUBLANES - 2:SUBLANES - 1])
    um1 = jnp.where(first, st[1:2], halo[SUBLANES - 1:SUBLANES])
    row = lax.broadcasted_iota(jnp.int32, u.shape, 0)
    u1 = jnp.where(row == 0, um1, pltpu.roll(u, 1, 0))
    u2 = jnp.where(row == 0, um2, jnp.where(row == 1, um1, pltpu.roll(u, 2, 0)))
    w = w_ref[...]
    conv = b_ref[...] + u2 * w[0:1]
    conv = conv + u1 * w[1:2]
    conv = conv + u * w[2:3]
    y = _rms(bg_ref[...] * conv) * g_ref[...]
    y_ref[...] = y.astype(BF16)
    so_ref[...] = u[tt - 2:tt]


def _conv_mix(proj, state, conv_w, conv_b, g):
    b, t, _ = proj.shape
    c = conv_w.shape[1]
    assert conv_w.shape[0] == 3
    tt = _tile(CONV_ROW_TILE, t)
    hs = tt // SUBLANES
    halo_map = lambda col: (lambda i, s: (i, jnp.maximum(s * hs - 1, 0), col))
    return pl.pallas_call(
        _conv_kernel,
        grid=(b, t // tt),
        in_specs=[pl.BlockSpec((None, tt, c), lambda i, s: (i, s, 0)),
                  pl.BlockSpec((None, tt, c), lambda i, s: (i, s, 1)),
                  pl.BlockSpec((None, tt, c), lambda i, s: (i, s, 2)),
                  pl.BlockSpec((None, SUBLANES, c), halo_map(1)),
                  pl.BlockSpec((None, SUBLANES, c), halo_map(2)),
                  pl.BlockSpec((None, 2, c), lambda i, s: (i, 0, 0)),
                  pl.BlockSpec((3, c), lambda i, s: (0, 0)),
                  pl.BlockSpec((1, c), lambda i, s: (0, 0)),
                  pl.BlockSpec((1, c), lambda i, s: (0, 0))],
        out_specs=[pl.BlockSpec((None, tt, c), lambda i, s: (i, s, 0)),
                   pl.BlockSpec((None, 2, c), lambda i, s: (i, 0, 0))],
        out_shape=[jax.ShapeDtypeStruct((b, t, c), BF16),
                   jax.ShapeDtypeStruct((b, 2, c), F32)],
        compiler_params=_params(("arbitrary", "arbitrary")),
        name="conv_mix",
    )(proj, proj, proj, proj, proj, state, conv_w, conv_b.reshape(1, c), g.reshape(1, c))


def _head_order(n_heads, n_kv):
    grp = n_heads // n_kv
    assert n_kv % 2 == 0
    return [(2 * p + half) * grp + g for p in range(n_kv // 2) for g in range(grp) for half in (0, 1)]


def _permute_blocks(x, order, width, axis, start):
    n = len(order) * width
    take = lambda lo, hi: lax.slice_in_dim(x, lo, hi, axis=axis)
    parts = [take(0, start)] + [take(start + h * width, start + (h + 1) * width) for h in order]
    parts.append(take(start + n, x.shape[axis]))
    return jnp.concatenate(parts, axis=axis)


def _rope_tables(pos, head_dim):
    half = head_dim // 2
    inv_freq = ROPE_THETA ** (-jnp.arange(half, dtype=F32) / half)
    ang = pos.astype(F32)[:, None] * inv_freq[None, :]
    cos, sin = jnp.cos(ang), jnp.sin(ang)
    reps = LANES // head_dim
    return (jnp.tile(jnp.concatenate([cos, cos], axis=-1), (1, reps)),
            jnp.tile(jnp.concatenate([-sin, sin], axis=-1), (1, reps)))


def _rope(x, cos, sin, head_dim):
    width = x.shape[1]
    half = head_dim // 2
    reps = width // LANES
    cos = jnp.concatenate([cos] * reps, axis=1)
    sin = jnp.concatenate([sin] * reps, axis=1)
    lane = lax.broadcasted_iota(jnp.int32, x.shape, 1)
    swapped = jnp.where(lane % head_dim < half,
                        pltpu.roll(x, width - half, 1), pltpu.roll(x, half, 1))
    return x * cos + swapped * sin


def _pair_blocks(slab):
    keys = slab.shape[0]
    lo = lax.broadcasted_iota(jnp.int32, (CHUNK, LANES), 1) < LANES // 2
    zero = jnp.zeros((CHUNK, LANES), slab.dtype)
    parts = []
    for c in range(keys // CHUNK):
        blk = slab[c * CHUNK:(c + 1) * CHUNK]
        parts += [jnp.where(lo, blk, zero), jnp.where(lo, zero, blk)]
    return jnp.concatenate(parts, axis=0)


def _attend_pair(qs, kblk, vblk, sink_vec, key_ok):
    head_dim = LANES // 2
    n_chunks = kblk.shape[0] // LANES
    lo = lax.broadcasted_iota(jnp.int32, (1, LANES), 1) < head_dim
    s = lax.dot_general(qs.astype(BF16), kblk, (((1,), (1,)), ((), ())),
                        preferred_element_type=F32) * (head_dim ** -0.5)
    tiles = [jnp.where(key_ok[c], s[:, c * LANES:(c + 1) * LANES], -jnp.inf) for c in range(n_chunks)]

    def per_half(t, reduce_fn, fill):
        a = reduce_fn(jnp.where(lo, t, fill), axis=-1, keepdims=True)
        b = reduce_fn(jnp.where(lo, fill, t), axis=-1, keepdims=True)
        return jnp.where(lo, a, b)

    tmax = tiles[0]
    for t in tiles[1:]:
        tmax = jnp.maximum(tmax, t)
    m = jnp.maximum(per_half(tmax, jnp.max, -jnp.inf), sink_vec)
    es = [jnp.exp(t - m) for t in tiles]
    esum = es[0]
    for e in es[1:]:
        esum = esum + e
    den = per_half(esum, jnp.sum, 0.0) + jnp.exp(sink_vec - m)
    e_all = jnp.concatenate([e.astype(BF16) for e in es], axis=1)
    return jnp.dot(e_all, vblk, preferred_element_type=F32) / den


def _attend(q, kslabs, vslabs, sinks_ref, key_ok, grp):
    rows = q.shape[0]
    head_dim = LANES // 2
    lo = lax.broadcasted_iota(jnp.int32, (1, LANES), 1) < head_dim
    outs = []
    for p in range(len(kslabs)):
        qs = jnp.concatenate([q[:, (p * grp + g) * LANES:(p * grp + g + 1) * LANES] for g in range(grp)], axis=0)
        sink_vec = jnp.concatenate(
            [jnp.broadcast_to(jnp.where(lo, sinks_ref[2 * p * grp + g], sinks_ref[(2 * p + 1) * grp + g]),
                              (rows, LANES)) for g in range(grp)], axis=0)
        o = _attend_pair(qs, kslabs[p], vslabs[p], sink_vec, key_ok)
        outs.extend(o[g * rows:(g + 1) * rows] for g in range(grp))
    return jnp.concatenate(outs, axis=1)


def _swa_banded_kernel(sinks_ref, q_ref, k_ref, v_ref, cos_ref, sin_ref, g_ref,
                       y_ref, kr_ref, kbuf, vbuf, *, grp, window):
    rb = q_ref.shape[0]
    head_dim = LANES // 2
    n_pairs = kbuf.shape[1] // LANES
    step = pl.program_id(1)

    @pl.when(step == 0)
    def _():
        kbuf[0:window] = jnp.zeros((window, kbuf.shape[1]), BF16)
        vbuf[0:window] = jnp.zeros((window, vbuf.shape[1]), BF16)

    @pl.when(step > 0)
    def _():
        kbuf[0:window] = kbuf[rb:rb + window]
        vbuf[0:window] = vbuf[rb:rb + window]

    cos, sin = cos_ref[...], sin_ref[...]
    k = _rope(k_ref[...], cos, sin, head_dim)
    kr_ref[...] = k
    kbuf[window:window + rb] = k.astype(BF16)
    vbuf[window:window + rb] = v_ref[...].astype(BF16)
    q = _rope(q_ref[...], cos, sin, head_dim)

    n_buf_chunks = (window + rb) // CHUNK
    kblocks = [[_pair_blocks(kbuf[c * CHUNK:(c + 1) * CHUNK, p * LANES:(p + 1) * LANES])
                for c in range(n_buf_chunks)] for p in range(n_pairs)]
    vblocks = [[_pair_blocks(vbuf[c * CHUNK:(c + 1) * CHUNK, p * LANES:(p + 1) * LANES])
                for c in range(n_buf_chunks)] for p in range(n_pairs)]
    band_chunks = window // CHUNK + 1
    lane_key = lax.broadcasted_iota(jnp.int32, (1, LANES), 1) % CHUNK
    for c in range(rb // CHUNK):
        lo_row = c * CHUNK
        key_ok = [step * rb + (c + b) * CHUNK + lane_key >= window for b in range(band_chunks)]
        ks = [jnp.concatenate(kblocks[p][c:c + band_chunks], axis=0) for p in range(n_pairs)]
        vs = [jnp.concatenate(vblocks[p][c:c + band_chunks], axis=0) for p in range(n_pairs)]
        y = _attend(q[lo_row:lo_row + CHUNK], ks, vs, sinks_ref, key_ok, grp)
        y_ref[lo_row:lo_row + CHUNK] = (_rms(y) * g_ref[...]).astype(BF16)


def _swa_banded(proj, sinks, g, n_kv, head_dim, window, col_q):
    b, t, _ = proj.shape
    assert 2 * head_dim == LANES and CHUNK == head_dim
    n_heads = sinks.shape[0]
    att_w = n_heads * head_dim
    kv_w = n_kv * head_dim
    rb = _tile(ATTN_CHUNKS_PER_STEP * CHUNK, t)
    assert window % CHUNK == 0 and t % CHUNK == 0 and rb >= window
    cos, sin = _rope_tables(jnp.arange(t, dtype=jnp.int32), head_dim)
    col_k = (col_q + att_w) // kv_w
    kern = functools.partial(_swa_banded_kernel, grp=n_heads // n_kv, window=window)
    return pl.pallas_call(
        kern,
        grid=(b, t // rb),
        in_specs=[pl.BlockSpec(memory_space=pltpu.SMEM),
                  pl.BlockSpec((None, rb, att_w), lambda i, s: (i, s, col_q // att_w)),
                  pl.BlockSpec((None, rb, kv_w), lambda i, s: (i, s, col_k)),
                  pl.BlockSpec((None, rb, kv_w), lambda i, s: (i, s, col_k + 1)),
                  pl.BlockSpec((rb, LANES), lambda i, s: (s, 0)),
                  pl.BlockSpec((rb, LANES), lambda i, s: (s, 0)),
                  pl.BlockSpec((1, att_w), lambda i, s: (0, 0))],
        out_specs=[pl.BlockSpec((None, rb, att_w), lambda i, s: (i, s, 0)),
                   pl.BlockSpec((None, rb, kv_w), lambda i, s: (i, s, 0))],
        out_shape=[jax.ShapeDtypeStruct((b, t, att_w), BF16),
                   jax.ShapeDtypeStruct((b, t, kv_w), F32)],
        scratch_shapes=[pltpu.VMEM((window + rb, kv_w), BF16),
                        pltpu.VMEM((window + rb, kv_w), BF16)],
        compiler_params=_params(("arbitrary", "arbitrary")),
        name="swa_banded",
    )(sinks, proj, proj, proj, cos, sin, g.reshape(1, att_w))


def _swa_cached_kernel(sinks_ref, q_ref, k_ref, v_ref, ck_ref, cv_ref, cos_ref, sin_ref, g_ref,
                       y_ref, kr_ref, *, grp):
    head_dim = LANES // 2
    t, kv_w = k_ref.shape
    window = ck_ref.shape[0]
    n_pairs = kv_w // LANES
    cos, sin = cos_ref[...], sin_ref[...]
    k = _rope(k_ref[...], cos, sin, head_dim)
    kr_ref[...] = k
    q = _rope(q_ref[...], cos, sin, head_dim)
    n_keys = window + t
    pad = -n_keys % CHUNK
    tail = [jnp.zeros((pad, kv_w), BF16)] if pad else []
    kb = jnp.concatenate([ck_ref[...].astype(BF16), k.astype(BF16)] + tail, axis=0)
    vb = jnp.concatenate([cv_ref[...].astype(BF16), v_ref[...].astype(BF16)] + tail, axis=0)
    ks = [_pair_blocks(kb[:, p * LANES:(p + 1) * LANES]) for p in range(n_pairs)]
    vs = [_pair_blocks(vb[:, p * LANES:(p + 1) * LANES]) for p in range(n_pairs)]
    lane_key = lax.broadcasted_iota(jnp.int32, (1, LANES), 1) % CHUNK
    key_ok = [c * CHUNK + lane_key < n_keys for c in range((n_keys + pad) // CHUNK)]
    y = _attend(q, ks, vs, sinks_ref, key_ok, grp)
    y_ref[...] = (_rms(y) * g_ref[...]).astype(BF16)


def _swa_cached(proj, cache_k, cache_v, sinks, g, n_kv, head_dim, col_q):
    b, t, _ = proj.shape
    assert 2 * head_dim == LANES and CHUNK == head_dim
    window = cache_k.shape[1]
    n_heads = sinks.shape[0]
    att_w = n_heads * head_dim
    kv_w = n_kv * head_dim
    cos, sin = _rope_tables(PAST_LEN + jnp.arange(t, dtype=jnp.int32), head_dim)
    col_k = (col_q + att_w) // kv_w
    kern = functools.partial(_swa_cached_kernel, grp=n_heads // n_kv)
    return pl.pallas_call(
        kern,
        grid=(b,),
        in_specs=[pl.BlockSpec(memory_space=pltpu.SMEM),
                  pl.BlockSpec((None, t, att_w), lambda i: (i, 0, col_q // att_w)),
                  pl.BlockSpec((None, t, kv_w), lambda i: (i, 0, col_k)),
                  pl.BlockSpec((None, t, kv_w), lambda i: (i, 0, col_k + 1)),
                  pl.BlockSpec((None, window, kv_w), lambda i: (i, 0, 0)),
                  pl.BlockSpec((None, window, kv_w), lambda i: (i, 0, 0)),
                  pl.BlockSpec((t, LANES), lambda i: (0, 0)),
                  pl.BlockSpec((t, LANES), lambda i: (0, 0)),
                  pl.BlockSpec((1, att_w), lambda i: (0, 0))],
        out_specs=[pl.BlockSpec((None, t, att_w), lambda i: (i, 0, 0)),
                   pl.BlockSpec((None, t, kv_w), lambda i: (i, 0, 0))],
        out_shape=[jax.ShapeDtypeStruct((b, t, att_w), BF16),
                   jax.ShapeDtypeStruct((b, t, kv_w), F32)],
        compiler_params=_params(("arbitrary",)),
        name="swa_cached",
    )(sinks, proj, proj, proj, cache_k, cache_v, cos, sin, g.reshape(1, att_w))


def _outproj_kernel(yc_ref, ya_ref, wt_ref, wb_ref, b_ref, x_ref, g1_ref, o_ref):
    bb, tt, c = yc_ref.shape
    acc = jnp.dot(yc_ref[...].reshape(bb * tt, c), wt_ref[...], preferred_element_type=F32)
    acc = acc + jnp.dot(ya_ref[...].reshape(bb * tt, c), wb_ref[...], preferred_element_type=F32)
    acc = acc + b_ref[...]
    o_ref[...] = x_ref[...] + g1_ref[...] * acc.reshape(o_ref.shape)


def _outproj(yc, ya, w_bf, b_out, x, mod):
    b, t, d = x.shape
    c = yc.shape[2]
    assert 2 * c == w_bf.shape[0]
    bb, tt = _token_blocks(b, t, ROW_TILE)
    tn = _tile(OUTPROJ_COL_TILE, d)
    return pl.pallas_call(
        _outproj_kernel,
        grid=(b // bb, t // tt, d // tn),
        in_specs=[pl.BlockSpec((bb, tt, c), lambda i, s, j: (i, s, 0)),
                  pl.BlockSpec((bb, tt, c), lambda i, s, j: (i, s, 0)),
                  pl.BlockSpec((c, tn), lambda i, s, j: (0, j)),
                  pl.BlockSpec((c, tn), lambda i, s, j: (1, j)),
                  pl.BlockSpec((1, tn), lambda i, s, j: (0, j)),
                  pl.BlockSpec((bb, tt, tn), lambda i, s, j: (i, s, j)),
                  pl.BlockSpec((bb, 1, tn), lambda i, s, j: (i, 0, 2 * (d // tn) + j))],
        out_specs=pl.BlockSpec((bb, tt, tn), lambda i, s, j: (i, s, j)),
        out_shape=jax.ShapeDtypeStruct((b, t, d), F32),
        compiler_params=_params(("arbitrary", "arbitrary", "arbitrary")),
        name="outproj",
    )(yc, ya, w_bf, w_bf, b_out.reshape(1, d), x, mod)


def _router_kernel(x_ref, sh_ref, sc_ref, g_ref, whi_ref, wlo_ref, br_ref, c0_ref, *rest, has_prev):
    hp_ref, idx_ref, gate_ref, rank_ref, cnt_ref, carry = rest[1:] if has_prev else rest
    bb, tt, d = x_ref.shape
    rows = bb * tt
    n_exp = whi_ref.shape[1]

    @pl.when((pl.program_id(0) == 0) & (pl.program_id(1) == 0))
    def _():
        carry[...] = c0_ref[...]

    y = _rms(x_ref[...]) * g_ref[...]
    h = (y * (1.0 + sc_ref[...]) + sh_ref[...]).reshape(rows, d)
    h_hi = h.astype(BF16)
    h_hi32 = h_hi.astype(F32)
    bits = lax.bitcast_convert_type(h_hi32, U32)
    hp_ref[...] = (bits[:, d // 2:] & jnp.uint32(0xFFFF0000)) | (bits[:, :d // 2] >> 16)

    h_lo = (h - h_hi32).astype(BF16)
    logits = jnp.dot(h_hi, wlo_ref[...], preferred_element_type=F32)
    logits = logits + jnp.dot(h_lo, whi_ref[...], preferred_element_type=F32)
    logits = logits + jnp.dot(h_hi, whi_ref[...], preferred_element_type=F32) + br_ref[...]
    col = lax.broadcasted_iota(jnp.int32, logits.shape, 1).astype(F32)
    work = logits
    vals, idxs = [], []
    for _ in range(TOP_K):
        m = jnp.max(work, axis=-1, keepdims=True)
        sel = jnp.min(jnp.where(work == m, col, float(n_exp)), axis=-1, keepdims=True)
        vals.append(m)
        idxs.append(sel)
        work = jnp.where(col == sel, -jnp.inf, work)
    exps = [jnp.exp(v - vals[0]) for v in vals]
    den = exps[0]
    for e in exps[1:]:
        den = den + e
    idx_ref[...] = jnp.concatenate(idxs, axis=1).astype(jnp.int32)
    gate_ref[...] = jnp.concatenate([e / den for e in exps], axis=1)

    onehots = [(col == sel).astype(F32) for sel in idxs]
    cnt = onehots[0]
    for oh in onehots[1:]:
        cnt = cnt + oh
    earlier = (lax.broadcasted_iota(jnp.int32, (rows, rows), 0)
               > lax.broadcasted_iota(jnp.int32, (rows, rows), 1)).astype(BF16)
    before = jnp.dot(earlier, cnt.astype(BF16), preferred_element_type=F32) + carry[...]
    ranks = [jnp.sum(oh * before, axis=-1, keepdims=True) for oh in onehots]
    rank_ref[...] = jnp.concatenate(ranks, axis=1).astype(jnp.int32)
    carry[...] = carry[...] + jnp.sum(cnt, axis=0, keepdims=True)
    cnt_ref[...] = carry[...]


def _router(x1, mod, g, w_hi, w_lo, b_router, counts_before, hp_prev, tok_offset, n_tok_all):
    b, t, d = x1.shape
    n_exp = w_hi.shape[1]
    bb, tt = _token_blocks(b, t, FFN_ROW_TILE)
    rows = bb * tt
    n_tok = b * t
    assert tok_offset % rows == 0
    blk0 = tok_offset // rows
    flat = lambda i, s: (i * (t // tt) + s, 0)
    has_prev = hp_prev is not None
    in_specs = [pl.BlockSpec((bb, tt, d), lambda i, s: (i, s, 0)),
                pl.BlockSpec((bb, 1, d), lambda i, s: (i, 0, 3)),
                pl.BlockSpec((bb, 1, d), lambda i, s: (i, 0, 4)),
                pl.BlockSpec((1, d), lambda i, s: (0, 0)),
                pl.BlockSpec((d, n_exp), lambda i, s: (0, 0)),
                pl.BlockSpec((d, n_exp), lambda i, s: (0, 0)),
                pl.BlockSpec((1, n_exp), lambda i, s: (0, 0)),
                pl.BlockSpec((1, n_exp), lambda i, s: (0, 0))]
    args = [x1, mod, mod, g.reshape(1, d), w_hi, w_lo, b_router.reshape(1, n_exp), counts_before]
    if has_prev:
        in_specs.append(pl.BlockSpec(memory_space=pl.ANY))
        args.append(hp_prev)
    return pl.pallas_call(
        functools.partial(_router_kernel, has_prev=has_prev),
        grid=(b // bb, t // tt),
        in_specs=in_specs,
        out_specs=[pl.BlockSpec((rows, d // 2), lambda i, s: (blk0 + i * (t // tt) + s, 0)),
                   pl.BlockSpec((rows, TOP_K), flat),
                   pl.BlockSpec((rows, TOP_K), flat),
                   pl.BlockSpec((rows, TOP_K), flat),
                   pl.BlockSpec((1, n_exp), lambda i, s: (0, 0))],
        out_shape=[jax.ShapeDtypeStruct((n_tok_all, d // 2), U32),
                   jax.ShapeDtypeStruct((n_tok, TOP_K), jnp.int32),
                   jax.ShapeDtypeStruct((n_tok, TOP_K), F32),
                   jax.ShapeDtypeStruct((n_tok, TOP_K), jnp.int32),
                   jax.ShapeDtypeStruct((1, n_exp), F32)],
        scratch_shapes=[pltpu.VMEM((1, n_exp), F32)],
        input_output_aliases={len(args) - 1: 0} if has_prev else {},
        compiler_params=_params(("arbitrary", "arbitrary")),
        name="router",
    )(*args)


def _issue_row_copies(idx_tile_ref, idx_smem, sem_idx, src_ref, dst_ref, sem):
    cp = pltpu.make_async_copy(idx_tile_ref.at[0, 0], idx_smem, sem_idx)
    cp.start()
    cp.wait()

    def body(g, carry):
        for u in range(ROW_COPY_UNROLL):
            r = g * ROW_COPY_UNROLL + u
            pltpu.make_async_copy(src_ref.at[pl.ds(idx_smem[r], 1)], dst_ref.at[pl.ds(r, 1)], sem).start(
                priority=u % 2)
        return carry

    lax.fori_loop(0, idx_smem.shape[0] // ROW_COPY_UNROLL, body, 0)


def _wait_row_copies(src_ref, dst_ref, sem):
    pltpu.make_async_copy(src_ref.at[pl.ds(0, dst_ref.shape[0])], dst_ref, sem).wait()


def _gather_kernel(nused_ref, idx0_ref, idxn_ref, src_ref, o_ref, idx_smem, buf, sem_idx, sem_rows):
    i = pl.program_id(0)
    n = nused_ref[0]
    half = src_ref.shape[1]

    @pl.when(i == 0)
    def _():
        _issue_row_copies(idx0_ref, idx_smem, sem_idx, src_ref, buf.at[0], sem_rows.at[0])

    @pl.when(i + 1 < n)
    def _():
        nxt = (i + 1) % 2
        _issue_row_copies(idxn_ref, idx_smem, sem_idx, src_ref, buf.at[nxt], sem_rows.at[nxt])

    @pl.when(i < n)
    def _():
        slot = i % 2
        _wait_row_copies(src_ref, buf.at[slot], sem_rows.at[slot])
        w = buf[slot]
        o_ref[:, :half] = _pair_rows(lax.bitcast_convert_type(w << 16, F32).astype(BF16))
        o_ref[:, half:] = _pair_rows(lax.bitcast_convert_type(w & jnp.uint32(0xFFFF0000), F32).astype(BF16))


def _pair_rows(x):
    return pltpu.bitcast(x, U32)


def _gather_rows(src_packed, row_src, n_used, rows):
    p = row_src.shape[0]
    half = src_packed.shape[1]
    nblk = p // rows
    clamp = lambda i, nu: jnp.minimum(i, nu[0] - 1)
    return pl.pallas_call(
        _gather_kernel,
        grid_spec=pltpu.PrefetchScalarGridSpec(
            num_scalar_prefetch=1,
            grid=(nblk,),
            in_specs=[pl.BlockSpec((1, 1, rows), lambda i, nu: (0, 0, 0)),
                      pl.BlockSpec((1, 1, rows), lambda i, nu: (clamp(i + 1, nu), 0, 0)),
                      pl.BlockSpec(memory_space=pl.ANY)],
            out_specs=pl.BlockSpec((rows // 2, 2 * half), lambda i, nu: (clamp(i, nu), 0)),
            scratch_shapes=[pltpu.SMEM((rows,), jnp.int32),
                            pltpu.VMEM((2, rows, half), U32),
                            pltpu.SemaphoreType.DMA(()),
                            pltpu.SemaphoreType.DMA((2,))]),
        out_shape=jax.ShapeDtypeStruct((p // 2, 2 * half), U32),
        compiler_params=_params(("arbitrary",)),
        name="moe_gather",
    )(n_used, row_src.reshape(nblk, 1, rows), row_src.reshape(nblk, 1, rows), src_packed)


def _stream_expert_tiles(first_ref, ntile_ref, ntot_ref, src_hbm, dst_hbm, xbuf, obuf, sem_in, sem_out, compute):
    j, e = pl.program_id(0), pl.program_id(1)
    nj, ne = pl.num_programs(0), pl.num_programs(1)
    in_rows = xbuf.shape[1]
    out_rows = obuf.shape[1]
    tn = obuf.shape[2]
    n_total = ntot_ref[0]

    def x_copy(t, slot):
        r0 = pl.multiple_of(t * in_rows, in_rows)
        return pltpu.make_async_copy(src_hbm.at[pl.ds(r0, in_rows)], xbuf.at[slot], sem_in.at[slot])

    def x_start(t, slot):
        x_copy(t, slot).start(priority=TILE_COPY_PRIORITY)

    def x_wait(slot):
        x_copy(0, slot).wait()

    def o_copy(t, slot):
        r0 = pl.multiple_of(t * out_rows, out_rows)
        c0 = pl.multiple_of(j * tn, tn)
        return pltpu.make_async_copy(obuf.at[slot], dst_hbm.at[pl.ds(r0, out_rows), pl.ds(c0, tn)],
                                     sem_out.at[slot])

    @pl.when((j == 0) & (e == 0))
    def _():
        x_start(0, 0)

    def body(t, carry):
        seq = j * n_total + t
        slot = seq % 2
        x_wait(slot)
        wraps = t + 1 == n_total

        @pl.when(jnp.logical_not(wraps & (j == nj - 1)))
        def _():
            x_start(jnp.where(wraps, 0, t + 1), 1 - slot)

        @pl.when(seq >= 2)
        def _():
            o_copy(t, slot).wait()

        obuf[slot] = compute(xbuf[slot])
        o_copy(t, slot).start()
        return carry

    t0 = first_ref[e]
    lax.fori_loop(t0, t0 + ntile_ref[e], body, 0)

    @pl.when((j == nj - 1) & (e == ne - 1))
    def _():
        n_seq = nj * n_total

        @pl.when(n_seq >= 2)
        def _():
            o_copy(0, n_seq % 2).wait()

        o_copy(0, (n_seq - 1) % 2).wait()


def _moe_up_kernel(first_ref, ntile_ref, eff_ref, ntot_ref, x_hbm, wg_ref, wu_ref, bg_ref, bu_ref, act_hbm,
                   xbuf, obuf, wg_bf, wu_bf, sem_in, sem_out):
    @pl.when(ntile_ref[pl.program_id(1)] > 0)
    def _():
        wg_bf[...] = wg_ref[...].astype(BF16)
        wu_bf[...] = wu_ref[...].astype(BF16)

    def compute(x_words):
        x = pltpu.bitcast(x_words, BF16)
        g = jnp.dot(x, wg_bf[...], preferred_element_type=F32) + bg_ref[...]
        u = jnp.dot(x, wu_bf[...], preferred_element_type=F32) + bu_ref[...]
        g = jnp.minimum(g, SWIGLU_LIMIT)
        u = jnp.clip(u, -SWIGLU_LIMIT, SWIGLU_LIMIT)
        return _pair_rows(((u + 1.0) * g * jax.nn.sigmoid(SWIGLU_ALPHA * g)).astype(BF16))

    _stream_expert_tiles(first_ref, ntile_ref, ntot_ref, x_hbm, act_hbm, xbuf, obuf, sem_in, sem_out, compute)


def _moe_up(xs, w_gate_up, b_gate_up, tile_first, tile_count, eff_expert, n_tiles):
    p2, d = xs.shape
    n_exp, _, f2 = w_gate_up.shape
    f = f2 // 2
    tf = _tile(MOE_FF_TILE, f)
    nj = f // tf
    return pl.pallas_call(
        _moe_up_kernel,
        grid_spec=pltpu.PrefetchScalarGridSpec(
            num_scalar_prefetch=4,
            grid=(nj, n_exp),
            in_specs=[pl.BlockSpec(memory_space=pl.ANY),
                      pl.BlockSpec((None, d, tf), lambda j, e, fi, nt, ef, ntot: (ef[e], 0, j)),
                      pl.BlockSpec((None, d, tf), lambda j, e, fi, nt, ef, ntot: (ef[e], 0, nj + j)),
                      pl.BlockSpec((None, 1, tf), lambda j, e, fi, nt, ef, ntot: (ef[e], 0, j)),
                      pl.BlockSpec((None, 1, tf), lambda j, e, fi, nt, ef, ntot: (ef[e], 0, nj + j))],
            out_specs=pl.BlockSpec(memory_space=pl.ANY),
            scratch_shapes=[pltpu.VMEM((2, MOE_ROWS // 2, d), U32),
                            pltpu.VMEM((2, MOE_ROWS // 2, tf), U32),
                            pltpu.VMEM((d, tf), BF16),
                            pltpu.VMEM((d, tf), BF16),
                            pltpu.SemaphoreType.DMA((2,)),
                            pltpu.SemaphoreType.DMA((2,))]),
        out_shape=jax.ShapeDtypeStruct((p2, f), U32),
        compiler_params=_params(("arbitrary", "arbitrary")),
        name="moe_up",
    )(tile_first, tile_count, eff_expert, n_tiles, xs, w_gate_up, w_gate_up,
      b_gate_up.reshape(n_exp, 1, f2), b_gate_up.reshape(n_exp, 1, f2))


def _moe_down_kernel(first_ref, ntile_ref, eff_ref, ntot_ref, a_hbm, w_ref, b_ref, y_hbm,
                     xbuf, obuf, w_bf, sem_in, sem_out):
    @pl.when(ntile_ref[pl.program_id(1)] > 0)
    def _():
        w_bf[...] = w_ref[...].astype(BF16)

    def compute(a_words):
        return jnp.dot(pltpu.bitcast(a_words, BF16), w_bf[...], preferred_element_type=F32) + b_ref[...]

    _stream_expert_tiles(first_ref, ntile_ref, ntot_ref, a_hbm, y_hbm, xbuf, obuf, sem_in, sem_out, compute)


def _moe_down(act, w_down, b_down, tile_first, tile_count, eff_expert, n_tiles):
    p2, f = act.shape
    n_exp, _, d = w_down.shape
    tn = _tile(MOE_OUT_TILE, d)
    return pl.pallas_call(
        _moe_down_kernel,
        grid_spec=pltpu.PrefetchScalarGridSpec(
            num_scalar_prefetch=4,
            grid=(d // tn, n_exp),
            in_specs=[pl.BlockSpec(memory_space=pl.ANY),
                      pl.BlockSpec((None, f, tn), lambda j, e, fi, nt, ef, ntot: (ef[e], 0, j)),
                      pl.BlockSpec((None, 1, tn), lambda j, e, fi, nt, ef, ntot: (ef[e], 0, j))],
            out_specs=pl.BlockSpec(memory_space=pl.ANY),
            scratch_shapes=[pltpu.VMEM((2, MOE_ROWS // 2, f), U32),
                            pltpu.VMEM((2, MOE_ROWS, tn), F32),
                            pltpu.VMEM((f, tn), BF16),
                            pltpu.SemaphoreType.DMA((2,)),
                            pltpu.SemaphoreType.DMA((2,))]),
        out_shape=jax.ShapeDtypeStruct((2 * p2, d), F32),
        compiler_params=_params(("arbitrary", "arbitrary")),
        name="moe_down",
    )(tile_first, tile_count, eff_expert, n_tiles, act, w_down, b_down.reshape(n_exp, 1, d))


def _combine_kernel(pos0_ref, posn_ref, y_ref, gate_ref, x_ref, g2_ref, gf_ref, o_ref,
                    pos_smem, buf, sem_idx, sem_rows):
    bb, tt, d = x_ref.shape
    rows = bb * tt
    i = pl.program_id(0)

    @pl.when(i == 0)
    def _():
        _issue_row_copies(pos0_ref, pos_smem, sem_idx, y_ref, buf.at[0], sem_rows.at[0])

    @pl.when(i + 1 < pl.num_programs(0))
    def _():
        nxt = (i + 1) % 2
        _issue_row_copies(posn_ref, pos_smem, sem_idx, y_ref, buf.at[nxt], sem_rows.at[nxt])

    slot = i % 2
    _wait_row_copies(y_ref, buf.at[slot], sem_rows.at[slot])
    gates = gate_ref[...]
    ffn = gates[:, 0:1] * buf[slot, 0:rows]
    for k in range(1, TOP_K):
        ffn = ffn + gates[:, k:k + 1] * buf[slot, k * rows:(k + 1) * rows]
    x2 = x_ref[...] + g2_ref[...] * ffn.reshape(bb, tt, d)
    o_ref[...] = _rms(x2) * gf_ref[...]


def _combine(y_sorted, pos, gates, x1, mod, gf, tok_offset):
    b, t, d = x1.shape
    bb, tt = _token_blocks(b, t, COMBINE_ROWS)
    rows = bb * tt
    assert tok_offset % rows == 0
    blk0 = tok_offset // rows
    n_tok = pos.shape[0]
    nblk = (b // bb) * (t // tt)
    ts = t // tt
    pos_tiles = pos.reshape(n_tok // rows, rows, TOP_K).transpose(0, 2, 1).reshape(n_tok // rows, 1, TOP_K * rows)
    return pl.pallas_call(
        _combine_kernel,
        grid=(nblk,),
        in_specs=[pl.BlockSpec((1, 1, TOP_K * rows), lambda i: (blk0, 0, 0)),
                  pl.BlockSpec((1, 1, TOP_K * rows), lambda i: (blk0 + jnp.minimum(i + 1, nblk - 1), 0, 0)),
                  pl.BlockSpec(memory_space=pl.ANY),
                  pl.BlockSpec((rows, TOP_K), lambda i: (blk0 + i, 0)),
                  pl.BlockSpec((bb, tt, d), lambda i: (i // ts, i % ts, 0)),
                  pl.BlockSpec((bb, 1, d), lambda i: (i // ts, 0, 5)),
                  pl.BlockSpec((1, d), lambda i: (0, 0))],
        out_specs=pl.BlockSpec((bb, tt, d), lambda i: (i // ts, i % ts, 0)),
        out_shape=jax.ShapeDtypeStruct((b, t, d), F32),
        scratch_shapes=[pltpu.SMEM((TOP_K * rows,), jnp.int32),
                        pltpu.VMEM((2, TOP_K * rows, d), F32),
                        pltpu.SemaphoreType.DMA(()),
                        pltpu.SemaphoreType.DMA((2,))],
        compiler_params=_params(("arbitrary",)),
        name="moe_combine",
    )(pos_tiles, pos_tiles, y_sorted, gates, x1, mod, gf.reshape(1, d))


def _routing(top_idx, rank, counts, rows):
    n_tok = top_idx.shape[0]
    n_exp = counts.shape[0]
    n_assign = n_tok * TOP_K
    tile_count = (counts + rows - 1) // rows
    tile_end = jnp.cumsum(tile_count)
    tile_first = tile_end - tile_count
    pos = (tile_first * rows)[top_idx] + rank
    p = n_assign + n_exp * rows
    token_of = jnp.arange(n_assign, dtype=jnp.int32) // TOP_K
    row_token = jnp.zeros((p,), jnp.int32).at[pos.reshape(-1)].set(
        token_of, unique_indices=True, mode="promise_in_bounds")
    ids = jnp.where(tile_count > 0, jnp.arange(n_exp, dtype=jnp.int32), -1)
    eff = lax.cummax(ids)
    eff = jnp.where(eff < 0, jnp.argmax(tile_count > 0).astype(jnp.int32), eff)
    i32 = lambda a: a.astype(jnp.int32)
    return i32(pos), row_token, i32(tile_first), i32(tile_count), i32(eff), i32(tile_end[-1:])


def kernel(x_prompt, x_sample, cache_k, cache_v, state_conv, c_prompt, c_sample, w_ada, b_ada, norm_mix_g, norm_ffn_g, norm_final_g, w_in, b_in, conv_w, conv_b, sinks, out_norm_conv_g, out_norm_attn_g, w_out, b_out, w_router, b_router, w_gate_up, b_gate_up, w_down, b_down):
    d = x_prompt.shape[2]
    depth, dec_b, window, n_kv, head_dim = cache_k.shape
    assert depth == 1
    conv_ch = conv_w.shape[1]
    n_exp = w_router.shape[1]
    n_heads = sinks.shape[0]
    bp, tp, _ = x_prompt.shape
    bs, ts, _ = x_sample.shape
    col_q = 3 * conv_ch
    kv_w = n_kv * head_dim
    att_w = n_heads * head_dim

    mod = _adaln(jnp.concatenate([c_prompt, c_sample], axis=0), w_ada, b_ada)
    mod_p = mod[:bp].reshape(bp, 1, 6 * d)
    mod_s = mod[bp:].reshape(bs, 1, 6 * d)
    order = _head_order(n_heads, n_kv)
    w_in_bf = w_in.astype(BF16)
    w_in_bf = lax.dynamic_update_slice(
        w_in_bf, _permute_blocks(w_in_bf[:, col_q:col_q + att_w], order, head_dim, 1, 0), (0, col_q))
    b_in_p = _permute_blocks(b_in, order, head_dim, 0, col_q)
    w_out_bf = w_out.astype(BF16)
    w_out_bf = lax.dynamic_update_slice(
        w_out_bf, _permute_blocks(w_out_bf[conv_ch:], order, head_dim, 0, 0), (conv_ch, 0))
    attn_g = _permute_blocks(out_norm_attn_g, order, head_dim, 0, 0)
    wr_hi = w_router.astype(BF16)
    wr_lo = (w_router - wr_hi.astype(F32)).astype(BF16)
    n_p, n_s = bp * tp, bs * ts

    def mixer(x, modx, conv_prev, ck, cv, counts_before, hp_prev, tok_offset):
        b, t, _ = x.shape
        proj = _inproj(x, modx, norm_mix_g, w_in_bf, b_in_p)
        yc, conv_state = _conv_mix(proj, conv_prev, conv_w, conv_b, out_norm_conv_g)
        if ck is None:
            ya, k_rot = _swa_banded(proj, sinks, attn_g, n_kv, head_dim, window, col_q)
            keep = window
        else:
            ya, k_rot = _swa_cached(proj, ck, cv, sinks, attn_g, n_kv, head_dim, col_q)
            keep = t
        k_rows = k_rot[:, t - keep:].reshape(1, b, keep, n_kv, head_dim)
        v_rows = proj[:, t - keep:, col_q + att_w + kv_w:].reshape(1, b, keep, n_kv, head_dim)
        x1 = _outproj(yc, ya, w_out_bf, b_out, x, modx)
        routed = _router(x1, modx, norm_ffn_g, wr_hi, wr_lo, b_router, counts_before, hp_prev,
                         tok_offset, n_p + n_s)
        return x1, routed, conv_state, k_rows, v_rows

    zero_state = jnp.zeros((bp, 2, conv_ch), F32)
    x1p, (hp_buf, idxp, gatep, rankp, cntp), conv_p, k_prompt, v_prompt = mixer(
        x_prompt, mod_p, zero_state, None, None, jnp.zeros((1, n_exp), F32), None, 0)
    x1s, (h2_packed, idxs, gates_s, ranks, cnt), conv_s, k_sample, v_sample = mixer(
        x_sample, mod_s, state_conv[0],
        cache_k[0].reshape(dec_b, window, kv_w), cache_v[0].reshape(dec_b, window, kv_w), cntp, hp_buf, n_p)

    top_idx = jnp.concatenate([idxp, idxs], axis=0)
    gates = jnp.concatenate([gatep, gates_s], axis=0)
    rank = jnp.concatenate([rankp, ranks], axis=0)
    pos, row_token, tile_first, tile_count, eff, n_tiles = _routing(
        top_idx, rank, cnt[0].astype(jnp.int32), MOE_ROWS)
    xs = _gather_rows(h2_packed, row_token, n_tiles, MOE_ROWS)
    act = _moe_up(xs, w_gate_up, b_gate_up, tile_first, tile_count, eff, n_tiles)
    y_sorted = _moe_down(act, w_down, b_down, tile_first, tile_count, eff, n_tiles)
    y_prompt = _combine(y_sorted, pos, gates, x1p, mod_p, norm_final_g, 0)
    y_sample = _combine(y_sorted, pos, gates, x1s, mod_s, norm_final_g, n_p)
    return (y_prompt, y_sample, k_prompt, v_prompt, conv_p[None], k_sample, v_sample, conv_s[None])
```

```python
import functools

import jax
import jax.numpy as jnp
from jax import lax
from jax.experimental import pallas as pl
from jax.experimental.pallas import tpu as pltpu

F32 = jnp.float32
BF16 = jnp.bfloat16
U32 = jnp.uint32

CHUNK = 64
TOP_K = 4
PAST_LEN = 2048
ROPE_THETA = 10000.0
NORM_EPS = 1e-5
SWIGLU_LIMIT = 7.0
SWIGLU_ALPHA = 1.702

LANES = 128
SUBLANES = 8
VMEM_LIMIT_BYTES = 58 * 1024 * 1024

ROW_TILE = 512
ADA_COL_TILE = 512
INPROJ_COL_TILE = 1024
OUTPROJ_COL_TILE = 1024
CONV_ROW_TILE = 256
ATTN_CHUNKS_PER_STEP = 4
FFN_ROW_TILE = 256
MOE_ROWS = 256
MOE_FF_TILE = 512
MOE_OUT_TILE = 1024
MOE_IN_SLOTS = 4
MOE_CAST_CHUNKS = 4
COMBINE_ROWS = 128
ROW_COPY_UNROLL = 8
TILE_COPY_PRIORITY = 1


def _tile(pref, dim):
    t = min(pref, dim)
    while dim % t:
        t -= 1
    return t


def _params(sem):
    return pltpu.CompilerParams(dimension_semantics=sem, vmem_limit_bytes=VMEM_LIMIT_BYTES)


def _rms(x):
    return x * lax.rsqrt(jnp.mean(x * x, axis=-1, keepdims=True) + NORM_EPS)


def _token_blocks(b, t, rows_pref):
    if t >= rows_pref:
        return 1, _tile(rows_pref, t)
    return _tile(max(rows_pref // t, 1), b), t


def _adaln_kernel(c_ref, w_ref, b_ref, o_ref):
    c = c_ref[...]
    a = (c * jax.nn.sigmoid(c)).astype(BF16)
    o_ref[...] = jnp.dot(a, w_ref[...].astype(BF16), preferred_element_type=F32) + b_ref[...]


def _adaln(c, w_ada, b_ada):
    nb, d = c.shape
    n = w_ada.shape[1]
    tn = _tile(ADA_COL_TILE, n)
    return pl.pallas_call(
        _adaln_kernel,
        grid=(n // tn,),
        in_specs=[pl.BlockSpec((nb, d), lambda j: (0, 0)),
                  pl.BlockSpec((d, tn), lambda j: (0, j)),
                  pl.BlockSpec((1, tn), lambda j: (0, j))],
        out_specs=pl.BlockSpec((nb, tn), lambda j: (0, j)),
        out_shape=jax.ShapeDtypeStruct((nb, n), F32),
        compiler_params=_params(("arbitrary",)),
        name="adaln",
    )(c, w_ada, b_ada.reshape(1, n))


def _inproj_kernel(x_ref, sh_ref, sc_ref, g_ref, w_ref, b_ref, o_ref, h_ref):
    bb, tt, d = x_ref.shape

    @pl.when(pl.program_id(2) == 0)
    def _():
        y = _rms(x_ref[...]) * g_ref[...]
        h = y * (1.0 + sc_ref[...]) + sh_ref[...]
        h_ref[...] = h.reshape(bb * tt, d).astype(BF16)

    acc = jnp.dot(h_ref[...], w_ref[...], preferred_element_type=F32) + b_ref[...]
    o_ref[...] = acc.reshape(o_ref.shape)


def _inproj(x, mod, g, w_bf, b_in):
    b, t, d = x.shape
    n = w_bf.shape[1]
    bb, tt = _token_blocks(b, t, ROW_TILE)
    tn = _tile(INPROJ_COL_TILE, n)
    return pl.pallas_call(
        _inproj_kernel,
        grid=(b // bb, t // tt, n // tn),
        in_specs=[pl.BlockSpec((bb, tt, d), lambda i, s, j: (i, s, 0)),
                  pl.BlockSpec((bb, 1, d), lambda i, s, j: (i, 0, 0)),
                  pl.BlockSpec((bb, 1, d), lambda i, s, j: (i, 0, 1)),
                  pl.BlockSpec((1, d), lambda i, s, j: (0, 0)),
                  pl.BlockSpec((d, tn), lambda i, s, j: (0, j)),
                  pl.BlockSpec((1, tn), lambda i, s, j: (0, j))],
        out_specs=pl.BlockSpec((bb, tt, tn), lambda i, s, j: (i, s, j)),
        out_shape=jax.ShapeDtypeStruct((b, t, n), F32),
        scratch_shapes=[pltpu.VMEM((bb * tt, d), BF16)],
        compiler_params=_params(("arbitrary", "arbitrary", "arbitrary")),
        name="inproj",
    )(x, mod, mod, g.reshape(1, d), w_bf, b_in.reshape(1, n))


def _conv_kernel(bg_ref, cg_ref, xc_ref, cgp_ref, xcp_ref, st_ref, w_ref, b_ref, g_ref, y_ref, so_ref):
    tt = cg_ref.shape[0]
    u = cg_ref[...] * xc_ref[...]
    halo = cgp_ref[...] * xcp_ref[...]
    st = st_ref[...]
    first = pl.program_id(1) == 0
    um2 = jnp.where(first, st[0:1], halo[SUBLANES - 2:SUBLANES - 1])
    um1 = jnp.where(first, st[1:2], halo[SUBLANES - 1:SUBLANES])
    row = lax.broadcasted_iota(jnp.int32, u.shape, 0)
    u1 = jnp.where(row == 0, um1, pltpu.roll(u, 1, 0))
    u2 = jnp.where(row == 0, um2, jnp.where(row == 1, um1, pltpu.roll(u, 2, 0)))
    w = w_ref[...]
    conv = b_ref[...] + u2 * w[0:1]
    conv = conv + u1 * w[1:2]
    conv = conv + u * w[2:3]
    y = _rms(bg_ref[...] * conv) * g_ref[...]
    y_ref[...] = y.astype(BF16)
    so_ref[...] = u[tt - 2:tt]


def _conv_mix(proj, state, conv_w, conv_b, g):
    b, t, _ = proj.shape
    c = conv_w.shape[1]
    assert conv_w.shape[0] == 3
    tt = _tile(CONV_ROW_TILE, t)
    hs = tt // SUBLANES
    halo_map = lambda col: (lambda i, s: (i, jnp.maximum(s * hs - 1, 0), col))
    return pl.pallas_call(
        _conv_kernel,
        grid=(b, t // tt),
        in_specs=[pl.BlockSpec((None, tt, c), lambda i, s: (i, s, 0)),
                  pl.BlockSpec((None, tt, c), lambda i, s: (i, s, 1)),
                  pl.BlockSpec((None, tt, c), lambda i, s: (i, s, 2)),
                  pl.BlockSpec((None, SUBLANES, c), halo_map(1)),
                  pl.BlockSpec((None, SUBLANES, c), halo_map(2)),
                  pl.BlockSpec((None, 2, c), lambda i, s: (i, 0, 0)),
                  pl.BlockSpec((3, c), lambda i, s: (0, 0)),
                  pl.BlockSpec((1, c), lambda i, s: (0, 0)),
                  pl.BlockSpec((1, c), lambda i, s: (0, 0))],
        out_specs=[pl.BlockSpec((None, tt, c), lambda i, s: (i, s, 0)),
                   pl.BlockSpec((None, 2, c), lambda i, s: (i, 0, 0))],
        out_shape=[jax.ShapeDtypeStruct((b, t, c), BF16),
                   jax.ShapeDtypeStruct((b, 2, c), F32)],
        compiler_params=_params(("arbitrary", "arbitrary")),
        name="conv_mix",
    )(proj, proj, proj, proj, proj, state, conv_w, conv_b.reshape(1, c), g.reshape(1, c))


def _head_order(n_heads, n_kv):
    grp = n_heads // n_kv
    assert n_kv % 2 == 0
    return [(2 * p + half) * grp + g for p in range(n_kv // 2) for g in range(grp) for half in (0, 1)]


def _permute_blocks(x, order, width, axis, start):
    n = len(order) * width
    take = lambda lo, hi: lax.slice_in_dim(x, lo, hi, axis=axis)
    parts = [take(0, start)] + [take(start + h * width, start + (h + 1) * width) for h in order]
    parts.append(take(start + n, x.shape[axis]))
    return jnp.concatenate(parts, axis=axis)


def _rope_tables(pos, head_dim):
    half = head_dim // 2
    inv_freq = ROPE_THETA ** (-jnp.arange(half, dtype=F32) / half)
    ang = pos.astype(F32)[:, None] * inv_freq[None, :]
    cos, sin = jnp.cos(ang), jnp.sin(ang)
    reps = LANES // head_dim
    return (jnp.tile(jnp.concatenate([cos, cos], axis=-1), (1, reps)),
            jnp.tile(jnp.concatenate([-sin, sin], axis=-1), (1, reps)))


def _rope(x, cos, sin, head_dim):
    width = x.shape[1]
    half = head_dim // 2
    reps = width // LANES
    cos = jnp.concatenate([cos] * reps, axis=1)
    sin = jnp.concatenate([sin] * reps, axis=1)
    lane = lax.broadcasted_iota(jnp.int32, x.shape, 1)
    swapped = jnp.where(lane % head_dim < half,
                        pltpu.roll(x, width - half, 1), pltpu.roll(x, half, 1))
    return x * cos + swapped * sin


def _pair_blocks(slab):
    keys = slab.shape[0]
    lo = lax.broadcasted_iota(jnp.int32, (CHUNK, LANES), 1) < LANES // 2
    zero = jnp.zeros((CHUNK, LANES), slab.dtype)
    parts = []
    for c in range(keys // CHUNK):
        blk = slab[c * CHUNK:(c + 1) * CHUNK]
        parts += [jnp.where(lo, blk, zero), jnp.where(lo, zero, blk)]
    return jnp.concatenate(parts, axis=0)


def _attend_pair(qs, kblk, vblk, sink_vec, key_ok):
    head_dim = LANES // 2
    n_chunks = kblk.shape[0] // LANES
    lo = lax.broadcasted_iota(jnp.int32, (1, LANES), 1) < head_dim
    s = lax.dot_general(qs.astype(BF16), kblk, (((1,), (1,)), ((), ())),
                        preferred_element_type=F32) * (head_dim ** -0.5)
    tiles = [jnp.where(key_ok[c], s[:, c * LANES:(c + 1) * LANES], -jnp.inf) for c in range(n_chunks)]

    def per_half(t, reduce_fn, fill):
        a = reduce_fn(jnp.where(lo, t, fill), axis=-1, keepdims=True)
        b = reduce_fn(jnp.where(lo, fill, t), axis=-1, keepdims=True)
        return jnp.where(lo, a, b)

    tmax = tiles[0]
    for t in tiles[1:]:
        tmax = jnp.maximum(tmax, t)
    m = jnp.maximum(per_half(tmax, jnp.max, -jnp.inf), sink_vec)
    es = [jnp.exp(t - m) for t in tiles]
    esum = es[0]
    for e in es[1:]:
        esum = esum + e
    den = per_half(esum, jnp.sum, 0.0) + jnp.exp(sink_vec - m)
    e_all = jnp.concatenate([e.astype(BF16) for e in es], axis=1)
    return jnp.dot(e_all, vblk, preferred_element_type=F32) / den


def _attend(q, kslabs, vslabs, sinks_ref, key_ok, grp):
    rows = q.shape[0]
    head_dim = LANES // 2
    lo = lax.broadcasted_iota(jnp.int32, (1, LANES), 1) < head_dim
    outs = []
    for p in range(len(kslabs)):
        qs = jnp.concatenate([q[:, (p * grp + g) * LANES:(p * grp + g + 1) * LANES] for g in range(grp)], axis=0)
        sink_vec = jnp.concatenate(
            [jnp.broadcast_to(jnp.where(lo, sinks_ref[2 * p * grp + g], sinks_ref[(2 * p + 1) * grp + g]),
                              (rows, LANES)) for g in range(grp)], axis=0)
        o = _attend_pair(qs, kslabs[p], vslabs[p], sink_vec, key_ok)
        outs.extend(o[g * rows:(g + 1) * rows] for g in range(grp))
    return jnp.concatenate(outs, axis=1)


def _swa_banded_kernel(sinks_ref, q_ref, k_ref, v_ref, cos_ref, sin_ref, g_ref,
                       y_ref, kr_ref, kbuf, vbuf, *, grp, window):
    rb = q_ref.shape[0]
    head_dim = LANES // 2
    n_pairs = kbuf.shape[1] // LANES
    step = pl.program_id(1)

    @pl.when(step == 0)
    def _():
        kbuf[0:window] = jnp.zeros((window, kbuf.shape[1]), BF16)
        vbuf[0:window] = jnp.zeros((window, vbuf.shape[1]), BF16)

    @pl.when(step > 0)
    def _():
        kbuf[0:window] = kbuf[rb:rb + window]
        vbuf[0:window] = vbuf[rb:rb + window]

    cos, sin = cos_ref[...], sin_ref[...]
    k = _rope(k_ref[...], cos, sin, head_dim)
    kr_ref[...] = k
    kbuf[window:window + rb] = k.astype(BF16)
    vbuf[window:window + rb] = v_ref[...].astype(BF16)
    q = _rope(q_ref[...], cos, sin, head_dim)

    n_buf_chunks = (window + rb) // CHUNK
    kblocks = [[_pair_blocks(kbuf[c * CHUNK:(c + 1) * CHUNK, p * LANES:(p + 1) * LANES])
                for c in range(n_buf_chunks)] for p in range(n_pairs)]
    vblocks = [[_pair_blocks(vbuf[c * CHUNK:(c + 1) * CHUNK, p * LANES:(p + 1) * LANES])
                for c in range(n_buf_chunks)] for p in range(n_pairs)]
    band_chunks = window // CHUNK + 1
    lane_key = lax.broadcasted_iota(jnp.int32, (1, LANES), 1) % CHUNK
    for c in range(rb // CHUNK):
        lo_row = c * CHUNK
        key_ok = [step * rb + (c + b) * CHUNK + lane_key >= window for b in range(band_chunks)]
        ks = [jnp.concatenate(kblocks[p][c:c + band_chunks], axis=0) for p in range(n_pairs)]
        vs = [jnp.concatenate(vblocks[p][c:c + band_chunks], axis=0) for p in range(n_pairs)]
        y = _attend(q[lo_row:lo_row + CHUNK], ks, vs, sinks_ref, key_ok, grp)
        y_ref[lo_row:lo_row + CHUNK] = (_rms(y) * g_ref[...]).astype(BF16)


def _swa_banded(proj, sinks, g, n_kv, head_dim, window, col_q):
    b, t, _ = proj.shape
    assert 2 * head_dim == LANES and CHUNK == head_dim
    n_heads = sinks.shape[0]
    att_w = n_heads * head_dim
    kv_w = n_kv * head_dim
    rb = _tile(ATTN_CHUNKS_PER_STEP * CHUNK, t)
    assert window % CHUNK == 0 and t % CHUNK == 0 and rb >= window
    cos, sin = _rope_tables(jnp.arange(t, dtype=jnp.int32), head_dim)
    col_k = (col_q + att_w) // kv_w
    kern = functools.partial(_swa_banded_kernel, grp=n_heads // n_kv, window=window)
    return pl.pallas_call(
        kern,
        grid=(b, t // rb),
        in_specs=[pl.BlockSpec(memory_space=pltpu.SMEM),
                  pl.BlockSpec((None, rb, att_w), lambda i, s: (i, s, col_q // att_w)),
                  pl.BlockSpec((None, rb, kv_w), lambda i, s: (i, s, col_k)),
                  pl.BlockSpec((None, rb, kv_w), lambda i, s: (i, s, col_k + 1)),
                  pl.BlockSpec((rb, LANES), lambda i, s: (s, 0)),
                  pl.BlockSpec((rb, LANES), lambda i, s: (s, 0)),
                  pl.BlockSpec((1, att_w), lambda i, s: (0, 0))],
        out_specs=[pl.BlockSpec((None, rb, att_w), lambda i, s: (i, s, 0)),
                   pl.BlockSpec((None, rb, kv_w), lambda i, s: (i, s, 0))],
        out_shape=[jax.ShapeDtypeStruct((b, t, att_w), BF16),
                   jax.ShapeDtypeStruct((b, t, kv_w), F32)],
        scratch_shapes=[pltpu.VMEM((window + rb, kv_w), BF16),
                        pltpu.VMEM((window + rb, kv_w), BF16)],
        compiler_params=_params(("arbitrary", "arbitrary")),
        name="swa_banded",
    )(sinks, proj, proj, proj, cos, sin, g.reshape(1, att_w))


def _swa_cached_kernel(sinks_ref, q_ref, k_ref, v_ref, ck_ref, cv_ref, cos_ref, sin_ref, g_ref,
                       y_ref, kr_ref, *, grp):
    head_dim = LANES // 2
    t, kv_w = k_ref.shape
    window = ck_ref.shape[0]
    n_pairs = kv_w // LANES
    cos, sin = cos_ref[...], sin_ref[...]
    k = _rope(k_ref[...], cos, sin, head_dim)
    kr_ref[...] = k
    q = _rope(q_ref[...], cos, sin, head_dim)
    n_keys = window + t
    pad = -n_keys % CHUNK
    tail = [jnp.zeros((pad, kv_w), BF16)] if pad else []
    kb = jnp.concatenate([ck_ref[...].astype(BF16), k.astype(BF16)] + tail, axis=0)
    vb = jnp.concatenate([cv_ref[...].astype(BF16), v_ref[...].astype(BF16)] + tail, axis=0)
    ks = [_pair_blocks(kb[:, p * LANES:(p + 1) * LANES]) for p in range(n_pairs)]
    vs = [_pair_blocks(vb[:, p * LANES:(p + 1) * LANES]) for p in range(n_pairs)]
    lane_key = lax.broadcasted_iota(jnp.int32, (1, LANES), 1) % CHUNK
    key_ok = [c * CHUNK + lane_key < n_keys for c in range((n_keys + pad) // CHUNK)]
    y = _attend(q, ks, vs, sinks_ref, key_ok, grp)
    y_ref[...] = (_rms(y) * g_ref[...]).astype(BF16)


def _swa_cached(proj, cache_k, cache_v, sinks, g, n_kv, head_dim, col_q):
    b, t, _ = proj.shape
    assert 2 * head_dim == LANES and CHUNK == head_dim
    window = cache_k.shape[1]
    n_heads = sinks.shape[0]
    att_w = n_heads * head_dim
    kv_w = n_kv * head_dim
    cos, sin = _rope_tables(PAST_LEN + jnp.arange(t, dtype=jnp.int32), head_dim)
    col_k = (col_q + att_w) // kv_w
    kern = functools.partial(_swa_cached_kernel, grp=n_heads // n_kv)
    return pl.pallas_call(
        kern,
        grid=(b,),
        in_specs=[pl.BlockSpec(memory_space=pltpu.SMEM),
                  pl.BlockSpec((None, t, att_w), lambda i: (i, 0, col_q // att_w)),
                  pl.BlockSpec((None, t, kv_w), lambda i: (i, 0, col_k)),
                  pl.BlockSpec((None, t, kv_w), lambda i: (i, 0, col_k + 1)),
                  pl.BlockSpec((None, window, kv_w), lambda i: (i, 0, 0)),
                  pl.BlockSpec((None, window, kv_w), lambda i: (i, 0, 0)),
                  pl.BlockSpec((t, LANES), lambda i: (0, 0)),
                  pl.BlockSpec((t, LANES), lambda i: (0, 0)),
                  pl.BlockSpec((1, att_w), lambda i: (0, 0))],
        out_specs=[pl.BlockSpec((None, t, att_w), lambda i: (i, 0, 0)),
                   pl.BlockSpec((None, t, kv_w), lambda i: (i, 0, 0))],
        out_shape=[jax.ShapeDtypeStruct((b, t, att_w), BF16),
                   jax.ShapeDtypeStruct((b, t, kv_w), F32)],
        compiler_params=_params(("arbitrary",)),
        name="swa_cached",
    )(sinks, proj, proj, proj, cache_k, cache_v, cos, sin, g.reshape(1, att_w))


def _outproj_kernel(yc_ref, ya_ref, wt_ref, wb_ref, b_ref, x_ref, g1_ref, o_ref):
    bb, tt, c = yc_ref.shape
    acc = jnp.dot(yc_ref[...].reshape(bb * tt, c), wt_ref[...], preferred_element_type=F32)
    acc = acc + jnp.dot(ya_ref[...].reshape(bb * tt, c), wb_ref[...], preferred_element_type=F32)
    acc = acc + b_ref[...]
    o_ref[...] = x_ref[...] + g1_ref[...] * acc.reshape(o_ref.shape)


def _outproj(yc, ya, w_bf, b_out, x, mod):
    b, t, d = x.shape
    c = yc.shape[2]
    assert 2 * c == w_bf.shape[0]
    bb, tt = _token_blocks(b, t, ROW_TILE)
    tn = _tile(OUTPROJ_COL_TILE, d)
    return pl.pallas_call(
        _outproj_kernel,
        grid=(b // bb, t // tt, d // tn),
        in_specs=[pl.BlockSpec((bb, tt, c), lambda i, s, j: (i, s, 0)),
                  pl.BlockSpec((bb, tt, c), lambda i, s, j: (i, s, 0)),
                  pl.BlockSpec((c, tn), lambda i, s, j: (0, j)),
                  pl.BlockSpec((c, tn), lambda i, s, j: (1, j)),
                  pl.BlockSpec((1, tn), lambda i, s, j: (0, j)),
                  pl.BlockSpec((bb, tt, tn), lambda i, s, j: (i, s, j)),
                  pl.BlockSpec((bb, 1, tn), lambda i, s, j: (i, 0, 2 * (d // tn) + j))],
        out_specs=pl.BlockSpec((bb, tt, tn), lambda i, s, j: (i, s, j)),
        out_shape=jax.ShapeDtypeStruct((b, t, d), F32),
        compiler_params=_params(("arbitrary", "arbitrary", "arbitrary")),
        name="outproj",
    )(yc, ya, w_bf, w_bf, b_out.reshape(1, d), x, mod)


def _router_kernel(x_ref, sh_ref, sc_ref, g_ref, whi_ref, wlo_ref, br_ref, c0_ref, *rest, has_prev):
    hp_ref, idx_ref, gate_ref, rank_ref, cnt_ref, carry = rest[1:] if has_prev else rest
    bb, tt, d = x_ref.shape
    rows = bb * tt
    n_exp = whi_ref.shape[1]

    @pl.when((pl.program_id(0) == 0) & (pl.program_id(1) == 0))
    def _():
        carry[...] = c0_ref[...]

    y = _rms(x_ref[...]) * g_ref[...]
    h = (y * (1.0 + sc_ref[...]) + sh_ref[...]).reshape(rows, d)
    h_hi = h.astype(BF16)
    h_hi32 = h_hi.astype(F32)
    bits = lax.bitcast_convert_type(h_hi32, U32)
    hp_ref[...] = (bits[:, d // 2:] & jnp.uint32(0xFFFF0000)) | (bits[:, :d // 2] >> 16)

    h_lo = (h - h_hi32).astype(BF16)
    logits = jnp.dot(h_hi, wlo_ref[...], preferred_element_type=F32)
    logits = logits + jnp.dot(h_lo, whi_ref[...], preferred_element_type=F32)
    logits = logits + jnp.dot(h_hi, whi_ref[...], preferred_element_type=F32) + br_ref[...]
    col = lax.broadcasted_iota(jnp.int32, logits.shape, 1).astype(F32)
    work = logits
    vals, idxs = [], []
    for _ in range(TOP_K):
        m = jnp.max(work, axis=-1, keepdims=True)
        sel = jnp.min(jnp.where(work == m, col, float(n_exp)), axis=-1, keepdims=True)
        vals.append(m)
        idxs.append(sel)
        work = jnp.where(col == sel, -jnp.inf, work)
    exps = [jnp.exp(v - vals[0]) for v in vals]
    den = exps[0]
    for e in exps[1:]:
        den = den + e
    idx_ref[...] = jnp.concatenate(idxs, axis=1).astype(jnp.int32)
    gate_ref[...] = jnp.concatenate([e / den for e in exps], axis=1)

    onehots = [(col == sel).astype(F32) for sel in idxs]
    cnt = onehots[0]
    for oh in onehots[1:]:
        cnt = cnt + oh
    earlier = (lax.broadcasted_iota(jnp.int32, (rows, rows), 0)
               > lax.broadcasted_iota(jnp.int32, (rows, rows), 1)).astype(BF16)
    before = jnp.dot(earlier, cnt.astype(BF16), preferred_element_type=F32) + carry[...]
    ranks = [jnp.sum(oh * before, axis=-1, keepdims=True) for oh in onehots]
    rank_ref[...] = jnp.concatenate(ranks, axis=1).astype(jnp.int32)
    carry[...] = carry[...] + jnp.sum(cnt, axis=0, keepdims=True)
    cnt_ref[...] = carry[...]


def _router(x1, mod, g, w_hi, w_lo, b_router, counts_before, hp_prev, tok_offset, n_tok_all):
    b, t, d = x1.shape
    n_exp = w_hi.shape[1]
    bb, tt = _token_blocks(b, t, FFN_ROW_TILE)
    rows = bb * tt
    n_tok = b * t
    assert tok_offset % rows == 0
    blk0 = tok_offset // rows
    flat = lambda i, s: (i * (t // tt) + s, 0)
    has_prev = hp_prev is not None
    in_specs = [pl.BlockSpec((bb, tt, d), lambda i, s: (i, s, 0)),
                pl.BlockSpec((bb, 1, d), lambda i, s: (i, 0, 3)),
                pl.BlockSpec((bb, 1, d), lambda i, s: (i, 0, 4)),
                pl.BlockSpec((1, d), lambda i, s: (0, 0)),
                pl.BlockSpec((d, n_exp), lambda i, s: (0, 0)),
                pl.BlockSpec((d, n_exp), lambda i, s: (0, 0)),
                pl.BlockSpec((1, n_exp), lambda i, s: (0, 0)),
                pl.BlockSpec((1, n_exp), lambda i, s: (0, 0))]
    args = [x1, mod, mod, g.reshape(1, d), w_hi, w_lo, b_router.reshape(1, n_exp), counts_before]
    if has_prev:
        in_specs.append(pl.BlockSpec(memory_space=pl.ANY))
        args.append(hp_prev)
    return pl.pallas_call(
        functools.partial(_router_kernel, has_prev=has_prev),
        grid=(b // bb, t // tt),
        in_specs=in_specs,
        out_specs=[pl.BlockSpec((rows, d // 2), lambda i, s: (blk0 + i * (t // tt) + s, 0)),
                   pl.BlockSpec((rows, TOP_K), flat),
                   pl.BlockSpec((rows, TOP_K), flat),
                   pl.BlockSpec((rows, TOP_K), flat),
                   pl.BlockSpec((1, n_exp), lambda i, s: (0, 0))],
        out_shape=[jax.ShapeDtypeStruct((n_tok_all, d // 2), U32),
                   jax.ShapeDtypeStruct((n_tok, TOP_K), jnp.int32),
                   jax.ShapeDtypeStruct((n_tok, TOP_K), F32),
                   jax.ShapeDtypeStruct((n_tok, TOP_K), jnp.int32),
                   jax.ShapeDtypeStruct((1, n_exp), F32)],
        scratch_shapes=[pltpu.VMEM((1, n_exp), F32)],
        input_output_aliases={len(args) - 1: 0} if has_prev else {},
        compiler_params=_params(("arbitrary", "arbitrary")),
        name="router",
    )(*args)


def _issue_row_copies(idx_tile_ref, idx_smem, sem_idx, src_ref, dst_ref, sem):
    cp = pltpu.make_async_copy(idx_tile_ref.at[0, 0], idx_smem, sem_idx)
    cp.start()
    cp.wait()

    def body(g, carry):
        for u in range(ROW_COPY_UNROLL):
            r = g * ROW_COPY_UNROLL + u
            pltpu.make_async_copy(src_ref.at[pl.ds(idx_smem[r], 1)], dst_ref.at[pl.ds(r, 1)], sem).start(
                priority=u % 2)
        return carry

    lax.fori_loop(0, idx_smem.shape[0] // ROW_COPY_UNROLL, body, 0)


def _wait_row_copies(src_ref, dst_ref, sem):
    pltpu.make_async_copy(src_ref.at[pl.ds(0, dst_ref.shape[0])], dst_ref, sem).wait()


def _gather_kernel(nused_ref, idx0_ref, idxn_ref, src_ref, o_ref, idx_smem, buf, sem_idx, sem_rows):
    i = pl.program_id(0)
    n = nused_ref[0]
    half = src_ref.shape[1]

    @pl.when(i == 0)
    def _():
        _issue_row_copies(idx0_ref, idx_smem, sem_idx, src_ref, buf.at[0], sem_rows.at[0])

    @pl.when(i + 1 < n)
    def _():
        nxt = (i + 1) % 2
        _issue_row_copies(idxn_ref, idx_smem, sem_idx, src_ref, buf.at[nxt], sem_rows.at[nxt])

    @pl.when(i < n)
    def _():
        slot = i % 2
        _wait_row_copies(src_ref, buf.at[slot], sem_rows.at[slot])
        w = buf[slot]
        o_ref[:, :half] = _pair_rows(lax.bitcast_convert_type(w << 16, F32).astype(BF16))
        o_ref[:, half:] = _pair_rows(lax.bitcast_convert_type(w & jnp.uint32(0xFFFF0000), F32).astype(BF16))


def _pair_rows(x):
    return pltpu.bitcast(x, U32)


def _gather_rows(src_packed, row_src, n_used, rows):
    p = row_src.shape[0]
    half = src_packed.shape[1]
    nblk = p // rows
    clamp = lambda i, nu: jnp.minimum(i, nu[0] - 1)
    return pl.pallas_call(
        _gather_kernel,
        grid_spec=pltpu.PrefetchScalarGridSpec(
            num_scalar_prefetch=1,
            grid=(nblk,),
            in_specs=[pl.BlockSpec((1, 1, rows), lambda i, nu: (0, 0, 0)),
                      pl.BlockSpec((1, 1, rows), lambda i, nu: (clamp(i + 1, nu), 0, 0)),
                      pl.BlockSpec(memory_space=pl.ANY)],
            out_specs=pl.BlockSpec((rows // 2, 2 * half), lambda i, nu: (clamp(i, nu), 0)),
            scratch_shapes=[pltpu.SMEM((rows,), jnp.int32),
                            pltpu.VMEM((2, rows, half), U32),
                            pltpu.SemaphoreType.DMA(()),
                            pltpu.SemaphoreType.DMA((2,))]),
        out_shape=jax.ShapeDtypeStruct((p // 2, 2 * half), U32),
        compiler_params=_params(("arbitrary",)),
        name="moe_gather",
    )(n_used, row_src.reshape(nblk, 1, rows), row_src.reshape(nblk, 1, rows), src_packed)


def _stream_expert_tiles(first_ref, ntile_ref, ntot_ref, src_hbm, dst_hbm, xbuf, obuf, sem_in, sem_out,
                         compute_first, compute):
    j, e = pl.program_id(0), pl.program_id(1)
    nj, ne = pl.num_programs(0), pl.num_programs(1)
    n_slots = xbuf.shape[0]
    lookahead = n_slots - 1
    in_rows = xbuf.shape[1]
    out_rows = obuf.shape[1]
    tn = obuf.shape[2]
    n_total = ntot_ref[0]
    n_seq = nj * n_total

    def x_copy(t, slot):
        r0 = pl.multiple_of(t * in_rows, in_rows)
        return pltpu.make_async_copy(src_hbm.at[pl.ds(r0, in_rows)], xbuf.at[slot], sem_in.at[slot])

    def x_start(seq):
        x_copy(lax.rem(seq, n_total), lax.rem(seq, n_slots)).start(priority=TILE_COPY_PRIORITY)

    def o_copy(t, slot):
        r0 = pl.multiple_of(t * out_rows, out_rows)
        c0 = pl.multiple_of(j * tn, tn)
        return pltpu.make_async_copy(obuf.at[slot], dst_hbm.at[pl.ds(r0, out_rows), pl.ds(c0, tn)],
                                     sem_out.at[slot])

    @pl.when((j == 0) & (e == 0))
    def _():
        for seq in range(lookahead):
            @pl.when(seq < n_seq)
            def _():
                x_start(jnp.int32(seq))

    def step(t, compute_fn):
        seq = j * n_total + t
        in_slot = lax.rem(seq, n_slots)
        out_slot = lax.rem(seq, 2)
        x_copy(0, in_slot).wait()

        @pl.when(seq + lookahead < n_seq)
        def _():
            x_start(seq + lookahead)

        @pl.when(seq >= 2)
        def _():
            o_copy(t, out_slot).wait()

        obuf[out_slot] = compute_fn(xbuf[in_slot])
        o_copy(t, out_slot).start()

    t0 = first_ref[e]
    nt = ntile_ref[e]

    @pl.when(nt > 0)
    def _():
        step(t0, compute_first)

    def body(t, carry):
        step(t, compute)
        return carry

    lax.fori_loop(t0 + 1, t0 + nt, body, 0)

    @pl.when((j == nj - 1) & (e == ne - 1))
    def _():
        @pl.when(n_seq >= 2)
        def _():
            o_copy(0, lax.rem(n_seq, 2)).wait()

        o_copy(0, lax.rem(n_seq - 1, 2)).wait()


def _swiglu(g, u):
    g = jnp.minimum(g, SWIGLU_LIMIT)
    u = jnp.clip(u, -SWIGLU_LIMIT, SWIGLU_LIMIT)
    return (u + 1.0) * g * jax.nn.sigmoid(SWIGLU_ALPHA * g)


def _moe_up_kernel(first_ref, ntile_ref, eff_ref, ntot_ref, x_hbm, wg_ref, wu_ref, bg_ref, bu_ref, act_hbm,
                   xbuf, obuf, wg_bf, wu_bf, sem_in, sem_out):
    kc = wg_ref.shape[0] // MOE_CAST_CHUNKS

    def compute_first(x_words):
        x = pltpu.bitcast(x_words, BF16)
        g = bg_ref[...]
        u = bu_ref[...]
        for c in range(MOE_CAST_CHUNKS):
            ks = slice(c * kc, (c + 1) * kc)
            wg_c = wg_ref[ks, :].astype(BF16)
            wu_c = wu_ref[ks, :].astype(BF16)
            wg_bf[ks, :] = wg_c
            wu_bf[ks, :] = wu_c
            g = g + jnp.dot(x[:, ks], wg_c, preferred_element_type=F32)
            u = u + jnp.dot(x[:, ks], wu_c, preferred_element_type=F32)
        return _pair_rows(_swiglu(g, u).astype(BF16))

    def compute(x_words):
        x = pltpu.bitcast(x_words, BF16)
        g = jnp.dot(x, wg_bf[...], preferred_element_type=F32) + bg_ref[...]
        u = jnp.dot(x, wu_bf[...], preferred_element_type=F32) + bu_ref[...]
        return _pair_rows(_swiglu(g, u).astype(BF16))

    _stream_expert_tiles(first_ref, ntile_ref, ntot_ref, x_hbm, act_hbm, xbuf, obuf, sem_in, sem_out,
                         compute_first, compute)


def _moe_up(xs, w_gate_up, b_gate_up, tile_first, tile_count, eff_expert, n_tiles):
    p2, d = xs.shape
    n_exp, _, f2 = w_gate_up.shape
    f = f2 // 2
    tf = _tile(MOE_FF_TILE, f)
    nj = f // tf
    return pl.pallas_call(
        _moe_up_kernel,
        grid_spec=pltpu.PrefetchScalarGridSpec(
            num_scalar_prefetch=4,
            grid=(nj, n_exp),
            in_specs=[pl.BlockSpec(memory_space=pl.ANY),
                      pl.BlockSpec((None, d, tf), lambda j, e, fi, nt, ef, ntot: (ef[e], 0, j)),
                      pl.BlockSpec((None, d, tf), lambda j, e, fi, nt, ef, ntot: (ef[e], 0, nj + j)),
                      pl.BlockSpec((None, 1, tf), lambda j, e, fi, nt, ef, ntot: (ef[e], 0, j)),
                      pl.BlockSpec((None, 1, tf), lambda j, e, fi, nt, ef, ntot: (ef[e], 0, nj + j))],
            out_specs=pl.BlockSpec(memory_space=pl.ANY),
            scratch_shapes=[pltpu.VMEM((MOE_IN_SLOTS, MOE_ROWS // 2, d), U32),
                            pltpu.VMEM((2, MOE_ROWS // 2, tf), U32),
                            pltpu.VMEM((d, tf), BF16),
                            pltpu.VMEM((d, tf), BF16),
                            pltpu.SemaphoreType.DMA((MOE_IN_SLOTS,)),
                            pltpu.SemaphoreType.DMA((2,))]),
        out_shape=jax.ShapeDtypeStruct((p2, f), U32),
        compiler_params=_params(("arbitrary", "arbitrary")),
        name="moe_up",
    )(tile_first, tile_count, eff_expert, n_tiles, xs, w_gate_up, w_gate_up,
      b_gate_up.reshape(n_exp, 1, f2), b_gate_up.reshape(n_exp, 1, f2))


def _moe_down_kernel(first_ref, ntile_ref, eff_ref, ntot_ref, a_hbm, w_ref, b_ref, y_hbm,
                     xbuf, obuf, w_bf, sem_in, sem_out):
    kc = w_ref.shape[0] // MOE_CAST_CHUNKS

    def compute_first(a_words):
        a = pltpu.bitcast(a_words, BF16)
        y = b_ref[...]
        for c in range(MOE_CAST_CHUNKS):
            ks = slice(c * kc, (c + 1) * kc)
            w_c = w_ref[ks, :].astype(BF16)
            w_bf[ks, :] = w_c
            y = y + jnp.dot(a[:, ks], w_c, preferred_element_type=F32)
        return y

    def compute(a_words):
        return jnp.dot(pltpu.bitcast(a_words, BF16), w_bf[...], preferred_element_type=F32) + b_ref[...]

    _stream_expert_tiles(first_ref, ntile_ref, ntot_ref, a_hbm, y_hbm, xbuf, obuf, sem_in, sem_out,
                         compute_first, compute)


def _moe_down(act, w_down, b_down, tile_first, tile_count, eff_expert, n_tiles):
    p2, f = act.shape
    n_exp, _, d = w_down.shape
    tn = _tile(MOE_OUT_TILE, d)
    return pl.pallas_call(
        _moe_down_kernel,
        grid_spec=pltpu.PrefetchScalarGridSpec(
            num_scalar_prefetch=4,
            grid=(d // tn, n_exp),
            in_specs=[pl.BlockSpec(memory_space=pl.ANY),
                      pl.BlockSpec((None, f, tn), lambda j, e, fi, nt, ef, ntot: (ef[e], 0, j)),
                      pl.BlockSpec((None, 1, tn), lambda j, e, fi, nt, ef, ntot: (ef[e], 0, j))],
            out_specs=pl.BlockSpec(memory_space=pl.ANY),
            scratch_shapes=[pltpu.VMEM((MOE_IN_SLOTS, MOE_ROWS // 2, f), U32),
                            pltpu.VMEM((2, MOE_ROWS, tn), F32),
                            pltpu.VMEM((f, tn), BF16),
                            pltpu.SemaphoreType.DMA((MOE_IN_SLOTS,)),
                            pltpu.SemaphoreType.DMA((2,))]),
        out_shape=jax.ShapeDtypeStruct((2 * p2, d), F32),
        compiler_params=_params(("arbitrary", "arbitrary")),
        name="moe_down",
    )(tile_first, tile_count, eff_expert, n_tiles, act, w_down, b_down.reshape(n_exp, 1, d))


def _combine_kernel(pos0_ref, posn_ref, y_ref, gate_ref, x_ref, g2_ref, gf_ref, o_ref,
                    pos_smem, buf, sem_idx, sem_rows):
    bb, tt, d = x_ref.shape
    rows = bb * tt
    i = pl.program_id(0)

    @pl.when(i == 0)
    def _():
        _issue_row_copies(pos0_ref, pos_smem, sem_idx, y_ref, buf.at[0], sem_rows.at[0])

    @pl.when(i + 1 < pl.num_programs(0))
    def _():
        nxt = (i + 1) % 2
        _issue_row_copies(posn_ref, pos_smem, sem_idx, y_ref, buf.at[nxt], sem_rows.at[nxt])

    slot = i % 2
    _wait_row_copies(y_ref, buf.at[slot], sem_rows.at[slot])
    gates = gate_ref[...]
    ffn = gates[:, 0:1] * buf[slot, 0:rows]
    for k in range(1, TOP_K):
        ffn = ffn + gates[:, k:k + 1] * buf[slot, k * rows:(k + 1) * rows]
    x2 = x_ref[...] + g2_ref[...] * ffn.reshape(bb, tt, d)
    o_ref[...] = _rms(x2) * gf_ref[...]


def _combine(y_sorted, pos, gates, x1, mod, gf, tok_offset):
    b, t, d = x1.shape
    bb, tt = _token_blocks(b, t, COMBINE_ROWS)
    rows = bb * tt
    assert tok_offset % rows == 0
    blk0 = tok_offset // rows
    n_tok = pos.shape[0]
    nblk = (b // bb) * (t // tt)
    ts = t // tt
    pos_tiles = pos.reshape(n_tok // rows, rows, TOP_K).transpose(0, 2, 1).reshape(n_tok // rows, 1, TOP_K * rows)
    return pl.pallas_call(
        _combine_kernel,
        grid=(nblk,),
        in_specs=[pl.BlockSpec((1, 1, TOP_K * rows), lambda i: (blk0, 0, 0)),
                  pl.BlockSpec((1, 1, TOP_K * rows), lambda i: (blk0 + jnp.minimum(i + 1, nblk - 1), 0, 0)),
                  pl.BlockSpec(memory_space=pl.ANY),
                  pl.BlockSpec((rows, TOP_K), lambda i: (blk0 + i, 0)),
                  pl.BlockSpec((bb, tt, d), lambda i: (i // ts, i % ts, 0)),
                  pl.BlockSpec((bb, 1, d), lambda i: (i // ts, 0, 5)),
                  pl.BlockSpec((1, d), lambda i: (0, 0))],
        out_specs=pl.BlockSpec((bb, tt, d), lambda i: (i // ts, i % ts, 0)),
        out_shape=jax.ShapeDtypeStruct((b, t, d), F32),
        scratch_shapes=[pltpu.SMEM((TOP_K * rows,), jnp.int32),
                        pltpu.VMEM((2, TOP_K * rows, d), F32),
                        pltpu.SemaphoreType.DMA(()),
                        pltpu.SemaphoreType.DMA((2,))],
        compiler_params=_params(("arbitrary",)),
        name="moe_combine",
    )(pos_tiles, pos_tiles, y_sorted, gates, x1, mod, gf.reshape(1, d))


def _routing(top_idx, rank, counts, rows):
    n_tok = top_idx.shape[0]
    n_exp = counts.shape[0]
    n_assign = n_tok * TOP_K
    tile_count = (counts + rows - 1) // rows
    tile_end = jnp.cumsum(tile_count)
    tile_first = tile_end - tile_count
    pos = (tile_first * rows)[top_idx] + rank
    p = n_assign + n_exp * rows
    token_of = jnp.arange(n_assign, dtype=jnp.int32) // TOP_K
    row_token = jnp.zeros((p,), jnp.int32).at[pos.reshape(-1)].set(
        token_of, unique_indices=True, mode="promise_in_bounds")
    ids = jnp.where(tile_count > 0, jnp.arange(n_exp, dtype=jnp.int32), -1)
    eff = lax.cummax(ids)
    eff = jnp.where(eff < 0, jnp.argmax(tile_count > 0).astype(jnp.int32), eff)
    i32 = lambda a: a.astype(jnp.int32)
    return i32(pos), row_token, i32(tile_first), i32(tile_count), i32(eff), i32(tile_end[-1:])


def kernel(x_prompt, x_sample, cache_k, cache_v, state_conv, c_prompt, c_sample, w_ada, b_ada, norm_mix_g, norm_ffn_g, norm_final_g, w_in, b_in, conv_w, conv_b, sinks, out_norm_conv_g, out_norm_attn_g, w_out, b_out, w_router, b_router, w_gate_up, b_gate_up, w_down, b_down):
    d = x_prompt.shape[2]
    depth, dec_b, window, n_kv, head_dim = cache_k.shape
    assert depth == 1
    conv_ch = conv_w.shape[1]
    n_exp = w_router.shape[1]
    n_heads = sinks.shape[0]
    bp, tp, _ = x_prompt.shape
    bs, ts, _ = x_sample.shape
    col_q = 3 * conv_ch
    kv_w = n_kv * head_dim
    att_w = n_heads * head_dim

    mod = _adaln(jnp.concatenate([c_prompt, c_sample], axis=0), w_ada, b_ada)
    mod_p = mod[:bp].reshape(bp, 1, 6 * d)
    mod_s = mod[bp:].reshape(bs, 1, 6 * d)
    order = _head_order(n_heads, n_kv)
    w_in_bf = w_in.astype(BF16)
    w_in_bf = lax.dynamic_update_slice(
        w_in_bf, _permute_blocks(w_in_bf[:, col_q:col_q + att_w], order, head_dim, 1, 0), (0, col_q))
    b_in_p = _permute_blocks(b_in, order, head_dim, 0, col_q)
    w_out_bf = w_out.astype(BF16)
    w_out_bf = lax.dynamic_update_slice(
        w_out_bf, _permute_blocks(w_out_bf[conv_ch:], order, head_dim, 0, 0), (conv_ch, 0))
    attn_g = _permute_blocks(out_norm_attn_g, order, head_dim, 0, 0)
    wr_hi = w_router.astype(BF16)
    wr_lo = (w_router - wr_hi.astype(F32)).astype(BF16)
    n_p, n_s = bp * tp, bs * ts

    def mixer(x, modx, conv_prev, ck, cv, counts_before, hp_prev, tok_offset):
        b, t, _ = x.shape
        proj = _inproj(x, modx, norm_mix_g, w_in_bf, b_in_p)
        yc, conv_state = _conv_mix(proj, conv_prev, conv_w, conv_b, out_norm_conv_g)
        if ck is None:
            ya, k_rot = _swa_banded(proj, sinks, attn_g, n_kv, head_dim, window, col_q)
            keep = window
        else:
            ya, k_rot = _swa_cached(proj, ck, cv, sinks, attn_g, n_kv, head_dim, col_q)
            keep = t
        k_rows = k_rot[:, t - keep:].reshape(1, b, keep, n_kv, head_dim)
        v_rows = proj[:, t - keep:, col_q + att_w + kv_w:].reshape(1, b, keep, n_kv, head_dim)
        x1 = _outproj(yc, ya, w_out_bf, b_out, x, modx)
        routed = _router(x1, modx, norm_ffn_g, wr_hi, wr_lo, b_router, counts_before, hp_prev,
                         tok_offset, n_p + n_s)
        return x1, routed, conv_state, k_rows, v_rows

    zero_state = jnp.zeros((bp, 2, conv_ch), F32)
    x1p, (hp_buf, idxp, gatep, rankp, cntp), conv_p, k_prompt, v_prompt = mixer(
        x_prompt, mod_p, zero_state, None, None, jnp.zeros((1, n_exp), F32), None, 0)
    x1s, (h2_packed, idxs, gates_s, ranks, cnt), conv_s, k_sample, v_sample = mixer(
        x_sample, mod_s, state_conv[0],
        cache_k[0].reshape(dec_b, window, kv_w), cache_v[0].reshape(dec_b, window, kv_w), cntp, hp_buf, n_p)

    top_idx = jnp.concatenate([idxp, idxs], axis=0)
    gates = jnp.concatenate([gatep, gates_s], axis=0)
    rank = jnp.concatenate([rankp, ranks], axis=0)
    pos, row_token, tile_first, tile_count, eff, n_tiles = _routing(
        top_idx, rank, cnt[0].astype(jnp.int32), MOE_ROWS)
    xs = _gather_rows(h2_packed, row_token, n_tiles, MOE_ROWS)
    act = _moe_up(xs, w_gate_up, b_gate_up, tile_first, tile_count, eff, n_tiles)
    y_sorted = _moe_down(act, w_down, b_down, tile_first, tile_count, eff, n_tiles)
    y_prompt = _combine(y_sorted, pos, gates, x1p, mod_p, norm_final_g, 0)
    y_sample = _combine(y_sorted, pos, gates, x1s, mod_s, norm_final_g, n_p)
    return (y_prompt, y_sample, k_prompt, v_prompt, conv_p[None], k_sample, v_sample, conv_s[None])
```

```python
import functools

import jax
import jax.numpy as jnp
from jax import lax
from jax.experimental import pallas as pl
from jax.experimental.pallas import tpu as pltpu

F32 = jnp.float32
BF16 = jnp.bfloat16
U32 = jnp.uint32

CHUNK = 64
TOP_K = 4
PAST_LEN = 2048
ROPE_THETA = 10000.0
NORM_EPS = 1e-5
SWIGLU_LIMIT = 7.0
SWIGLU_ALPHA = 1.702

LANES = 128
SUBLANES = 8
VMEM_LIMIT_BYTES = 58 * 1024 * 1024

ROW_TILE = 512
ADA_COL_TILE = 512
INPROJ_COL_TILE = 1024
OUTPROJ_COL_TILE = 1024
CONV_ROW_TILE = 256
ATTN_CHUNKS_PER_STEP = 4
FFN_ROW_TILE = 256
MOE_ROWS = 256
MOE_FF_TILE = 512
MOE_OUT_TILE = 1024
MOE_IN_SLOTS = 5
MOE_CAST_CHUNKS = 4
COMBINE_ROWS = 128
ROW_COPY_UNROLL = 8
INDEX_BATCH_TILES = 8
TILE_COPY_PRIORITY = 1


def _tile(pref, dim):
    t = min(pref, dim)
    while dim % t:
        t -= 1
    return t


def _params(sem):
    return pltpu.CompilerParams(dimension_semantics=sem, vmem_limit_bytes=VMEM_LIMIT_BYTES)


def _rms(x):
    return x * lax.rsqrt(jnp.mean(x * x, axis=-1, keepdims=True) + NORM_EPS)


def _token_blocks(b, t, rows_pref):
    if t >= rows_pref:
        return 1, _tile(rows_pref, t)
    return _tile(max(rows_pref // t, 1), b), t


def _adaln_kernel(c_ref, w_ref, b_ref, o_ref):
    c = c_ref[...]
    a = (c * jax.nn.sigmoid(c)).astype(BF16)
    o_ref[...] = jnp.dot(a, w_ref[...].astype(BF16), preferred_element_type=F32) + b_ref[...]


def _adaln(c, w_ada, b_ada):
    nb, d = c.shape
    n = w_ada.shape[1]
    tn = _tile(ADA_COL_TILE, n)
    return pl.pallas_call(
        _adaln_kernel,
        grid=(n // tn,),
        in_specs=[pl.BlockSpec((nb, d), lambda j: (0, 0)),
                  pl.BlockSpec((d, tn), lambda j: (0, j)),
                  pl.BlockSpec((1, tn), lambda j: (0, j))],
        out_specs=pl.BlockSpec((nb, tn), lambda j: (0, j)),
        out_shape=jax.ShapeDtypeStruct((nb, n), F32),
        compiler_params=_params(("arbitrary",)),
        name="adaln",
    )(c, w_ada, b_ada.reshape(1, n))


def _inproj_kernel(x_ref, sh_ref, sc_ref, g_ref, w_ref, b_ref, o_ref, h_ref):
    bb, tt, d = x_ref.shape

    @pl.when(pl.program_id(2) == 0)
    def _():
        y = _rms(x_ref[...]) * g_ref[...]
        h = y * (1.0 + sc_ref[...]) + sh_ref[...]
        h_ref[...] = h.reshape(bb * tt, d).astype(BF16)

    acc = jnp.dot(h_ref[...], w_ref[...], preferred_element_type=F32) + b_ref[...]
    o_ref[...] = acc.reshape(o_ref.shape)


def _inproj(x, mod, g, w_bf, b_in):
    b, t, d = x.shape
    n = w_bf.shape[1]
    bb, tt = _token_blocks(b, t, ROW_TILE)
    tn = _tile(INPROJ_COL_TILE, n)
    return pl.pallas_call(
        _inproj_kernel,
        grid=(b // bb, t // tt, n // tn),
        in_specs=[pl.BlockSpec((bb, tt, d), lambda i, s, j: (i, s, 0)),
                  pl.BlockSpec((bb, 1, d), lambda i, s, j: (i, 0, 0)),
                  pl.BlockSpec((bb, 1, d), lambda i, s, j: (i, 0, 1)),
                  pl.BlockSpec((1, d), lambda i, s, j: (0, 0)),
                  pl.BlockSpec((d, tn), lambda i, s, j: (0, j)),
                  pl.BlockSpec((1, tn), lambda i, s, j: (0, j))],
        out_specs=pl.BlockSpec((bb, tt, tn), lambda i, s, j: (i, s, j)),
        out_shape=jax.ShapeDtypeStruct((b, t, n), F32),
        scratch_shapes=[pltpu.VMEM((bb * tt, d), BF16)],
        compiler_params=_params(("arbitrary", "arbitrary", "arbitrary")),
        name="inproj",
    )(x, mod, mod, g.reshape(1, d), w_bf, b_in.reshape(1, n))


def _conv_kernel(bg_ref, cg_ref, xc_ref, cgp_ref, xcp_ref, st_ref, w_ref, b_ref, g_ref, y_ref, so_ref):
    tt = cg_ref.shape[0]
    u = cg_ref[...] * xc_ref[...]
    halo = cgp_ref[...] * xcp_ref[...]
    st = st_ref[...]
    first = pl.program_id(1) == 0
    um2 = jnp.where(first, st[0:1], halo[SUBLANES - 2:SUBLANES - 1])
    um1 = jnp.where(first, st[1:2], halo[SUBLANES - 1:SUBLANES])
    row = lax.broadcasted_iota(jnp.int32, u.shape, 0)
    u1 = jnp.where(row == 0, um1, pltpu.roll(u, 1, 0))
    u2 = jnp.where(row == 0, um2, jnp.where(row == 1, um1, pltpu.roll(u, 2, 0)))
    w = w_ref[...]
    conv = b_ref[...] + u2 * w[0:1]
    conv = conv + u1 * w[1:2]
    conv = conv + u * w[2:3]
    y = _rms(bg_ref[...] * conv) * g_ref[...]
    y_ref[...] = y.astype(BF16)
    so_ref[...] = u[tt - 2:tt]


def _conv_mix(proj, state, conv_w, conv_b, g):
    b, t, _ = proj.shape
    c = conv_w.shape[1]
    assert conv_w.shape[0] == 3
    tt = _tile(CONV_ROW_TILE, t)
    hs = tt // SUBLANES
    halo_map = lambda col: (lambda i, s: (i, jnp.maximum(s * hs - 1, 0), col))
    return pl.pallas_call(
        _conv_kernel,
        grid=(b, t // tt),
        in_specs=[pl.BlockSpec((None, tt, c), lambda i, s: (i, s, 0)),
                  pl.BlockSpec((None, tt, c), lambda i, s: (i, s, 1)),
                  pl.BlockSpec((None, tt, c), lambda i, s: (i, s, 2)),
                  pl.BlockSpec((None, SUBLANES, c), halo_map(1)),
                  pl.BlockSpec((None, SUBLANES, c), halo_map(2)),
                  pl.BlockSpec((None, 2, c), lambda i, s: (i, 0, 0)),
                  pl.BlockSpec((3, c), lambda i, s: (0, 0)),
                  pl.BlockSpec((1, c), lambda i, s: (0, 0)),
                  pl.BlockSpec((1, c), lambda i, s: (0, 0))],
        out_specs=[pl.BlockSpec((None, tt, c), lambda i, s: (i, s, 0)),
                   pl.BlockSpec((None, 2, c), lambda i, s: (i, 0, 0))],
        out_shape=[jax.ShapeDtypeStruct((b, t, c), BF16),
                   jax.ShapeDtypeStruct((b, 2, c), F32)],
        compiler_params=_params(("arbitrary", "arbitrary")),
        name="conv_mix",
    )(proj, proj, proj, proj, proj, state, conv_w, conv_b.reshape(1, c), g.reshape(1, c))


def _head_order(n_heads, n_kv):
    grp = n_heads // n_kv
    assert n_kv % 2 == 0
    return [(2 * p + half) * grp + g for p in range(n_kv // 2) for g in range(grp) for half in (0, 1)]


def _permute_blocks(x, order, width, axis, start):
    n = len(order) * width
    take = lambda lo, hi: lax.slice_in_dim(x, lo, hi, axis=axis)
    parts = [take(0, start)] + [take(start + h * width, start + (h + 1) * width) for h in order]
    parts.append(take(start + n, x.shape[axis]))
    return jnp.concatenate(parts, axis=axis)


def _rope_tables(pos, head_dim):
    half = head_dim // 2
    inv_freq = ROPE_THETA ** (-jnp.arange(half, dtype=F32) / half)
    ang = pos.astype(F32)[:, None] * inv_freq[None, :]
    cos, sin = jnp.cos(ang), jnp.sin(ang)
    reps = LANES // head_dim
    return (jnp.tile(jnp.concatenate([cos, cos], axis=-1), (1, reps)),
            jnp.tile(jnp.concatenate([-sin, sin], axis=-1), (1, reps)))


def _rope(x, cos, sin, head_dim):
    width = x.shape[1]
    half = head_dim // 2
    reps = width // LANES
    cos = jnp.concatenate([cos] * reps, axis=1)
    sin = jnp.concatenate([sin] * reps, axis=1)
    lane = lax.broadcasted_iota(jnp.int32, x.shape, 1)
    swapped = jnp.where(lane % head_dim < half,
                        pltpu.roll(x, width - half, 1), pltpu.roll(x, half, 1))
    return x * cos + swapped * sin


def _pair_blocks(slab):
    keys = slab.shape[0]
    lo = lax.broadcasted_iota(jnp.int32, (CHUNK, LANES), 1) < LANES // 2
    zero = jnp.zeros((CHUNK, LANES), slab.dtype)
    parts = []
    for c in range(keys // CHUNK):
        blk = slab[c * CHUNK:(c + 1) * CHUNK]
        parts += [jnp.where(lo, blk, zero), jnp.where(lo, zero, blk)]
    return jnp.concatenate(parts, axis=0)


def _attend_pair(qs, kblk, vblk, sink_vec, key_ok):
    head_dim = LANES // 2
    n_chunks = kblk.shape[0] // LANES
    lo = lax.broadcasted_iota(jnp.int32, (1, LANES), 1) < head_dim
    s = lax.dot_general(qs.astype(BF16), kblk, (((1,), (1,)), ((), ())),
                        preferred_element_type=F32) * (head_dim ** -0.5)
    tiles = [jnp.where(key_ok[c], s[:, c * LANES:(c + 1) * LANES], -jnp.inf) for c in range(n_chunks)]

    def per_half(t, reduce_fn, fill):
        a = reduce_fn(jnp.where(lo, t, fill), axis=-1, keepdims=True)
        b = reduce_fn(jnp.where(lo, fill, t), axis=-1, keepdims=True)
        return jnp.where(lo, a, b)

    tmax = tiles[0]
    for t in tiles[1:]:
        tmax = jnp.maximum(tmax, t)
    m = jnp.maximum(per_half(tmax, jnp.max, -jnp.inf), sink_vec)
    es = [jnp.exp(t - m) for t in tiles]
    esum = es[0]
    for e in es[1:]:
        esum = esum + e
    den = per_half(esum, jnp.sum, 0.0) + jnp.exp(sink_vec - m)
    e_all = jnp.concatenate([e.astype(BF16) for e in es], axis=1)
    return jnp.dot(e_all, vblk, preferred_element_type=F32) / den


def _attend(q, kslabs, vslabs, sinks_ref, key_ok, grp):
    rows = q.shape[0]
    head_dim = LANES // 2
    lo = lax.broadcasted_iota(jnp.int32, (1, LANES), 1) < head_dim
    outs = []
    for p in range(len(kslabs)):
        qs = jnp.concatenate([q[:, (p * grp + g) * LANES:(p * grp + g + 1) * LANES] for g in range(grp)], axis=0)
        sink_vec = jnp.concatenate(
            [jnp.broadcast_to(jnp.where(lo, sinks_ref[2 * p * grp + g], sinks_ref[(2 * p + 1) * grp + g]),
                              (rows, LANES)) for g in range(grp)], axis=0)
        o = _attend_pair(qs, kslabs[p], vslabs[p], sink_vec, key_ok)
        outs.extend(o[g * rows:(g + 1) * rows] for g in range(grp))
    return jnp.concatenate(outs, axis=1)


def _swa_banded_kernel(sinks_ref, q_ref, k_ref, v_ref, cos_ref, sin_ref, g_ref,
                       y_ref, kr_ref, kbuf, vbuf, *, grp, window):
    rb = q_ref.shape[0]
    head_dim = LANES // 2
    n_pairs = kbuf.shape[1] // LANES
    step = pl.program_id(1)

    @pl.when(step == 0)
    def _():
        kbuf[0:window] = jnp.zeros((window, kbuf.shape[1]), BF16)
        vbuf[0:window] = jnp.zeros((window, vbuf.shape[1]), BF16)

    @pl.when(step > 0)
    def _():
        kbuf[0:window] = kbuf[rb:rb + window]
        vbuf[0:window] = vbuf[rb:rb + window]

    cos, sin = cos_ref[...], sin_ref[...]
    k = _rope(k_ref[...], cos, sin, head_dim)
    kr_ref[...] = k
    kbuf[window:window + rb] = k.astype(BF16)
    vbuf[window:window + rb] = v_ref[...].astype(BF16)
    q = _rope(q_ref[...], cos, sin, head_dim)

    n_buf_chunks = (window + rb) // CHUNK
    kblocks = [[_pair_blocks(kbuf[c * CHUNK:(c + 1) * CHUNK, p * LANES:(p + 1) * LANES])
                for c in range(n_buf_chunks)] for p in range(n_pairs)]
    vblocks = [[_pair_blocks(vbuf[c * CHUNK:(c + 1) * CHUNK, p * LANES:(p + 1) * LANES])
                for c in range(n_buf_chunks)] for p in range(n_pairs)]
    band_chunks = window // CHUNK + 1
    lane_key = lax.broadcasted_iota(jnp.int32, (1, LANES), 1) % CHUNK
    for c in range(rb // CHUNK):
        lo_row = c * CHUNK
        key_ok = [step * rb + (c + b) * CHUNK + lane_key >= window for b in range(band_chunks)]
        ks = [jnp.concatenate(kblocks[p][c:c + band_chunks], axis=0) for p in range(n_pairs)]
        vs = [jnp.concatenate(vblocks[p][c:c + band_chunks], axis=0) for p in range(n_pairs)]
        y = _attend(q[lo_row:lo_row + CHUNK], ks, vs, sinks_ref, key_ok, grp)
        y_ref[lo_row:lo_row + CHUNK] = (_rms(y) * g_ref[...]).astype(BF16)


def _swa_banded(proj, sinks, g, n_kv, head_dim, window, col_q):
    b, t, _ = proj.shape
    assert 2 * head_dim == LANES and CHUNK == head_dim
    n_heads = sinks.shape[0]
    att_w = n_heads * head_dim
    kv_w = n_kv * head_dim
    rb = _tile(ATTN_CHUNKS_PER_STEP * CHUNK, t)
    assert window % CHUNK == 0 and t % CHUNK == 0 and rb >= window
    cos, sin = _rope_tables(jnp.arange(t, dtype=jnp.int32), head_dim)
    col_k = (col_q + att_w) // kv_w
    kern = functools.partial(_swa_banded_kernel, grp=n_heads // n_kv, window=window)
    return pl.pallas_call(
        kern,
        grid=(b, t // rb),
        in_specs=[pl.BlockSpec(memory_space=pltpu.SMEM),
                  pl.BlockSpec((None, rb, att_w), lambda i, s: (i, s, col_q // att_w)),
                  pl.BlockSpec((None, rb, kv_w), lambda i, s: (i, s, col_k)),
                  pl.BlockSpec((None, rb, kv_w), lambda i, s: (i, s, col_k + 1)),
                  pl.BlockSpec((rb, LANES), lambda i, s: (s, 0)),
                  pl.BlockSpec((rb, LANES), lambda i, s: (s, 0)),
                  pl.BlockSpec((1, att_w), lambda i, s: (0, 0))],
        out_specs=[pl.BlockSpec((None, rb, att_w), lambda i, s: (i, s, 0)),
                   pl.BlockSpec((None, rb, kv_w), lambda i, s: (i, s, 0))],
        out_shape=[jax.ShapeDtypeStruct((b, t, att_w), BF16),
                   jax.ShapeDtypeStruct((b, t, kv_w), F32)],
        scratch_shapes=[pltpu.VMEM((window + rb, kv_w), BF16),
                        pltpu.VMEM((window + rb, kv_w), BF16)],
        compiler_params=_params(("arbitrary", "arbitrary")),
        name="swa_banded",
    )(sinks, proj, proj, proj, cos, sin, g.reshape(1, att_w))


def _swa_cached_kernel(sinks_ref, q_ref, k_ref, v_ref, ck_ref, cv_ref, cos_ref, sin_ref, g_ref,
                       y_ref, kr_ref, *, grp):
    head_dim = LANES // 2
    t, kv_w = k_ref.shape
    window = ck_ref.shape[0]
    n_pairs = kv_w // LANES
    cos, sin = cos_ref[...], sin_ref[...]
    k = _rope(k_ref[...], cos, sin, head_dim)
    kr_ref[...] = k
    q = _rope(q_ref[...], cos, sin, head_dim)
    n_keys = window + t
    pad = -n_keys % CHUNK
    tail = [jnp.zeros((pad, kv_w), BF16)] if pad else []
    kb = jnp.concatenate([ck_ref[...].astype(BF16), k.astype(BF16)] + tail, axis=0)
    vb = jnp.concatenate([cv_ref[...].astype(BF16), v_ref[...].astype(BF16)] + tail, axis=0)
    ks = [_pair_blocks(kb[:, p * LANES:(p + 1) * LANES]) for p in range(n_pairs)]
    vs = [_pair_blocks(vb[:, p * LANES:(p + 1) * LANES]) for p in range(n_pairs)]
    lane_key = lax.broadcasted_iota(jnp.int32, (1, LANES), 1) % CHUNK
    key_ok = [c * CHUNK + lane_key < n_keys for c in range((n_keys + pad) // CHUNK)]
    y = _attend(q, ks, vs, sinks_ref, key_ok, grp)
    y_ref[...] = (_rms(y) * g_ref[...]).astype(BF16)


def _swa_cached(proj, cache_k, cache_v, sinks, g, n_kv, head_dim, col_q):
    b, t, _ = proj.shape
    assert 2 * head_dim == LANES and CHUNK == head_dim
    window = cache_k.shape[1]
    n_heads = sinks.shape[0]
    att_w = n_heads * head_dim
    kv_w = n_kv * head_dim
    cos, sin = _rope_tables(PAST_LEN + jnp.arange(t, dtype=jnp.int32), head_dim)
    col_k = (col_q + att_w) // kv_w
    kern = functools.partial(_swa_cached_kernel, grp=n_heads // n_kv)
    return pl.pallas_call(
        kern,
        grid=(b,),
        in_specs=[pl.BlockSpec(memory_space=pltpu.SMEM),
                  pl.BlockSpec((None, t, att_w), lambda i: (i, 0, col_q // att_w)),
                  pl.BlockSpec((None, t, kv_w), lambda i: (i, 0, col_k)),
                  pl.BlockSpec((None, t, kv_w), lambda i: (i, 0, col_k + 1)),
                  pl.BlockSpec((None, window, kv_w), lambda i: (i, 0, 0)),
                  pl.BlockSpec((None, window, kv_w), lambda i: (i, 0, 0)),
                  pl.BlockSpec((t, LANES), lambda i: (0, 0)),
                  pl.BlockSpec((t, LANES), lambda i: (0, 0)),
                  pl.BlockSpec((1, att_w), lambda i: (0, 0))],
        out_specs=[pl.BlockSpec((None, t, att_w), lambda i: (i, 0, 0)),
                   pl.BlockSpec((None, t, kv_w), lambda i: (i, 0, 0))],
        out_shape=[jax.ShapeDtypeStruct((b, t, att_w), BF16),
                   jax.ShapeDtypeStruct((b, t, kv_w), F32)],
        compiler_params=_params(("arbitrary",)),
        name="swa_cached",
    )(sinks, proj, proj, proj, cache_k, cache_v, cos, sin, g.reshape(1, att_w))


def _outproj_kernel(yc_ref, ya_ref, wt_ref, wb_ref, b_ref, x_ref, g1_ref, o_ref):
    bb, tt, c = yc_ref.shape
    acc = jnp.dot(yc_ref[...].reshape(bb * tt, c), wt_ref[...], preferred_element_type=F32)
    acc = acc + jnp.dot(ya_ref[...].reshape(bb * tt, c), wb_ref[...], preferred_element_type=F32)
    acc = acc + b_ref[...]
    o_ref[...] = x_ref[...] + g1_ref[...] * acc.reshape(o_ref.shape)


def _outproj(yc, ya, w_bf, b_out, x, mod):
    b, t, d = x.shape
    c = yc.shape[2]
    assert 2 * c == w_bf.shape[0]
    bb, tt = _token_blocks(b, t, ROW_TILE)
    tn = _tile(OUTPROJ_COL_TILE, d)
    return pl.pallas_call(
        _outproj_kernel,
        grid=(b // bb, t // tt, d // tn),
        in_specs=[pl.BlockSpec((bb, tt, c), lambda i, s, j: (i, s, 0)),
                  pl.BlockSpec((bb, tt, c), lambda i, s, j: (i, s, 0)),
                  pl.BlockSpec((c, tn), lambda i, s, j: (0, j)),
                  pl.BlockSpec((c, tn), lambda i, s, j: (1, j)),
                  pl.BlockSpec((1, tn), lambda i, s, j: (0, j)),
                  pl.BlockSpec((bb, tt, tn), lambda i, s, j: (i, s, j)),
                  pl.BlockSpec((bb, 1, tn), lambda i, s, j: (i, 0, 2 * (d // tn) + j))],
        out_specs=pl.BlockSpec((bb, tt, tn), lambda i, s, j: (i, s, j)),
        out_shape=jax.ShapeDtypeStruct((b, t, d), F32),
        compiler_params=_params(("arbitrary", "arbitrary", "arbitrary")),
        name="outproj",
    )(yc, ya, w_bf, w_bf, b_out.reshape(1, d), x, mod)


def _router_kernel(x_ref, sh_ref, sc_ref, g_ref, whi_ref, wlo_ref, br_ref, c0_ref, *rest, has_prev):
    hp_ref, idx_ref, gate_ref, rank_ref, cnt_ref, carry = rest[1:] if has_prev else rest
    bb, tt, d = x_ref.shape
    rows = bb * tt
    n_exp = whi_ref.shape[1]

    @pl.when((pl.program_id(0) == 0) & (pl.program_id(1) == 0))
    def _():
        carry[...] = c0_ref[...]

    y = _rms(x_ref[...]) * g_ref[...]
    h = (y * (1.0 + sc_ref[...]) + sh_ref[...]).reshape(rows, d)
    h_hi = h.astype(BF16)
    h_hi32 = h_hi.astype(F32)
    bits = lax.bitcast_convert_type(h_hi32, U32)
    hp_ref[...] = (bits[:, d // 2:] & jnp.uint32(0xFFFF0000)) | (bits[:, :d // 2] >> 16)

    h_lo = (h - h_hi32).astype(BF16)
    logits = jnp.dot(h_hi, wlo_ref[...], preferred_element_type=F32)
    logits = logits + jnp.dot(h_lo, whi_ref[...], preferred_element_type=F32)
    logits = logits + jnp.dot(h_hi, whi_ref[...], preferred_element_type=F32) + br_ref[...]
    col = lax.broadcasted_iota(jnp.int32, logits.shape, 1).astype(F32)
    work = logits
    vals, idxs = [], []
    for _ in range(TOP_K):
        m = jnp.max(work, axis=-1, keepdims=True)
        sel = jnp.min(jnp.where(work == m, col, float(n_exp)), axis=-1, keepdims=True)
        vals.append(m)
        idxs.append(sel)
        work = jnp.where(col == sel, -jnp.inf, work)
    exps = [jnp.exp(v - vals[0]) for v in vals]
    den = exps[0]
    for e in exps[1:]:
        den = den + e
    idx_ref[...] = jnp.concatenate(idxs, axis=1).astype(jnp.int32)
    gate_ref[...] = jnp.concatenate([e / den for e in exps], axis=1)

    onehots = [(col == sel).astype(F32) for sel in idxs]
    cnt = onehots[0]
    for oh in onehots[1:]:
        cnt = cnt + oh
    earlier = (lax.broadcasted_iota(jnp.int32, (rows, rows), 0)
               > lax.broadcasted_iota(jnp.int32, (rows, rows), 1)).astype(BF16)
    before = jnp.dot(earlier, cnt.astype(BF16), preferred_element_type=F32) + carry[...]
    ranks = [jnp.sum(oh * before, axis=-1, keepdims=True) for oh in onehots]
    rank_ref[...] = jnp.concatenate(ranks, axis=1).astype(jnp.int32)
    carry[...] = carry[...] + jnp.sum(cnt, axis=0, keepdims=True)
    cnt_ref[...] = carry[...]


def _router(x1, mod, g, w_hi, w_lo, b_router, counts_before, hp_prev, tok_offset, n_tok_all):
    b, t, d = x1.shape
    n_exp = w_hi.shape[1]
    bb, tt = _token_blocks(b, t, FFN_ROW_TILE)
    rows = bb * tt
    n_tok = b * t
    assert tok_offset % rows == 0
    blk0 = tok_offset // rows
    flat = lambda i, s: (i * (t // tt) + s, 0)
    has_prev = hp_prev is not None
    in_specs = [pl.BlockSpec((bb, tt, d), lambda i, s: (i, s, 0)),
                pl.BlockSpec((bb, 1, d), lambda i, s: (i, 0, 3)),
                pl.BlockSpec((bb, 1, d), lambda i, s: (i, 0, 4)),
                pl.BlockSpec((1, d), lambda i, s: (0, 0)),
                pl.BlockSpec((d, n_exp), lambda i, s: (0, 0)),
                pl.BlockSpec((d, n_exp), lambda i, s: (0, 0)),
                pl.BlockSpec((1, n_exp), lambda i, s: (0, 0)),
                pl.BlockSpec((1, n_exp), lambda i, s: (0, 0))]
    args = [x1, mod, mod, g.reshape(1, d), w_hi, w_lo, b_router.reshape(1, n_exp), counts_before]
    if has_prev:
        in_specs.append(pl.BlockSpec(memory_space=pl.ANY))
        args.append(hp_prev)
    return pl.pallas_call(
        functools.partial(_router_kernel, has_prev=has_prev),
        grid=(b // bb, t // tt),
        in_specs=in_specs,
        out_specs=[pl.BlockSpec((rows, d // 2), lambda i, s: (blk0 + i * (t // tt) + s, 0)),
                   pl.BlockSpec((rows, TOP_K), flat),
                   pl.BlockSpec((rows, TOP_K), flat),
                   pl.BlockSpec((rows, TOP_K), flat),
                   pl.BlockSpec((1, n_exp), lambda i, s: (0, 0))],
        out_shape=[jax.ShapeDtypeStruct((n_tok_all, d // 2), U32),
                   jax.ShapeDtypeStruct((n_tok, TOP_K), jnp.int32),
                   jax.ShapeDtypeStruct((n_tok, TOP_K), F32),
                   jax.ShapeDtypeStruct((n_tok, TOP_K), jnp.int32),
                   jax.ShapeDtypeStruct((1, n_exp), F32)],
        scratch_shapes=[pltpu.VMEM((1, n_exp), F32)],
        input_output_aliases={len(args) - 1: 0} if has_prev else {},
        compiler_params=_params(("arbitrary", "arbitrary")),
        name="router",
    )(*args)


def _stage_indices(idx_batch_ref, idx_smem, sem_idx):
    cp = pltpu.make_async_copy(idx_batch_ref.at[0, 0], idx_smem, sem_idx)
    cp.start()
    cp.wait()


def _issue_row_copies(idx_smem, base, src_ref, dst_ref, sem):
    def body(g, carry):
        for u in range(ROW_COPY_UNROLL):
            r = g * ROW_COPY_UNROLL + u
            pltpu.make_async_copy(src_ref.at[pl.ds(idx_smem[base + r], 1)], dst_ref.at[pl.ds(r, 1)], sem).start(
                priority=u % 2)
        return carry

    lax.fori_loop(0, dst_ref.shape[0] // ROW_COPY_UNROLL, body, 0)


def _issue_tile(tile, first_tile, idx0_ref, idxn_ref, idx_smem, sem_idx, src_ref, dst_ref, sem):
    n_rows = dst_ref.shape[0]
    batch = idx_smem.shape[0] // n_rows
    rel = tile - first_tile

    @pl.when(rel == 0)
    def _():
        _stage_indices(idx0_ref, idx_smem, sem_idx)

    @pl.when((rel > 0) & (lax.rem(rel, batch) == 0))
    def _():
        _stage_indices(idxn_ref, idx_smem, sem_idx)

    _issue_row_copies(idx_smem, lax.rem(rel, batch) * n_rows, src_ref, dst_ref, sem)


def _wait_row_copies(src_ref, dst_ref, sem):
    pltpu.make_async_copy(src_ref.at[pl.ds(0, dst_ref.shape[0])], dst_ref, sem).wait()


def _gather_kernel(nused_ref, idx0_ref, idxn_ref, src_ref, o_ref, idx_smem, buf, sem_idx, sem_rows):
    i = pl.program_id(0)
    n = nused_ref[0]
    half = src_ref.shape[1]

    @pl.when(i == 0)
    def _():
        _issue_tile(jnp.int32(0), 0, idx0_ref, idxn_ref, idx_smem, sem_idx, src_ref, buf.at[0], sem_rows.at[0])

    @pl.when(i + 1 < n)
    def _():
        nxt = (i + 1) % 2
        _issue_tile(i + 1, 0, idx0_ref, idxn_ref, idx_smem, sem_idx, src_ref, buf.at[nxt], sem_rows.at[nxt])

    @pl.when(i < n)
    def _():
        slot = i % 2
        _wait_row_copies(src_ref, buf.at[slot], sem_rows.at[slot])
        w = buf[slot]
        o_ref[:, :half] = _pair_rows(lax.bitcast_convert_type(w << 16, F32).astype(BF16))
        o_ref[:, half:] = _pair_rows(lax.bitcast_convert_type(w & jnp.uint32(0xFFFF0000), F32).astype(BF16))


def _pair_rows(x):
    return pltpu.bitcast(x, U32)


def _gather_rows(src_packed, row_src, n_used, rows):
    p = row_src.shape[0]
    half = src_packed.shape[1]
    nblk = p // rows
    batch = _tile(INDEX_BATCH_TILES, nblk)
    idx_batches = row_src.reshape(nblk // batch, 1, batch * rows)
    clamp = lambda i, nu: jnp.minimum(i, nu[0] - 1)
    return pl.pallas_call(
        _gather_kernel,
        grid_spec=pltpu.PrefetchScalarGridSpec(
            num_scalar_prefetch=1,
            grid=(nblk,),
            in_specs=[pl.BlockSpec((1, 1, batch * rows), lambda i, nu: (0, 0, 0)),
                      pl.BlockSpec((1, 1, batch * rows), lambda i, nu: (clamp(i + 1, nu) // batch, 0, 0)),
                      pl.BlockSpec(memory_space=pl.ANY)],
            out_specs=pl.BlockSpec((rows // 2, 2 * half), lambda i, nu: (clamp(i, nu), 0)),
            scratch_shapes=[pltpu.SMEM((batch * rows,), jnp.int32),
                            pltpu.VMEM((2, rows, half), U32),
                            pltpu.SemaphoreType.DMA(()),
                            pltpu.SemaphoreType.DMA((2,))]),
        out_shape=jax.ShapeDtypeStruct((p // 2, 2 * half), U32),
        compiler_params=_params(("arbitrary",)),
        name="moe_gather",
    )(n_used, idx_batches, idx_batches, src_packed)


def _stream_expert_tiles(first_ref, ntile_ref, ntot_ref, src_hbm, dst_hbm, xbuf, obuf, sem_in, sem_out,
                         compute_first, compute):
    j, e = pl.program_id(0), pl.program_id(1)
    nj, ne = pl.num_programs(0), pl.num_programs(1)
    n_slots = xbuf.shape[0]
    lookahead = n_slots - 2
    in_rows = xbuf.shape[1]
    out_rows = obuf.shape[1]
    tn = obuf.shape[2]
    n_total = ntot_ref[0]
    n_seq = nj * n_total

    def x_copy(t, slot):
        r0 = pl.multiple_of(t * in_rows, in_rows)
        return pltpu.make_async_copy(src_hbm.at[pl.ds(r0, in_rows)], xbuf.at[slot], sem_in.at[slot])

    def x_start(seq):
        x_copy(lax.rem(seq, n_total), lax.rem(seq, n_slots)).start(priority=TILE_COPY_PRIORITY)

    def o_copy(t, slot):
        r0 = pl.multiple_of(t * out_rows, out_rows)
        c0 = pl.multiple_of(j * tn, tn)
        return pltpu.make_async_copy(obuf.at[slot], dst_hbm.at[pl.ds(r0, out_rows), pl.ds(c0, tn)],
                                     sem_out.at[slot])

    @pl.when((j == 0) & (e == 0))
    def _():
        for seq in range(lookahead):
            @pl.when(seq < n_seq)
            def _():
                x_start(jnp.int32(seq))

    def step(t, compute_fn, width):
        seq = j * n_total + t
        in_slots = [lax.rem(seq + k, n_slots) for k in range(width)]
        out_slots = [lax.rem(seq + k, 2) for k in range(width)]
        for k in range(width):
            x_copy(0, in_slots[k]).wait()

        for k in range(width):
            @pl.when(seq + lookahead + k < n_seq)
            def _():
                x_start(seq + lookahead + k)

        for k in range(width):
            @pl.when(seq + k >= 2)
            def _():
                o_copy(t, out_slots[k]).wait()

        for k in range(width):
            obuf[out_slots[k]] = compute_fn(xbuf[in_slots[k]])
        for k in range(width):
            o_copy(t + k, out_slots[k]).start()

    t0 = first_ref[e]
    nt = ntile_ref[e]

    @pl.when(nt > 0)
    def _():
        step(t0, compute_first, 1)

    n_pairs = jnp.maximum(nt - 1, 0) // 2

    def body(p, carry):
        step(t0 + 1 + 2 * p, compute, 2)
        return carry

    lax.fori_loop(0, n_pairs, body, 0)

    @pl.when((nt > 1) & (lax.rem(nt - 1, 2) == 1))
    def _():
        step(t0 + nt - 1, compute, 1)

    @pl.when((j == nj - 1) & (e == ne - 1))
    def _():
        @pl.when(n_seq >= 2)
        def _():
            o_copy(0, lax.rem(n_seq, 2)).wait()

        o_copy(0, lax.rem(n_seq - 1, 2)).wait()


def _swiglu(g, u):
    g = jnp.minimum(g, SWIGLU_LIMIT)
    u = jnp.clip(u, -SWIGLU_LIMIT, SWIGLU_LIMIT)
    return (u + 1.0) * g * jax.nn.sigmoid(SWIGLU_ALPHA * g)


def _moe_up_kernel(first_ref, ntile_ref, eff_ref, ntot_ref, x_hbm, wg_ref, wu_ref, bg_ref, bu_ref, act_hbm,
                   xbuf, obuf, wg_bf, wu_bf, sem_in, sem_out):
    kc = wg_ref.shape[0] // MOE_CAST_CHUNKS

    def compute_first(x_words):
        x = pltpu.bitcast(x_words, BF16)
        g = bg_ref[...]
        u = bu_ref[...]
        for c in range(MOE_CAST_CHUNKS):
            ks = slice(c * kc, (c + 1) * kc)
            wg_c = wg_ref[ks, :].astype(BF16)
            wu_c = wu_ref[ks, :].astype(BF16)
            wg_bf[ks, :] = wg_c
            wu_bf[ks, :] = wu_c
            g = g + jnp.dot(x[:, ks], wg_c, preferred_element_type=F32)
            u = u + jnp.dot(x[:, ks], wu_c, preferred_element_type=F32)
        return _pair_rows(_swiglu(g, u).astype(BF16))

    def compute(x_words):
        x = pltpu.bitcast(x_words, BF16)
        g = jnp.dot(x, wg_bf[...], preferred_element_type=F32) + bg_ref[...]
        u = jnp.dot(x, wu_bf[...], preferred_element_type=F32) + bu_ref[...]
        return _pair_rows(_swiglu(g, u).astype(BF16))

    _stream_expert_tiles(first_ref, ntile_ref, ntot_ref, x_hbm, act_hbm, xbuf, obuf, sem_in, sem_out,
                         compute_first, compute)


def _moe_up(xs, w_gate_up, b_gate_up, tile_first, tile_count, eff_expert, n_tiles):
    p2, d = xs.shape
    n_exp, _, f2 = w_gate_up.shape
    f = f2 // 2
    tf = _tile(MOE_FF_TILE, f)
    nj = f // tf
    return pl.pallas_call(
        _moe_up_kernel,
        grid_spec=pltpu.PrefetchScalarGridSpec(
            num_scalar_prefetch=4,
            grid=(nj, n_exp),
            in_specs=[pl.BlockSpec(memory_space=pl.ANY),
                      pl.BlockSpec((None, d, tf), lambda j, e, fi, nt, ef, ntot: (ef[e], 0, j)),
                      pl.BlockSpec((None, d, tf), lambda j, e, fi, nt, ef, ntot: (ef[e], 0, nj + j)),
                      pl.BlockSpec((None, 1, tf), lambda j, e, fi, nt, ef, ntot: (ef[e], 0, j)),
                      pl.BlockSpec((None, 1, tf), lambda j, e, fi, nt, ef, ntot: (ef[e], 0, nj + j))],
            out_specs=pl.BlockSpec(memory_space=pl.ANY),
            scratch_shapes=[pltpu.VMEM((MOE_IN_SLOTS, MOE_ROWS // 2, d), U32),
                            pltpu.VMEM((2, MOE_ROWS // 2, tf), U32),
                            pltpu.VMEM((d, tf), BF16),
                            pltpu.VMEM((d, tf), BF16),
                            pltpu.SemaphoreType.DMA((MOE_IN_SLOTS,)),
                            pltpu.SemaphoreType.DMA((2,))]),
        out_shape=jax.ShapeDtypeStruct((p2, f), U32),
        compiler_params=_params(("arbitrary", "arbitrary")),
        name="moe_up",
    )(tile_first, tile_count, eff_expert, n_tiles, xs, w_gate_up, w_gate_up,
      b_gate_up.reshape(n_exp, 1, f2), b_gate_up.reshape(n_exp, 1, f2))


def _moe_down_kernel(first_ref, ntile_ref, eff_ref, ntot_ref, a_hbm, w_ref, b_ref, y_hbm,
                     xbuf, obuf, w_bf, sem_in, sem_out):
    kc = w_ref.shape[0] // MOE_CAST_CHUNKS

    def compute_first(a_words):
        a = pltpu.bitcast(a_words, BF16)
        y = b_ref[...]
        for c in range(MOE_CAST_CHUNKS):
            ks = slice(c * kc, (c + 1) * kc)
            w_c = w_ref[ks, :].astype(BF16)
            w_bf[ks, :] = w_c
            y = y + jnp.dot(a[:, ks], w_c, preferred_element_type=F32)
        return y

    def compute(a_words):
        return jnp.dot(pltpu.bitcast(a_words, BF16), w_bf[...], preferred_element_type=F32) + b_ref[...]

    _stream_expert_tiles(first_ref, ntile_ref, ntot_ref, a_hbm, y_hbm, xbuf, obuf, sem_in, sem_out,
                         compute_first, compute)


def _moe_down(act, w_down, b_down, tile_first, tile_count, eff_expert, n_tiles):
    p2, f = act.shape
    n_exp, _, d = w_down.shape
    tn = _tile(MOE_OUT_TILE, d)
    return pl.pallas_call(
        _moe_down_kernel,
        grid_spec=pltpu.PrefetchScalarGridSpec(
            num_scalar_prefetch=4,
            grid=(d // tn, n_exp),
            in_specs=[pl.BlockSpec(memory_space=pl.ANY),
                      pl.BlockSpec((None, f, tn), lambda j, e, fi, nt, ef, ntot: (ef[e], 0, j)),
                      pl.BlockSpec((None, 1, tn), lambda j, e, fi, nt, ef, ntot: (ef[e], 0, j))],
            out_specs=pl.BlockSpec(memory_space=pl.ANY),
            scratch_shapes=[pltpu.VMEM((MOE_IN_SLOTS, MOE_ROWS // 2, f), U32),
                            pltpu.VMEM((2, MOE_ROWS, tn), F32),
                            pltpu.VMEM((f, tn), BF16),
                            pltpu.SemaphoreType.DMA((MOE_IN_SLOTS,)),
                            pltpu.SemaphoreType.DMA((2,))]),
        out_shape=jax.ShapeDtypeStruct((2 * p2, d), F32),
        compiler_params=_params(("arbitrary", "arbitrary")),
        name="moe_down",
    )(tile_first, tile_count, eff_expert, n_tiles, act, w_down, b_down.reshape(n_exp, 1, d))


def _combine_kernel(pos0_ref, posn_ref, y_ref, gate_ref, x_ref, g2_ref, gf_ref, o_ref,
                    pos_smem, buf, sem_idx, sem_rows):
    bb, tt, d = x_ref.shape
    rows = bb * tt
    i = pl.program_id(0)

    @pl.when(i == 0)
    def _():
        _issue_tile(jnp.int32(0), 0, pos0_ref, posn_ref, pos_smem, sem_idx, y_ref, buf.at[0], sem_rows.at[0])

    @pl.when(i + 1 < pl.num_programs(0))
    def _():
        nxt = (i + 1) % 2
        _issue_tile(i + 1, 0, pos0_ref, posn_ref, pos_smem, sem_idx, y_ref, buf.at[nxt], sem_rows.at[nxt])

    slot = i % 2
    _wait_row_copies(y_ref, buf.at[slot], sem_rows.at[slot])
    gates = gate_ref[...]
    ffn = gates[:, 0:1] * buf[slot, 0:rows]
    for k in range(1, TOP_K):
        ffn = ffn + gates[:, k:k + 1] * buf[slot, k * rows:(k + 1) * rows]
    x2 = x_ref[...] + g2_ref[...] * ffn.reshape(bb, tt, d)
    o_ref[...] = _rms(x2) * gf_ref[...]


def _combine(y_sorted, pos, gates, x1, mod, gf, tok_offset):
    b, t, d = x1.shape
    bb, tt = _token_blocks(b, t, COMBINE_ROWS)
    rows = bb * tt
    assert tok_offset % rows == 0
    blk0 = tok_offset // rows
    n_tok = pos.shape[0]
    nblk = (b // bb) * (t // tt)
    ts = t // tt
    pos_tiles = pos.reshape(n_tok // rows, rows, TOP_K).transpose(0, 2, 1)[blk0:blk0 + nblk]
    batch = _tile(INDEX_BATCH_TILES, nblk)
    pos_batches = pos_tiles.reshape(nblk // batch, 1, batch * TOP_K * rows)
    return pl.pallas_call(
        _combine_kernel,
        grid=(nblk,),
        in_specs=[pl.BlockSpec((1, 1, batch * TOP_K * rows), lambda i: (0, 0, 0)),
                  pl.BlockSpec((1, 1, batch * TOP_K * rows), lambda i: (jnp.minimum(i + 1, nblk - 1) // batch, 0, 0)),
                  pl.BlockSpec(memory_space=pl.ANY),
                  pl.BlockSpec((rows, TOP_K), lambda i: (blk0 + i, 0)),
                  pl.BlockSpec((bb, tt, d), lambda i: (i // ts, i % ts, 0)),
                  pl.BlockSpec((bb, 1, d), lambda i: (i // ts, 0, 5)),
                  pl.BlockSpec((1, d), lambda i: (0, 0))],
        out_specs=pl.BlockSpec((bb, tt, d), lambda i: (i // ts, i % ts, 0)),
        out_shape=jax.ShapeDtypeStruct((b, t, d), F32),
        scratch_shapes=[pltpu.SMEM((batch * TOP_K * rows,), jnp.int32),
                        pltpu.VMEM((2, TOP_K * rows, d), F32),
                        pltpu.SemaphoreType.DMA(()),
                        pltpu.SemaphoreType.DMA((2,))],
        compiler_params=_params(("arbitrary",)),
        name="moe_combine",
    )(pos_batches, pos_batches, y_sorted, gates, x1, mod, gf.reshape(1, d))


def _routing(top_idx, rank, counts, rows):
    n_tok = top_idx.shape[0]
    n_exp = counts.shape[0]
    n_assign = n_tok * TOP_K
    tile_count = (counts + rows - 1) // rows
    tile_end = jnp.cumsum(tile_count)
    tile_first = tile_end - tile_count
    pos = (tile_first * rows)[top_idx] + rank
    p = n_assign + n_exp * rows
    token_of = jnp.arange(n_assign, dtype=jnp.int32) // TOP_K
    row_token = jnp.zeros((p,), jnp.int32).at[pos.reshape(-1)].set(
        token_of, unique_indices=True, mode="promise_in_bounds")
    ids = jnp.where(tile_count > 0, jnp.arange(n_exp, dtype=jnp.int32), -1)
    eff = lax.cummax(ids)
    eff = jnp.where(eff < 0, jnp.argmax(tile_count > 0).astype(jnp.int32), eff)
    i32 = lambda a: a.astype(jnp.int32)
    return i32(pos), row_token, i32(tile_first), i32(tile_count), i32(eff), i32(tile_end[-1:])


def kernel(x_prompt, x_sample, cache_k, cache_v, state_conv, c_prompt, c_sample, w_ada, b_ada, norm_mix_g, norm_ffn_g, norm_final_g, w_in, b_in, conv_w, conv_b, sinks, out_norm_conv_g, out_norm_attn_g, w_out, b_out, w_router, b_router, w_gate_up, b_gate_up, w_down, b_down):
    d = x_prompt.shape[2]
    depth, dec_b, window, n_kv, head_dim = cache_k.shape
    assert depth == 1
    conv_ch = conv_w.shape[1]
    n_exp = w_router.shape[1]
    n_heads = sinks.shape[0]
    bp, tp, _ = x_prompt.shape
    bs, ts, _ = x_sample.shape
    col_q = 3 * conv_ch
    kv_w = n_kv * head_dim
    att_w = n_heads * head_dim

    mod = _adaln(jnp.concatenate([c_prompt, c_sample], axis=0), w_ada, b_ada)
    mod_p = mod[:bp].reshape(bp, 1, 6 * d)
    mod_s = mod[bp:].reshape(bs, 1, 6 * d)
    order = _head_order(n_heads, n_kv)
    w_in_bf = w_in.astype(BF16)
    w_in_bf = lax.dynamic_update_slice(
        w_in_bf, _permute_blocks(w_in_bf[:, col_q:col_q + att_w], order, head_dim, 1, 0), (0, col_q))
    b_in_p = _permute_blocks(b_in, order, head_dim, 0, col_q)
    w_out_bf = w_out.astype(BF16)
    w_out_bf = lax.dynamic_update_slice(
        w_out_bf, _permute_blocks(w_out_bf[conv_ch:], order, head_dim, 0, 0), (conv_ch, 0))
    attn_g = _permute_blocks(out_norm_attn_g, order, head_dim, 0, 0)
    wr_hi = w_router.astype(BF16)
    wr_lo = (w_router - wr_hi.astype(F32)).astype(BF16)
    n_p, n_s = bp * tp, bs * ts

    def mixer(x, modx, conv_prev, ck, cv, counts_before, hp_prev, tok_offset):
        b, t, _ = x.shape
        proj = _inproj(x, modx, norm_mix_g, w_in_bf, b_in_p)
        yc, conv_state = _conv_mix(proj, conv_prev, conv_w, conv_b, out_norm_conv_g)
        if ck is None:
            ya, k_rot = _swa_banded(proj, sinks, attn_g, n_kv, head_dim, window, col_q)
            keep = window
        else:
            ya, k_rot = _swa_cached(proj, ck, cv, sinks, attn_g, n_kv, head_dim, col_q)
            keep = t
        k_rows = k_rot[:, t - keep:].reshape(1, b, keep, n_kv, head_dim)
        v_rows = proj[:, t - keep:, col_q + att_w + kv_w:].reshape(1, b, keep, n_kv, head_dim)
        x1 = _outproj(yc, ya, w_out_bf, b_out, x, modx)
        routed = _router(x1, modx, norm_ffn_g, wr_hi, wr_lo, b_router, counts_before, hp_prev,
                         tok_offset, n_p + n_s)
        return x1, routed, conv_state, k_rows, v_rows

    zero_state = jnp.zeros((bp, 2, conv_ch), F32)
    x1p, (hp_buf, idxp, gatep, rankp, cntp), conv_p, k_prompt, v_prompt = mixer(
        x_prompt, mod_p, zero_state, None, None, jnp.zeros((1, n_exp), F32), None, 0)
    x1s, (h2_packed, idxs, gates_s, ranks, cnt), conv_s, k_sample, v_sample = mixer(
        x_sample, mod_s, state_conv[0],
        cache_k[0].reshape(dec_b, window, kv_w), cache_v[0].reshape(dec_b, window, kv_w), cntp, hp_buf, n_p)

    top_idx = jnp.concatenate([idxp, idxs], axis=0)
    gates = jnp.concatenate([gatep, gates_s], axis=0)
    rank = jnp.concatenate([rankp, ranks], axis=0)
    pos, row_token, tile_first, tile_count, eff, n_tiles = _routing(
        top_idx, rank, cnt[0].astype(jnp.int32), MOE_ROWS)
    xs = _gather_rows(h2_packed, row_token, n_tiles, MOE_ROWS)
    act = _moe_up(xs, w_gate_up, b_gate_up, tile_first, tile_count, eff, n_tiles)
    y_sorted = _moe_down(act, w_down, b_down, tile_first, tile_count, eff, n_tiles)
    y_prompt = _combine(y_sorted, pos, gates, x1p, mod_p, norm_final_g, 0)
    y_sample = _combine(y_sorted, pos, gates, x1s, mod_s, norm_final_g, n_p)
    return (y_prompt, y_sample, k_prompt, v_prompt, conv_p[None], k_sample, v_sample, conv_s[None])
```

```python
import functools

import jax
import jax.numpy as jnp
from jax import lax
from jax.experimental import pallas as pl
from jax.experimental.pallas import tpu as pltpu

F32 = jnp.float32
BF16 = jnp.bfloat16
U32 = jnp.uint32

CHUNK = 64
TOP_K = 4
PAST_LEN = 2048
ROPE_THETA = 10000.0
NORM_EPS = 1e-5
SWIGLU_LIMIT = 7.0
SWIGLU_ALPHA = 1.702

LANES = 128
SUBLANES = 8
VMEM_LIMIT_BYTES = 58 * 1024 * 1024

ROW_TILE = 512
ADA_COL_TILE = 512
INPROJ_COL_TILE = 1024
OUTPROJ_COL_TILE = 1024
CONV_ROW_TILE = 256
ATTN_CHUNKS_PER_STEP = 4
FFN_ROW_TILE = 256
MOE_ROWS = 256
MOE_FF_TILE = 512
MOE_OUT_TILE = 1024
MOE_IN_SLOTS = 5
MOE_OUT_SLOTS = 4
MOE_CAST_CHUNKS = 4
COMBINE_ROWS = 128
ROW_COPY_UNROLL = 8
INDEX_BATCH_TILES = 8
TILE_COPY_PRIORITY = 1


def _tile(pref, dim):
    t = min(pref, dim)
    while dim % t:
        t -= 1
    return t


def _params(sem):
    return pltpu.CompilerParams(dimension_semantics=sem, vmem_limit_bytes=VMEM_LIMIT_BYTES)


def _rms(x):
    return x * lax.rsqrt(jnp.mean(x * x, axis=-1, keepdims=True) + NORM_EPS)


def _token_blocks(b, t, rows_pref):
    if t >= rows_pref:
        return 1, _tile(rows_pref, t)
    return _tile(max(rows_pref // t, 1), b), t


def _adaln_kernel(c_ref, w_ref, b_ref, o_ref):
    c = c_ref[...]
    a = (c * jax.nn.sigmoid(c)).astype(BF16)
    o_ref[...] = jnp.dot(a, w_ref[...].astype(BF16), preferred_element_type=F32) + b_ref[...]


def _adaln(c, w_ada, b_ada):
    nb, d = c.shape
    n = w_ada.shape[1]
    tn = _tile(ADA_COL_TILE, n)
    return pl.pallas_call(
        _adaln_kernel,
        grid=(n // tn,),
        in_specs=[pl.BlockSpec((nb, d), lambda j: (0, 0)),
                  pl.BlockSpec((d, tn), lambda j: (0, j)),
                  pl.BlockSpec((1, tn), lambda j: (0, j))],
        out_specs=pl.BlockSpec((nb, tn), lambda j: (0, j)),
        out_shape=jax.ShapeDtypeStruct((nb, n), F32),
        compiler_params=_params(("arbitrary",)),
        name="adaln",
    )(c, w_ada, b_ada.reshape(1, n))


def _inproj_kernel(x_ref, sh_ref, sc_ref, g_ref, w_ref, b_ref, o_ref, h_ref):
    bb, tt, d = x_ref.shape

    @pl.when(pl.program_id(2) == 0)
    def _():
        y = _rms(x_ref[...]) * g_ref[...]
        h = y * (1.0 + sc_ref[...]) + sh_ref[...]
        h_ref[...] = h.reshape(bb * tt, d).astype(BF16)

    acc = jnp.dot(h_ref[...], w_ref[...], preferred_element_type=F32) + b_ref[...]
    o_ref[...] = acc.reshape(o_ref.shape)


def _inproj(x, mod, g, w_bf, b_in):
    b, t, d = x.shape
    n = w_bf.shape[1]
    bb, tt = _token_blocks(b, t, ROW_TILE)
    tn = _tile(INPROJ_COL_TILE, n)
    return pl.pallas_call(
        _inproj_kernel,
        grid=(b // bb, t // tt, n // tn),
        in_specs=[pl.BlockSpec((bb, tt, d), lambda i, s, j: (i, s, 0)),
                  pl.BlockSpec((bb, 1, d), lambda i, s, j: (i, 0, 0)),
                  pl.BlockSpec((bb, 1, d), lambda i, s, j: (i, 0, 1)),
                  pl.BlockSpec((1, d), lambda i, s, j: (0, 0)),
                  pl.BlockSpec((d, tn), lambda i, s, j: (0, j)),
                  pl.BlockSpec((1, tn), lambda i, s, j: (0, j))],
        out_specs=pl.BlockSpec((bb, tt, tn), lambda i, s, j: (i, s, j)),
        out_shape=jax.ShapeDtypeStruct((b, t, n), F32),
        scratch_shapes=[pltpu.VMEM((bb * tt, d), BF16)],
        compiler_params=_params(("arbitrary", "arbitrary", "arbitrary")),
        name="inproj",
    )(x, mod, mod, g.reshape(1, d), w_bf, b_in.reshape(1, n))


def _conv_kernel(bg_ref, cg_ref, xc_ref, cgp_ref, xcp_ref, st_ref, w_ref, b_ref, g_ref, y_ref, so_ref):
    tt = cg_ref.shape[0]
    u = cg_ref[...] * xc_ref[...]
    halo = cgp_ref[...] * xcp_ref[...]
    st = st_ref[...]
    first = pl.program_id(1) == 0
    um2 = jnp.where(first, st[0:1], halo[SUBLANES - 2:SUBLANES - 1])
    um1 = jnp.where(first, st[1:2], halo[SUBLANES - 1:SUBLANES])
    row = lax.broadcasted_iota(jnp.int32, u.shape, 0)
    u1 = jnp.where(row == 0, um1, pltpu.roll(u, 1, 0))
    u2 = jnp.where(row == 0, um2, jnp.where(row == 1, um1, pltpu.roll(u, 2, 0)))
    w = w_ref[...]
    conv = b_ref[...] + u2 * w[0:1]
    conv = conv + u1 * w[1:2]
    conv = conv + u * w[2:3]
    y = _rms(bg_ref[...] * conv) * g_ref[...]
    y_ref[...] = y.astype(BF16)
    so_ref[...] = u[tt - 2:tt]


def _conv_mix(proj, state, conv_w, conv_b, g):
    b, t, _ = proj.shape
    c = conv_w.shape[1]
    assert conv_w.shape[0] == 3
    tt = _tile(CONV_ROW_TILE, t)
    hs = tt // SUBLANES
    halo_map = lambda col: (lambda i, s: (i, jnp.maximum(s * hs - 1, 0), col))
    return pl.pallas_call(
        _conv_kernel,
        grid=(b, t // tt),
        in_specs=[pl.BlockSpec((None, tt, c), lambda i, s: (i, s, 0)),
                  pl.BlockSpec((None, tt, c), lambda i, s: (i, s, 1)),
                  pl.BlockSpec((None, tt, c), lambda i, s: (i, s, 2)),
                  pl.BlockSpec((None, SUBLANES, c), halo_map(1)),
                  pl.BlockSpec((None, SUBLANES, c), halo_map(2)),
                  pl.BlockSpec((None, 2, c), lambda i, s: (i, 0, 0)),
                  pl.BlockSpec((3, c), lambda i, s: (0, 0)),
                  pl.BlockSpec((1, c), lambda i, s: (0, 0)),
                  pl.BlockSpec((1, c), lambda i, s: (0, 0))],
        out_specs=[pl.BlockSpec((None, tt, c), lambda i, s: (i, s, 0)),
                   pl.BlockSpec((None, 2, c), lambda i, s: (i, 0, 0))],
        out_shape=[jax.ShapeDtypeStruct((b, t, c), BF16),
                   jax.ShapeDtypeStruct((b, 2, c), F32)],
        compiler_params=_params(("arbitrary", "arbitrary")),
        name="conv_mix",
    )(proj, proj, proj, proj, proj, state, conv_w, conv_b.reshape(1, c), g.reshape(1, c))


def _head_order(n_heads, n_kv):
    grp = n_heads // n_kv
    assert n_kv % 2 == 0
    return [(2 * p + half) * grp + g for p in range(n_kv // 2) for g in range(grp) for half in (0, 1)]


def _permute_blocks(x, order, width, axis, start):
    n = len(order) * width
    take = lambda lo, hi: lax.slice_in_dim(x, lo, hi, axis=axis)
    parts = [take(0, start)] + [take(start + h * width, start + (h + 1) * width) for h in order]
    parts.append(take(start + n, x.shape[axis]))
    return jnp.concatenate(parts, axis=axis)


def _rope_tables(pos, head_dim):
    half = head_dim // 2
    inv_freq = ROPE_THETA ** (-jnp.arange(half, dtype=F32) / half)
    ang = pos.astype(F32)[:, None] * inv_freq[None, :]
    cos, sin = jnp.cos(ang), jnp.sin(ang)
    reps = LANES // head_dim
    return (jnp.tile(jnp.concatenate([cos, cos], axis=-1), (1, reps)),
            jnp.tile(jnp.concatenate([-sin, sin], axis=-1), (1, reps)))


def _rope(x, cos, sin, head_dim):
    width = x.shape[1]
    half = head_dim // 2
    reps = width // LANES
    cos = jnp.concatenate([cos] * reps, axis=1)
    sin = jnp.concatenate([sin] * reps, axis=1)
    lane = lax.broadcasted_iota(jnp.int32, x.shape, 1)
    swapped = jnp.where(lane % head_dim < half,
                        pltpu.roll(x, width - half, 1), pltpu.roll(x, half, 1))
    return x * cos + swapped * sin


def _pair_blocks(slab):
    keys = slab.shape[0]
    lo = lax.broadcasted_iota(jnp.int32, (CHUNK, LANES), 1) < LANES // 2
    zero = jnp.zeros((CHUNK, LANES), slab.dtype)
    parts = []
    for c in range(keys // CHUNK):
        blk = slab[c * CHUNK:(c + 1) * CHUNK]
        parts += [jnp.where(lo, blk, zero), jnp.where(lo, zero, blk)]
    return jnp.concatenate(parts, axis=0)


def _attend_pair(qs, kblk, vblk, sink_vec, key_ok):
    head_dim = LANES // 2
    n_chunks = kblk.shape[0] // LANES
    lo = lax.broadcasted_iota(jnp.int32, (1, LANES), 1) < head_dim
    s = lax.dot_general(qs.astype(BF16), kblk, (((1,), (1,)), ((), ())),
                        preferred_element_type=F32) * (head_dim ** -0.5)
    tiles = [jnp.where(key_ok[c], s[:, c * LANES:(c + 1) * LANES], -jnp.inf) for c in range(n_chunks)]

    def per_half(t, reduce_fn, fill):
        a = reduce_fn(jnp.where(lo, t, fill), axis=-1, keepdims=True)
        b = reduce_fn(jnp.where(lo, fill, t), axis=-1, keepdims=True)
        return jnp.where(lo, a, b)

    tmax = tiles[0]
    for t in tiles[1:]:
        tmax = jnp.maximum(tmax, t)
    m = jnp.maximum(per_half(tmax, jnp.max, -jnp.inf), sink_vec)
    es = [jnp.exp(t - m) for t in tiles]
    esum = es[0]
    for e in es[1:]:
        esum = esum + e
    den = per_half(esum, jnp.sum, 0.0) + jnp.exp(sink_vec - m)
    e_all = jnp.concatenate([e.astype(BF16) for e in es], axis=1)
    return jnp.dot(e_all, vblk, preferred_element_type=F32) / den


def _attend(q, kslabs, vslabs, sinks_ref, key_ok, grp):
    rows = q.shape[0]
    head_dim = LANES // 2
    lo = lax.broadcasted_iota(jnp.int32, (1, LANES), 1) < head_dim
    outs = []
    for p in range(len(kslabs)):
        qs = jnp.concatenate([q[:, (p * grp + g) * LANES:(p * grp + g + 1) * LANES] for g in range(grp)], axis=0)
        sink_vec = jnp.concatenate(
            [jnp.broadcast_to(jnp.where(lo, sinks_ref[2 * p * grp + g], sinks_ref[(2 * p + 1) * grp + g]),
                              (rows, LANES)) for g in range(grp)], axis=0)
        o = _attend_pair(qs, kslabs[p], vslabs[p], sink_vec, key_ok)
        outs.extend(o[g * rows:(g + 1) * rows] for g in range(grp))
    return jnp.concatenate(outs, axis=1)


def _swa_banded_kernel(sinks_ref, q_ref, k_ref, v_ref, cos_ref, sin_ref, g_ref,
                       y_ref, kr_ref, kbuf, vbuf, *, grp, window):
    rb = q_ref.shape[0]
    head_dim = LANES // 2
    n_pairs = kbuf.shape[1] // LANES
    step = pl.program_id(1)

    @pl.when(step == 0)
    def _():
        kbuf[0:window] = jnp.zeros((window, kbuf.shape[1]), BF16)
        vbuf[0:window] = jnp.zeros((window, vbuf.shape[1]), BF16)

    @pl.when(step > 0)
    def _():
        kbuf[0:window] = kbuf[rb:rb + window]
        vbuf[0:window] = vbuf[rb:rb + window]

    cos, sin = cos_ref[...], sin_ref[...]
    k = _rope(k_ref[...], cos, sin, head_dim)
    kr_ref[...] = k
    kbuf[window:window + rb] = k.astype(BF16)
    vbuf[window:window + rb] = v_ref[...].astype(BF16)
    q = _rope(q_ref[...], cos, sin, head_dim)

    n_buf_chunks = (window + rb) // CHUNK
    kblocks = [[_pair_blocks(kbuf[c * CHUNK:(c + 1) * CHUNK, p * LANES:(p + 1) * LANES])
                for c in range(n_buf_chunks)] for p in range(n_pairs)]
    vblocks = [[_pair_blocks(vbuf[c * CHUNK:(c + 1) * CHUNK, p * LANES:(p + 1) * LANES])
                for c in range(n_buf_chunks)] for p in range(n_pairs)]
    band_chunks = window // CHUNK + 1
    lane_key = lax.broadcasted_iota(jnp.int32, (1, LANES), 1) % CHUNK
    for c in range(rb // CHUNK):
        lo_row = c * CHUNK
        key_ok = [step * rb + (c + b) * CHUNK + lane_key >= window for b in range(band_chunks)]
        ks = [jnp.concatenate(kblocks[p][c:c + band_chunks], axis=0) for p in range(n_pairs)]
        vs = [jnp.concatenate(vblocks[p][c:c + band_chunks], axis=0) for p in range(n_pairs)]
        y = _attend(q[lo_row:lo_row + CHUNK], ks, vs, sinks_ref, key_ok, grp)
        y_ref[lo_row:lo_row + CHUNK] = (_rms(y) * g_ref[...]).astype(BF16)


def _swa_banded(proj, sinks, g, n_kv, head_dim, window, col_q):
    b, t, _ = proj.shape
    assert 2 * head_dim == LANES and CHUNK == head_dim
    n_heads = sinks.shape[0]
    att_w = n_heads * head_dim
    kv_w = n_kv * head_dim
    rb = _tile(ATTN_CHUNKS_PER_STEP * CHUNK, t)
    assert window % CHUNK == 0 and t % CHUNK == 0 and rb >= window
    cos, sin = _rope_tables(jnp.arange(t, dtype=jnp.int32), head_dim)
    col_k = (col_q + att_w) // kv_w
    kern = functools.partial(_swa_banded_kernel, grp=n_heads // n_kv, window=window)
    return pl.pallas_call(
        kern,
        grid=(b, t // rb),
        in_specs=[pl.BlockSpec(memory_space=pltpu.SMEM),
                  pl.BlockSpec((None, rb, att_w), lambda i, s: (i, s, col_q // att_w)),
                  pl.BlockSpec((None, rb, kv_w), lambda i, s: (i, s, col_k)),
                  pl.BlockSpec((None, rb, kv_w), lambda i, s: (i, s, col_k + 1)),
                  pl.BlockSpec((rb, LANES), lambda i, s: (s, 0)),
                  pl.BlockSpec((rb, LANES), lambda i, s: (s, 0)),
                  pl.BlockSpec((1, att_w), lambda i, s: (0, 0))],
        out_specs=[pl.BlockSpec((None, rb, att_w), lambda i, s: (i, s, 0)),
                   pl.BlockSpec((None, rb, kv_w), lambda i, s: (i, s, 0))],
        out_shape=[jax.ShapeDtypeStruct((b, t, att_w), BF16),
                   jax.ShapeDtypeStruct((b, t, kv_w), F32)],
        scratch_shapes=[pltpu.VMEM((window + rb, kv_w), BF16),
                        pltpu.VMEM((window + rb, kv_w), BF16)],
        compiler_params=_params(("arbitrary", "arbitrary")),
        name="swa_banded",
    )(sinks, proj, proj, proj, cos, sin, g.reshape(1, att_w))


def _swa_cached_kernel(sinks_ref, q_ref, k_ref, v_ref, ck_ref, cv_ref, cos_ref, sin_ref, g_ref,
                       y_ref, kr_ref, *, grp):
    head_dim = LANES // 2
    t, kv_w = k_ref.shape
    window = ck_ref.shape[0]
    n_pairs = kv_w // LANES
    cos, sin = cos_ref[...], sin_ref[...]
    k = _rope(k_ref[...], cos, sin, head_dim)
    kr_ref[...] = k
    q = _rope(q_ref[...], cos, sin, head_dim)
    n_keys = window + t
    pad = -n_keys % CHUNK
    tail = [jnp.zeros((pad, kv_w), BF16)] if pad else []
    kb = jnp.concatenate([ck_ref[...].astype(BF16), k.astype(BF16)] + tail, axis=0)
    vb = jnp.concatenate([cv_ref[...].astype(BF16), v_ref[...].astype(BF16)] + tail, axis=0)
    ks = [_pair_blocks(kb[:, p * LANES:(p + 1) * LANES]) for p in range(n_pairs)]
    vs = [_pair_blocks(vb[:, p * LANES:(p + 1) * LANES]) for p in range(n_pairs)]
    lane_key = lax.broadcasted_iota(jnp.int32, (1, LANES), 1) % CHUNK
    key_ok = [c * CHUNK + lane_key < n_keys for c in range((n_keys + pad) // CHUNK)]
    y = _attend(q, ks, vs, sinks_ref, key_ok, grp)
    y_ref[...] = (_rms(y) * g_ref[...]).astype(BF16)


def _swa_cached(proj, cache_k, cache_v, sinks, g, n_kv, head_dim, col_q):
    b, t, _ = proj.shape
    assert 2 * head_dim == LANES and CHUNK == head_dim
    window = cache_k.shape[1]
    n_heads = sinks.shape[0]
    att_w = n_heads * head_dim
    kv_w = n_kv * head_dim
    cos, sin = _rope_tables(PAST_LEN + jnp.arange(t, dtype=jnp.int32), head_dim)
    col_k = (col_q + att_w) // kv_w
    kern = functools.partial(_swa_cached_kernel, grp=n_heads // n_kv)
    return pl.pallas_call(
        kern,
        grid=(b,),
        in_specs=[pl.BlockSpec(memory_space=pltpu.SMEM),
                  pl.BlockSpec((None, t, att_w), lambda i: (i, 0, col_q // att_w)),
                  pl.BlockSpec((None, t, kv_w), lambda i: (i, 0, col_k)),
                  pl.BlockSpec((None, t, kv_w), lambda i: (i, 0, col_k + 1)),
                  pl.BlockSpec((None, window, kv_w), lambda i: (i, 0, 0)),
                  pl.BlockSpec((None, window, kv_w), lambda i: (i, 0, 0)),
                  pl.BlockSpec((t, LANES), lambda i: (0, 0)),
                  pl.BlockSpec((t, LANES), lambda i: (0, 0)),
                  pl.BlockSpec((1, att_w), lambda i: (0, 0))],
        out_specs=[pl.BlockSpec((None, t, att_w), lambda i: (i, 0, 0)),
                   pl.BlockSpec((None, t, kv_w), lambda i: (i, 0, 0))],
        out_shape=[jax.ShapeDtypeStruct((b, t, att_w), BF16),
                   jax.ShapeDtypeStruct((b, t, kv_w), F32)],
        compiler_params=_params(("arbitrary",)),
        name="swa_cached",
    )(sinks, proj, proj, proj, cache_k, cache_v, cos, sin, g.reshape(1, att_w))


def _outproj_kernel(yc_ref, ya_ref, wt_ref, wb_ref, b_ref, x_ref, g1_ref, o_ref):
    bb, tt, c = yc_ref.shape
    acc = jnp.dot(yc_ref[...].reshape(bb * tt, c), wt_ref[...], preferred_element_type=F32)
    acc = acc + jnp.dot(ya_ref[...].reshape(bb * tt, c), wb_ref[...], preferred_element_type=F32)
    acc = acc + b_ref[...]
    o_ref[...] = x_ref[...] + g1_ref[...] * acc.reshape(o_ref.shape)


def _outproj(yc, ya, w_bf, b_out, x, mod):
    b, t, d = x.shape
    c = yc.shape[2]
    assert 2 * c == w_bf.shape[0]
    bb, tt = _token_blocks(b, t, ROW_TILE)
    tn = _tile(OUTPROJ_COL_TILE, d)
    return pl.pallas_call(
        _outproj_kernel,
        grid=(b // bb, t // tt, d // tn),
        in_specs=[pl.BlockSpec((bb, tt, c), lambda i, s, j: (i, s, 0)),
                  pl.BlockSpec((bb, tt, c), lambda i, s, j: (i, s, 0)),
                  pl.BlockSpec((c, tn), lambda i, s, j: (0, j)),
                  pl.BlockSpec((c, tn), lambda i, s, j: (1, j)),
                  pl.BlockSpec((1, tn), lambda i, s, j: (0, j)),
                  pl.BlockSpec((bb, tt, tn), lambda i, s, j: (i, s, j)),
                  pl.BlockSpec((bb, 1, tn), lambda i, s, j: (i, 0, 2 * (d // tn) + j))],
        out_specs=pl.BlockSpec((bb, tt, tn), lambda i, s, j: (i, s, j)),
        out_shape=jax.ShapeDtypeStruct((b, t, d), F32),
        compiler_params=_params(("arbitrary", "arbitrary", "arbitrary")),
        name="outproj",
    )(yc, ya, w_bf, w_bf, b_out.reshape(1, d), x, mod)


def _router_kernel(x_ref, sh_ref, sc_ref, g_ref, whi_ref, wlo_ref, br_ref, c0_ref, *rest, has_prev):
    hp_ref, idx_ref, gate_ref, rank_ref, cnt_ref, carry = rest[1:] if has_prev else rest
    bb, tt, d = x_ref.shape
    rows = bb * tt
    n_exp = whi_ref.shape[1]

    @pl.when((pl.program_id(0) == 0) & (pl.program_id(1) == 0))
    def _():
        carry[...] = c0_ref[...]

    y = _rms(x_ref[...]) * g_ref[...]
    h = (y * (1.0 + sc_ref[...]) + sh_ref[...]).reshape(rows, d)
    h_hi = h.astype(BF16)
    h_hi32 = h_hi.astype(F32)
    bits = lax.bitcast_convert_type(h_hi32, U32)
    hp_ref[...] = (bits[:, d // 2:] & jnp.uint32(0xFFFF0000)) | (bits[:, :d // 2] >> 16)

    h_lo = (h - h_hi32).astype(BF16)
    logits = jnp.dot(h_hi, wlo_ref[...], preferred_element_type=F32)
    logits = logits + jnp.dot(h_lo, whi_ref[...], preferred_element_type=F32)
    logits = logits + jnp.dot(h_hi, whi_ref[...], preferred_element_type=F32) + br_ref[...]
    col = lax.broadcasted_iota(jnp.int32, logits.shape, 1).astype(F32)
    work = logits
    vals, idxs = [], []
    for _ in range(TOP_K):
        m = jnp.max(work, axis=-1, keepdims=True)
        sel = jnp.min(jnp.where(work == m, col, float(n_exp)), axis=-1, keepdims=True)
        vals.append(m)
        idxs.append(sel)
        work = jnp.where(col == sel, -jnp.inf, work)
    exps = [jnp.exp(v - vals[0]) for v in vals]
    den = exps[0]
    for e in exps[1:]:
        den = den + e
    idx_ref[...] = jnp.concatenate(idxs, axis=1).astype(jnp.int32)
    gate_ref[...] = jnp.concatenate([e / den for e in exps], axis=1)

    onehots = [(col == sel).astype(F32) for sel in idxs]
    cnt = onehots[0]
    for oh in onehots[1:]:
        cnt = cnt + oh
    earlier = (lax.broadcasted_iota(jnp.int32, (rows, rows), 0)
               > lax.broadcasted_iota(jnp.int32, (rows, rows), 1)).astype(BF16)
    before = jnp.dot(earlier, cnt.astype(BF16), preferred_element_type=F32) + carry[...]
    ranks = [jnp.sum(oh * before, axis=-1, keepdims=True) for oh in onehots]
    rank_ref[...] = jnp.concatenate(ranks, axis=1).astype(jnp.int32)
    carry[...] = carry[...] + jnp.sum(cnt, axis=0, keepdims=True)
    cnt_ref[...] = carry[...]


def _router(x1, mod, g, w_hi, w_lo, b_router, counts_before, hp_prev, tok_offset, n_tok_all):
    b, t, d = x1.shape
    n_exp = w_hi.shape[1]
    bb, tt = _token_blocks(b, t, FFN_ROW_TILE)
    rows = bb * tt
    n_tok = b * t
    assert tok_offset % rows == 0
    blk0 = tok_offset // rows
    flat = lambda i, s: (i * (t // tt) + s, 0)
    has_prev = hp_prev is not None
    in_specs = [pl.BlockSpec((bb, tt, d), lambda i, s: (i, s, 0)),
                pl.BlockSpec((bb, 1, d), lambda i, s: (i, 0, 3)),
                pl.BlockSpec((bb, 1, d), lambda i, s: (i, 0, 4)),
                pl.BlockSpec((1, d), lambda i, s: (0, 0)),
                pl.BlockSpec((d, n_exp), lambda i, s: (0, 0)),
                pl.BlockSpec((d, n_exp), lambda i, s: (0, 0)),
                pl.BlockSpec((1, n_exp), lambda i, s: (0, 0)),
                pl.BlockSpec((1, n_exp), lambda i, s: (0, 0))]
    args = [x1, mod, mod, g.reshape(1, d), w_hi, w_lo, b_router.reshape(1, n_exp), counts_before]
    if has_prev:
        in_specs.append(pl.BlockSpec(memory_space=pl.ANY))
        args.append(hp_prev)
    return pl.pallas_call(
        functools.partial(_router_kernel, has_prev=has_prev),
        grid=(b // bb, t // tt),
        in_specs=in_specs,
        out_specs=[pl.BlockSpec((rows, d // 2), lambda i, s: (blk0 + i * (t // tt) + s, 0)),
                   pl.BlockSpec((rows, TOP_K), flat),
                   pl.BlockSpec((rows, TOP_K), flat),
                   pl.BlockSpec((rows, TOP_K), flat),
                   pl.BlockSpec((1, n_exp), lambda i, s: (0, 0))],
        out_shape=[jax.ShapeDtypeStruct((n_tok_all, d // 2), U32),
                   jax.ShapeDtypeStruct((n_tok, TOP_K), jnp.int32),
                   jax.ShapeDtypeStruct((n_tok, TOP_K), F32),
                   jax.ShapeDtypeStruct((n_tok, TOP_K), jnp.int32),
                   jax.ShapeDtypeStruct((1, n_exp), F32)],
        scratch_shapes=[pltpu.VMEM((1, n_exp), F32)],
        input_output_aliases={len(args) - 1: 0} if has_prev else {},
        compiler_params=_params(("arbitrary", "arbitrary")),
        name="router",
    )(*args)


def _stage_indices(idx_batch_ref, idx_smem, sem_idx):
    cp = pltpu.make_async_copy(idx_batch_ref.at[0, 0], idx_smem, sem_idx)
    cp.start()
    cp.wait()


def _issue_row_copies(idx_smem, base, src_ref, dst_ref, sem):
    def body(g, carry):
        for u in range(ROW_COPY_UNROLL):
            r = g * ROW_COPY_UNROLL + u
            pltpu.make_async_copy(src_ref.at[pl.ds(idx_smem[base + r], 1)], dst_ref.at[pl.ds(r, 1)], sem).start(
                priority=u % 2)
        return carry

    lax.fori_loop(0, dst_ref.shape[0] // ROW_COPY_UNROLL, body, 0)


def _issue_tile(tile, first_tile, idx0_ref, idxn_ref, idx_smem, sem_idx, src_ref, dst_ref, sem):
    n_rows = dst_ref.shape[0]
    batch = idx_smem.shape[0] // n_rows
    rel = tile - first_tile

    @pl.when(rel == 0)
    def _():
        _stage_indices(idx0_ref, idx_smem, sem_idx)

    @pl.when((rel > 0) & (lax.rem(rel, batch) == 0))
    def _():
        _stage_indices(idxn_ref, idx_smem, sem_idx)

    _issue_row_copies(idx_smem, lax.rem(rel, batch) * n_rows, src_ref, dst_ref, sem)


def _wait_row_copies(src_ref, dst_ref, sem):
    pltpu.make_async_copy(src_ref.at[pl.ds(0, dst_ref.shape[0])], dst_ref, sem).wait()


def _gather_kernel(nused_ref, idx0_ref, idxn_ref, src_ref, o_ref, idx_smem, buf, sem_idx, sem_rows):
    i = pl.program_id(0)
    n = nused_ref[0]
    half = src_ref.shape[1]

    @pl.when(i == 0)
    def _():
        _issue_tile(jnp.int32(0), 0, idx0_ref, idxn_ref, idx_smem, sem_idx, src_ref, buf.at[0], sem_rows.at[0])

    @pl.when(i + 1 < n)
    def _():
        nxt = (i + 1) % 2
        _issue_tile(i + 1, 0, idx0_ref, idxn_ref, idx_smem, sem_idx, src_ref, buf.at[nxt], sem_rows.at[nxt])

    @pl.when(i < n)
    def _():
        slot = i % 2
        _wait_row_copies(src_ref, buf.at[slot], sem_rows.at[slot])
        w = buf[slot]
        o_ref[:, :half] = _pair_rows(lax.bitcast_convert_type(w << 16, F32).astype(BF16))
        o_ref[:, half:] = _pair_rows(lax.bitcast_convert_type(w & jnp.uint32(0xFFFF0000), F32).astype(BF16))


def _pair_rows(x):
    return pltpu.bitcast(x, U32)


def _gather_rows(src_packed, row_src, n_used, rows):
    p = row_src.shape[0]
    half = src_packed.shape[1]
    nblk = p // rows
    batch = _tile(INDEX_BATCH_TILES, nblk)
    idx_batches = row_src.reshape(nblk // batch, 1, batch * rows)
    clamp = lambda i, nu: jnp.minimum(i, nu[0] - 1)
    return pl.pallas_call(
        _gather_kernel,
        grid_spec=pltpu.PrefetchScalarGridSpec(
            num_scalar_prefetch=1,
            grid=(nblk,),
            in_specs=[pl.BlockSpec((1, 1, batch * rows), lambda i, nu: (0, 0, 0)),
                      pl.BlockSpec((1, 1, batch * rows), lambda i, nu: (clamp(i + 1, nu) // batch, 0, 0)),
                      pl.BlockSpec(memory_space=pl.ANY)],
            out_specs=pl.BlockSpec((rows // 2, 2 * half), lambda i, nu: (clamp(i, nu), 0)),
            scratch_shapes=[pltpu.SMEM((batch * rows,), jnp.int32),
                            pltpu.VMEM((2, rows, half), U32),
                            pltpu.SemaphoreType.DMA(()),
                            pltpu.SemaphoreType.DMA((2,))]),
        out_shape=jax.ShapeDtypeStruct((p // 2, 2 * half), U32),
        compiler_params=_params(("arbitrary",)),
        name="moe_gather",
    )(n_used, idx_batches, idx_batches, src_packed)


def _stream_expert_tiles(first_ref, ntile_ref, ntot_ref, src_hbm, dst_hbm, xbuf, obuf, sem_in, sem_out,
                         compute_first, compute):
    j, e = pl.program_id(0), pl.program_id(1)
    nj, ne = pl.num_programs(0), pl.num_programs(1)
    n_slots = xbuf.shape[0]
    lookahead = n_slots - 2
    in_rows = xbuf.shape[1]
    n_out = obuf.shape[0]
    out_rows = obuf.shape[1]
    tn = obuf.shape[2]
    n_total = ntot_ref[0]
    n_seq = nj * n_total

    def x_copy(t, slot):
        r0 = pl.multiple_of(t * in_rows, in_rows)
        return pltpu.make_async_copy(src_hbm.at[pl.ds(r0, in_rows)], xbuf.at[slot], sem_in.at[slot])

    def x_start(seq):
        x_copy(lax.rem(seq, n_total), lax.rem(seq, n_slots)).start(priority=TILE_COPY_PRIORITY)

    def o_copy(t, slot):
        r0 = pl.multiple_of(t * out_rows, out_rows)
        c0 = pl.multiple_of(j * tn, tn)
        return pltpu.make_async_copy(obuf.at[slot], dst_hbm.at[pl.ds(r0, out_rows), pl.ds(c0, tn)],
                                     sem_out.at[slot])

    @pl.when((j == 0) & (e == 0))
    def _():
        for seq in range(lookahead):
            @pl.when(seq < n_seq)
            def _():
                x_start(jnp.int32(seq))

    def step(t, compute_fn, width):
        seq = j * n_total + t
        in_slots = [lax.rem(seq + k, n_slots) for k in range(width)]
        out_slots = [lax.rem(seq + k, n_out) for k in range(width)]
        for k in range(width):
            x_copy(0, in_slots[k]).wait()

        for k in range(width):
            @pl.when(seq + lookahead + k < n_seq)
            def _():
                x_start(seq + lookahead + k)

        for k in range(width):
            @pl.when(seq + k >= n_out)
            def _():
                o_copy(t, out_slots[k]).wait()

        for k in range(width):
            obuf[out_slots[k]] = compute_fn(xbuf[in_slots[k]])
        for k in range(width):
            o_copy(t + k, out_slots[k]).start()

    t0 = first_ref[e]
    nt = ntile_ref[e]

    @pl.when(nt > 0)
    def _():
        step(t0, compute_first, 1)

    n_pairs = jnp.maximum(nt - 1, 0) // 2

    def body(p, carry):
        step(t0 + 1 + 2 * p, compute, 2)
        return carry

    lax.fori_loop(0, n_pairs, body, 0)

    @pl.when((nt > 1) & (lax.rem(nt - 1, 2) == 1))
    def _():
        step(t0 + nt - 1, compute, 1)

    @pl.when((j == nj - 1) & (e == ne - 1))
    def _():
        for k in range(n_out):
            @pl.when(n_seq > k)
            def _():
                o_copy(0, lax.rem(n_seq - 1 - k, n_out)).wait()


def _swiglu(g, u):
    g = jnp.minimum(g, SWIGLU_LIMIT)
    u = jnp.clip(u, -SWIGLU_LIMIT, SWIGLU_LIMIT)
    return (u + 1.0) * g * jax.nn.sigmoid(SWIGLU_ALPHA * g)


def _moe_up_kernel(first_ref, ntile_ref, eff_ref, ntot_ref, x_hbm, wg_ref, wu_ref, bg_ref, bu_ref, act_hbm,
                   xbuf, obuf, wg_bf, wu_bf, sem_in, sem_out):
    kc = wg_ref.shape[0] // MOE_CAST_CHUNKS

    def compute_first(x_words):
        x = pltpu.bitcast(x_words, BF16)
        g = bg_ref[...]
        u = bu_ref[...]
        for c in range(MOE_CAST_CHUNKS):
            ks = slice(c * kc, (c + 1) * kc)
            wg_c = wg_ref[ks, :].astype(BF16)
            wu_c = wu_ref[ks, :].astype(BF16)
            wg_bf[ks, :] = wg_c
            wu_bf[ks, :] = wu_c
            g = g + jnp.dot(x[:, ks], wg_c, preferred_element_type=F32)
            u = u + jnp.dot(x[:, ks], wu_c, preferred_element_type=F32)
        return _pair_rows(_swiglu(g, u).astype(BF16))

    def compute(x_words):
        x = pltpu.bitcast(x_words, BF16)
        g = jnp.dot(x, wg_bf[...], preferred_element_type=F32) + bg_ref[...]
        u = jnp.dot(x, wu_bf[...], preferred_element_type=F32) + bu_ref[...]
        return _pair_rows(_swiglu(g, u).astype(BF16))

    _stream_expert_tiles(first_ref, ntile_ref, ntot_ref, x_hbm, act_hbm, xbuf, obuf, sem_in, sem_out,
                         compute_first, compute)


def _moe_up(xs, w_gate_up, b_gate_up, tile_first, tile_count, eff_expert, n_tiles):
    p2, d = xs.shape
    n_exp, _, f2 = w_gate_up.shape
    f = f2 // 2
    tf = _tile(MOE_FF_TILE, f)
    nj = f // tf
    return pl.pallas_call(
        _moe_up_kernel,
        grid_spec=pltpu.PrefetchScalarGridSpec(
            num_scalar_prefetch=4,
            grid=(nj, n_exp),
            in_specs=[pl.BlockSpec(memory_space=pl.ANY),
                      pl.BlockSpec((None, d, tf), lambda j, e, fi, nt, ef, ntot: (ef[e], 0, j)),
                      pl.BlockSpec((None, d, tf), lambda j, e, fi, nt, ef, ntot: (ef[e], 0, nj + j)),
                      pl.BlockSpec((None, 1, tf), lambda j, e, fi, nt, ef, ntot: (ef[e], 0, j)),
                      pl.BlockSpec((None, 1, tf), lambda j, e, fi, nt, ef, ntot: (ef[e], 0, nj + j))],
            out_specs=pl.BlockSpec(memory_space=pl.ANY),
            scratch_shapes=[pltpu.VMEM((MOE_IN_SLOTS, MOE_ROWS // 2, d), U32),
                            pltpu.VMEM((MOE_OUT_SLOTS, MOE_ROWS // 2, tf), U32),
                            pltpu.VMEM((d, tf), BF16),
                            pltpu.VMEM((d, tf), BF16),
                            pltpu.SemaphoreType.DMA((MOE_IN_SLOTS,)),
                            pltpu.SemaphoreType.DMA((MOE_OUT_SLOTS,))]),
        out_shape=jax.ShapeDtypeStruct((p2, f), U32),
        compiler_params=_params(("arbitrary", "arbitrary")),
        name="moe_up",
    )(tile_first, tile_count, eff_expert, n_tiles, xs, w_gate_up, w_gate_up,
      b_gate_up.reshape(n_exp, 1, f2), b_gate_up.reshape(n_exp, 1, f2))


def _moe_down_kernel(first_ref, ntile_ref, eff_ref, ntot_ref, a_hbm, w_ref, b_ref, y_hbm,
                     xbuf, obuf, w_bf, sem_in, sem_out):
    kc = w_ref.shape[0] // MOE_CAST_CHUNKS

    def compute_first(a_words):
        a = pltpu.bitcast(a_words, BF16)
        y = b_ref[...]
        for c in range(MOE_CAST_CHUNKS):
            ks = slice(c * kc, (c + 1) * kc)
            w_c = w_ref[ks, :].astype(BF16)
            w_bf[ks, :] = w_c
            y = y + jnp.dot(a[:, ks], w_c, preferred_element_type=F32)
        return y

    def compute(a_words):
        return jnp.dot(pltpu.bitcast(a_words, BF16), w_bf[...], preferred_element_type=F32) + b_ref[...]

    _stream_expert_tiles(first_ref, ntile_ref, ntot_ref, a_hbm, y_hbm, xbuf, obuf, sem_in, sem_out,
                         compute_first, compute)


def _moe_down(act, w_down, b_down, tile_first, tile_count, eff_expert, n_tiles):
    p2, f = act.shape
    n_exp, _, d = w_down.shape
    tn = _tile(MOE_OUT_TILE, d)
    return pl.pallas_call(
        _moe_down_kernel,
        grid_spec=pltpu.PrefetchScalarGridSpec(
            num_scalar_prefetch=4,
            grid=(d // tn, n_exp),
            in_specs=[pl.BlockSpec(memory_space=pl.ANY),
                      pl.BlockSpec((None, f, tn), lambda j, e, fi, nt, ef, ntot: (ef[e], 0, j)),
                      pl.BlockSpec((None, 1, tn), lambda j, e, fi, nt, ef, ntot: (ef[e], 0, j))],
            out_specs=pl.BlockSpec(memory_space=pl.ANY),
            scratch_shapes=[pltpu.VMEM((MOE_IN_SLOTS, MOE_ROWS // 2, f), U32),
                            pltpu.VMEM((MOE_OUT_SLOTS, MOE_ROWS, tn), F32),
                            pltpu.VMEM((f, tn), BF16),
                            pltpu.SemaphoreType.DMA((MOE_IN_SLOTS,)),
                            pltpu.SemaphoreType.DMA((MOE_OUT_SLOTS,))]),
        out_shape=jax.ShapeDtypeStruct((2 * p2, d), F32),
        compiler_params=_params(("arbitrary", "arbitrary")),
        name="moe_down",
    )(tile_first, tile_count, eff_expert, n_tiles, act, w_down, b_down.reshape(n_exp, 1, d))


def _combine_kernel(pos0_ref, posn_ref, y_ref, gate_ref, x_ref, g2_ref, gf_ref, o_ref,
                    pos_smem, buf, sem_idx, sem_rows):
    bb, tt, d = x_ref.shape
    rows = bb * tt
    i = pl.program_id(0)

    @pl.when(i == 0)
    def _():
        _issue_tile(jnp.int32(0), 0, pos0_ref, posn_ref, pos_smem, sem_idx, y_ref, buf.at[0], sem_rows.at[0])

    @pl.when(i + 1 < pl.num_programs(0))
    def _():
        nxt = (i + 1) % 2
        _issue_tile(i + 1, 0, pos0_ref, posn_ref, pos_smem, sem_idx, y_ref, buf.at[nxt], sem_rows.at[nxt])

    slot = i % 2
    _wait_row_copies(y_ref, buf.at[slot], sem_rows.at[slot])
    gates = gate_ref[...]
    ffn = gates[:, 0:1] * buf[slot, 0:rows]
    for k in range(1, TOP_K):
        ffn = ffn + gates[:, k:k + 1] * buf[slot, k * rows:(k + 1) * rows]
    x2 = x_ref[...] + g2_ref[...] * ffn.reshape(bb, tt, d)
    o_ref[...] = _rms(x2) * gf_ref[...]


def _combine(y_sorted, pos, gates, x1, mod, gf, tok_offset):
    b, t, d = x1.shape
    bb, tt = _token_blocks(b, t, COMBINE_ROWS)
    rows = bb * tt
    assert tok_offset % rows == 0
    blk0 = tok_offset // rows
    n_tok = pos.shape[0]
    nblk = (b // bb) * (t // tt)
    ts = t // tt
    pos_tiles = pos.reshape(n_tok // rows, rows, TOP_K).transpose(0, 2, 1)[blk0:blk0 + nblk]
    batch = _tile(INDEX_BATCH_TILES, nblk)
    pos_batches = pos_tiles.reshape(nblk // batch, 1, batch * TOP_K * rows)
    return pl.pallas_call(
        _combine_kernel,
        grid=(nblk,),
        in_specs=[pl.BlockSpec((1, 1, batch * TOP_K * rows), lambda i: (0, 0, 0)),
                  pl.BlockSpec((1, 1, batch * TOP_K * rows), lambda i: (jnp.minimum(i + 1, nblk - 1) // batch, 0, 0)),
                  pl.BlockSpec(memory_space=pl.ANY),
                  pl.BlockSpec((rows, TOP_K), lambda i: (blk0 + i, 0)),
                  pl.BlockSpec((bb, tt, d), lambda i: (i // ts, i % ts, 0)),
                  pl.BlockSpec((bb, 1, d), lambda i: (i // ts, 0, 5)),
                  pl.BlockSpec((1, d), lambda i: (0, 0))],
        out_specs=pl.BlockSpec((bb, tt, d), lambda i: (i // ts, i % ts, 0)),
        out_shape=jax.ShapeDtypeStruct((b, t, d), F32),
        scratch_shapes=[pltpu.SMEM((batch * TOP_K * rows,), jnp.int32),
                        pltpu.VMEM((2, TOP_K * rows, d), F32),
                        pltpu.SemaphoreType.DMA(()),
                        pltpu.SemaphoreType.DMA((2,))],
        compiler_params=_params(("arbitrary",)),
        name="moe_combine",
    )(pos_batches, pos_batches, y_sorted, gates, x1, mod, gf.reshape(1, d))


def _routing(top_idx, rank, counts, rows):
    n_tok = top_idx.shape[0]
    n_exp = counts.shape[0]
    n_assign = n_tok * TOP_K
    tile_count = (counts + rows - 1) // rows
    tile_end = jnp.cumsum(tile_count)
    tile_first = tile_end - tile_count
    pos = (tile_first * rows)[top_idx] + rank
    p = n_assign + n_exp * rows
    token_of = jnp.arange(n_assign, dtype=jnp.int32) // TOP_K
    row_token = jnp.zeros((p,), jnp.int32).at[pos.reshape(-1)].set(
        token_of, unique_indices=True, mode="promise_in_bounds")
    ids = jnp.where(tile_count > 0, jnp.arange(n_exp, dtype=jnp.int32), -1)
    eff = lax.cummax(ids)
    eff = jnp.where(eff < 0, jnp.argmax(tile_count > 0).astype(jnp.int32), eff)
    i32 = lambda a: a.astype(jnp.int32)
    return i32(pos), row_token, i32(tile_first), i32(tile_count), i32(eff), i32(tile_end[-1:])


def kernel(x_prompt, x_sample, cache_k, cache_v, state_conv, c_prompt, c_sample, w_ada, b_ada, norm_mix_g, norm_ffn_g, norm_final_g, w_in, b_in, conv_w, conv_b, sinks, out_norm_conv_g, out_norm_attn_g, w_out, b_out, w_router, b_router, w_gate_up, b_gate_up, w_down, b_down):
    d = x_prompt.shape[2]
    depth, dec_b, window, n_kv, head_dim = cache_k.shape
    assert depth == 1
    conv_ch = conv_w.shape[1]
    n_exp = w_router.shape[1]
    n_heads = sinks.shape[0]
    bp, tp, _ = x_prompt.shape
    bs, ts, _ = x_sample.shape
    col_q = 3 * conv_ch
    kv_w = n_kv * head_dim
    att_w = n_heads * head_dim

    mod = _adaln(jnp.concatenate([c_prompt, c_sample], axis=0), w_ada, b_ada)
    mod_p = mod[:bp].reshape(bp, 1, 6 * d)
    mod_s = mod[bp:].reshape(bs, 1, 6 * d)
    order = _head_order(n_heads, n_kv)
    w_in_bf = w_in.astype(BF16)
    w_in_bf = lax.dynamic_update_slice(
        w_in_bf, _permute_blocks(w_in_bf[:, col_q:col_q + att_w], order, head_dim, 1, 0), (0, col_q))
    b_in_p = _permute_blocks(b_in, order, head_dim, 0, col_q)
    w_out_bf = w_out.astype(BF16)
    w_out_bf = lax.dynamic_update_slice(
        w_out_bf, _permute_blocks(w_out_bf[conv_ch:], order, head_dim, 0, 0), (conv_ch, 0))
    attn_g = _permute_blocks(out_norm_attn_g, order, head_dim, 0, 0)
    wr_hi = w_router.astype(BF16)
    wr_lo = (w_router - wr_hi.astype(F32)).astype(BF16)
    n_p, n_s = bp * tp, bs * ts

    def mixer(x, modx, conv_prev, ck, cv, counts_before, hp_prev, tok_offset):
        b, t, _ = x.shape
        proj = _inproj(x, modx, norm_mix_g, w_in_bf, b_in_p)
        yc, conv_state = _conv_mix(proj, conv_prev, conv_w, conv_b, out_norm_conv_g)
        if ck is None:
            ya, k_rot = _swa_banded(proj, sinks, attn_g, n_kv, head_dim, window, col_q)
            keep = window
        else:
            ya, k_rot = _swa_cached(proj, ck, cv, sinks, attn_g, n_kv, head_dim, col_q)
            keep = t
        k_rows = k_rot[:, t - keep:].reshape(1, b, keep, n_kv, head_dim)
        v_rows = proj[:, t - keep:, col_q + att_w + kv_w:].reshape(1, b, keep, n_kv, head_dim)
        x1 = _outproj(yc, ya, w_out_bf, b_out, x, modx)
        routed = _router(x1, modx, norm_ffn_g, wr_hi, wr_lo, b_router, counts_before, hp_prev,
                         tok_offset, n_p + n_s)
        return x1, routed, conv_state, k_rows, v_rows

    zero_state = jnp.zeros((bp, 2, conv_ch), F32)
    x1p, (hp_buf, idxp, gatep, rankp, cntp), conv_p, k_prompt, v_prompt = mixer(
        x_prompt, mod_p, zero_state, None, None, jnp.zeros((1, n_exp), F32), None, 0)
    x1s, (h2_packed, idxs, gates_s, ranks, cnt), conv_s, k_sample, v_sample = mixer(
        x_sample, mod_s, state_conv[0],
        cache_k[0].reshape(dec_b, window, kv_w), cache_v[0].reshape(dec_b, window, kv_w), cntp, hp_buf, n_p)

    top_idx = jnp.concatenate([idxp, idxs], axis=0)
    gates = jnp.concatenate([gatep, gates_s], axis=0)
    rank = jnp.concatenate([rankp, ranks], axis=0)
    pos, row_token, tile_first, tile_count, eff, n_tiles = _routing(
        top_idx, rank, cnt[0].astype(jnp.int32), MOE_ROWS)
    xs = _gather_rows(h2_packed, row_token, n_tiles, MOE_ROWS)
    act = _moe_up(xs, w_gate_up, b_gate_up, tile_first, tile_count, eff, n_tiles)
    y_sorted = _moe_down(act, w_down, b_down, tile_first, tile_count, eff, n_tiles)
    y_prompt = _combine(y_sorted, pos, gates, x1p, mod_p, norm_final_g, 0)
    y_sample = _combine(y_sorted, pos, gates, x1s, mod_s, norm_final_g, n_p)
    return (y_prompt, y_sample, k_prompt, v_prompt, conv_p[None], k_sample, v_sample, conv_s[None])
```

```python
import functools

import jax
import jax.numpy as jnp
from jax import lax
from jax.experimental import pallas as pl
from jax.experimental.pallas import tpu as pltpu

F32 = jnp.float32
BF16 = jnp.bfloat16
U32 = jnp.uint32

CHUNK = 64
TOP_K = 4
PAST_LEN = 2048
ROPE_THETA = 10000.0
NORM_EPS = 1e-5
SWIGLU_LIMIT = 7.0
SWIGLU_ALPHA = 1.702

LANES = 128
SUBLANES = 8
VMEM_LIMIT_BYTES = 58 * 1024 * 1024

ROW_TILE = 512
ADA_COL_TILE = 512
INPROJ_COL_TILE = 1024
OUTPROJ_COL_TILE = 1024
CONV_ROW_TILE = 256
ATTN_CHUNKS_PER_STEP = 4
FFN_ROW_TILE = 256
MOE_ROWS = 256
MOE_FF_TILE = 512
MOE_OUT_TILE = 1024
MOE_IN_SLOTS = 5
MOE_OUT_SLOTS = 4
MOE_CAST_CHUNKS = 4
COMBINE_ROWS = 128
ROW_COPY_UNROLL = 8
INDEX_BATCH_TILES = 8
TILE_COPY_PRIORITY = 1


def _tile(pref, dim):
    t = min(pref, dim)
    while dim % t:
        t -= 1
    return t


def _params(sem):
    return pltpu.CompilerParams(dimension_semantics=sem, vmem_limit_bytes=VMEM_LIMIT_BYTES)


def _rms(x):
    return x * lax.rsqrt(jnp.mean(x * x, axis=-1, keepdims=True) + NORM_EPS)


def _token_blocks(b, t, rows_pref):
    if t >= rows_pref:
        return 1, _tile(rows_pref, t)
    return _tile(max(rows_pref // t, 1), b), t


def _adaln_kernel(c_ref, w_ref, b_ref, o_ref):
    c = c_ref[...]
    a = (c * jax.nn.sigmoid(c)).astype(BF16)
    o_ref[...] = jnp.dot(a, w_ref[...].astype(BF16), preferred_element_type=F32) + b_ref[...]


def _adaln(c, w_ada, b_ada):
    nb, d = c.shape
    n = w_ada.shape[1]
    tn = _tile(ADA_COL_TILE, n)
    return pl.pallas_call(
        _adaln_kernel,
        grid=(n // tn,),
        in_specs=[pl.BlockSpec((nb, d), lambda j: (0, 0)),
                  pl.BlockSpec((d, tn), lambda j: (0, j)),
                  pl.BlockSpec((1, tn), lambda j: (0, j))],
        out_specs=pl.BlockSpec((nb, tn), lambda j: (0, j)),
        out_shape=jax.ShapeDtypeStruct((nb, n), F32),
        compiler_params=_params(("arbitrary",)),
        name="adaln",
    )(c, w_ada, b_ada.reshape(1, n))


def _inproj_kernel(x_ref, sh_ref, sc_ref, g_ref, w_ref, b_ref, o_ref, h_ref):
    bb, tt, d = x_ref.shape

    @pl.when(pl.program_id(2) == 0)
    def _():
        y = _rms(x_ref[...]) * g_ref[...]
        h = y * (1.0 + sc_ref[...]) + sh_ref[...]
        h_ref[...] = h.reshape(bb * tt, d).astype(BF16)

    acc = jnp.dot(h_ref[...], w_ref[...], preferred_element_type=F32) + b_ref[...]
    o_ref[...] = acc.reshape(o_ref.shape)


def _inproj(x, mod, g, w_bf, b_in):
    b, t, d = x.shape
    n = w_bf.shape[1]
    bb, tt = _token_blocks(b, t, ROW_TILE)
    tn = _tile(INPROJ_COL_TILE, n)
    return pl.pallas_call(
        _inproj_kernel,
        grid=(b // bb, t // tt, n // tn),
        in_specs=[pl.BlockSpec((bb, tt, d), lambda i, s, j: (i, s, 0)),
                  pl.BlockSpec((bb, 1, d), lambda i, s, j: (i, 0, 0)),
                  pl.BlockSpec((bb, 1, d), lambda i, s, j: (i, 0, 1)),
                  pl.BlockSpec((1, d), lambda i, s, j: (0, 0)),
                  pl.BlockSpec((d, tn), lambda i, s, j: (0, j)),
                  pl.BlockSpec((1, tn), lambda i, s, j: (0, j))],
        out_specs=pl.BlockSpec((bb, tt, tn), lambda i, s, j: (i, s, j)),
        out_shape=jax.ShapeDtypeStruct((b, t, n), F32),
        scratch_shapes=[pltpu.VMEM((bb * tt, d), BF16)],
        compiler_params=_params(("arbitrary", "arbitrary", "arbitrary")),
        name="inproj",
    )(x, mod, mod, g.reshape(1, d), w_bf, b_in.reshape(1, n))


def _conv_kernel(bg_ref, cg_ref, xc_ref, cgp_ref, xcp_ref, st_ref, w_ref, b_ref, g_ref, y_ref, so_ref):
    tt = cg_ref.shape[0]
    u = cg_ref[...] * xc_ref[...]
    halo = cgp_ref[...] * xcp_ref[...]
    st = st_ref[...]
    first = pl.program_id(1) == 0
    um2 = jnp.where(first, st[0:1], halo[SUBLANES - 2:SUBLANES - 1])
    um1 = jnp.where(first, st[1:2], halo[SUBLANES - 1:SUBLANES])
    row = lax.broadcasted_iota(jnp.int32, u.shape, 0)
    u1 = jnp.where(row == 0, um1, pltpu.roll(u, 1, 0))
    u2 = jnp.where(row == 0, um2, jnp.where(row == 1, um1, pltpu.roll(u, 2, 0)))
    w = w_ref[...]
    conv = b_ref[...] + u2 * w[0:1]
    conv = conv + u1 * w[1:2]
    conv = conv + u * w[2:3]
    y = _rms(bg_ref[...] * conv) * g_ref[...]
    y_ref[...] = y.astype(BF16)
    so_ref[...] = u[tt - 2:tt]


def _conv_mix(proj, state, conv_w, conv_b, g):
    b, t, _ = proj.shape
    c = conv_w.shape[1]
    assert conv_w.shape[0] == 3
    tt = _tile(CONV_ROW_TILE, t)
    hs = tt // SUBLANES
    halo_map = lambda col: (lambda i, s: (i, jnp.maximum(s * hs - 1, 0), col))
    return pl.pallas_call(
        _conv_kernel,
        grid=(b, t // tt),
        in_specs=[pl.BlockSpec((None, tt, c), lambda i, s: (i, s, 0)),
                  pl.BlockSpec((None, tt, c), lambda i, s: (i, s, 1)),
                  pl.BlockSpec((None, tt, c), lambda i, s: (i, s, 2)),
                  pl.BlockSpec((None, SUBLANES, c), halo_map(1)),
                  pl.BlockSpec((None, SUBLANES, c), halo_map(2)),
                  pl.BlockSpec((None, 2, c), lambda i, s: (i, 0, 0)),
                  pl.BlockSpec((3, c), lambda i, s: (0, 0)),
                  pl.BlockSpec((1, c), lambda i, s: (0, 0)),
                  pl.BlockSpec((1, c), lambda i, s: (0, 0))],
        out_specs=[pl.BlockSpec((None, tt, c), lambda i, s: (i, s, 0)),
                   pl.BlockSpec((None, 2, c), lambda i, s: (i, 0, 0))],
        out_shape=[jax.ShapeDtypeStruct((b, t, c), BF16),
                   jax.ShapeDtypeStruct((b, 2, c), F32)],
        compiler_params=_params(("arbitrary", "arbitrary")),
        name="conv_mix",
    )(proj, proj, proj, proj, proj, state, conv_w, conv_b.reshape(1, c), g.reshape(1, c))


def _head_order(n_heads, n_kv):
    grp = n_heads // n_kv
    assert n_kv % 2 == 0
    return [(2 * p + half) * grp + g for p in range(n_kv // 2) for g in range(grp) for half in (0, 1)]


def _permute_blocks(x, order, width, axis, start):
    n = len(order) * width
    take = lambda lo, hi: lax.slice_in_dim(x, lo, hi, axis=axis)
    parts = [take(0, start)] + [take(start + h * width, start + (h + 1) * width) for h in order]
    parts.append(take(start + n, x.shape[axis]))
    return jnp.concatenate(parts, axis=axis)


def _rope_tables(pos, head_dim):
    half = head_dim // 2
    inv_freq = ROPE_THETA ** (-jnp.arange(half, dtype=F32) / half)
    ang = pos.astype(F32)[:, None] * inv_freq[None, :]
    cos, sin = jnp.cos(ang), jnp.sin(ang)
    reps = LANES // head_dim
    return (jnp.tile(jnp.concatenate([cos, cos], axis=-1), (1, reps)),
            jnp.tile(jnp.concatenate([-sin, sin], axis=-1), (1, reps)))


def _rope(x, cos, sin, head_dim):
    width = x.shape[1]
    half = head_dim // 2
    reps = width // LANES
    cos = jnp.concatenate([cos] * reps, axis=1)
    sin = jnp.concatenate([sin] * reps, axis=1)
    lane = lax.broadcasted_iota(jnp.int32, x.shape, 1)
    swapped = jnp.where(lane % head_dim < half,
                        pltpu.roll(x, width - half, 1), pltpu.roll(x, half, 1))
    return x * cos + swapped * sin


def _pair_blocks(slab):
    keys = slab.shape[0]
    lo = lax.broadcasted_iota(jnp.int32, (CHUNK, LANES), 1) < LANES // 2
    zero = jnp.zeros((CHUNK, LANES), slab.dtype)
    parts = []
    for c in range(keys // CHUNK):
        blk = slab[c * CHUNK:(c + 1) * CHUNK]
        parts += [jnp.where(lo, blk, zero), jnp.where(lo, zero, blk)]
    return jnp.concatenate(parts, axis=0)


def _attend_pair(qs, kblk, vblk, sink_vec, key_ok):
    head_dim = LANES // 2
    n_chunks = kblk.shape[0] // LANES
    lo = lax.broadcasted_iota(jnp.int32, (1, LANES), 1) < head_dim
    s = lax.dot_general(qs.astype(BF16), kblk, (((1,), (1,)), ((), ())),
                        preferred_element_type=F32) * (head_dim ** -0.5)
    tiles = [jnp.where(key_ok[c], s[:, c * LANES:(c + 1) * LANES], -jnp.inf) for c in range(n_chunks)]

    def per_half(t, reduce_fn, fill):
        a = reduce_fn(jnp.where(lo, t, fill), axis=-1, keepdims=True)
        b = reduce_fn(jnp.where(lo, fill, t), axis=-1, keepdims=True)
        return jnp.where(lo, a, b)

    tmax = tiles[0]
    for t in tiles[1:]:
        tmax = jnp.maximum(tmax, t)
    m = jnp.maximum(per_half(tmax, jnp.max, -jnp.inf), sink_vec)
    es = [jnp.exp(t - m) for t in tiles]
    esum = es[0]
    for e in es[1:]:
        esum = esum + e
    den = per_half(esum, jnp.sum, 0.0) + jnp.exp(sink_vec - m)
    e_all = jnp.concatenate([e.astype(BF16) for e in es], axis=1)
    return jnp.dot(e_all, vblk, preferred_element_type=F32) / den


def _attend(q, kslabs, vslabs, sinks_ref, key_ok, grp):
    rows = q.shape[0]
    head_dim = LANES // 2
    lo = lax.broadcasted_iota(jnp.int32, (1, LANES), 1) < head_dim
    outs = []
    for p in range(len(kslabs)):
        qs = jnp.concatenate([q[:, (p * grp + g) * LANES:(p * grp + g + 1) * LANES] for g in range(grp)], axis=0)
        sink_vec = jnp.concatenate(
            [jnp.broadcast_to(jnp.where(lo, sinks_ref[2 * p * grp + g], sinks_ref[(2 * p + 1) * grp + g]),
                              (rows, LANES)) for g in range(grp)], axis=0)
        o = _attend_pair(qs, kslabs[p], vslabs[p], sink_vec, key_ok)
        outs.extend(o[g * rows:(g + 1) * rows] for g in range(grp))
    return jnp.concatenate(outs, axis=1)


def _swa_banded_kernel(sinks_ref, q_ref, k_ref, v_ref, cos_ref, sin_ref, g_ref,
                       y_ref, kr_ref, kbuf, vbuf, *, grp, window):
    rb = q_ref.shape[0]
    head_dim = LANES // 2
    n_pairs = kbuf.shape[1] // LANES
    step = pl.program_id(1)

    @pl.when(step == 0)
    def _():
        kbuf[0:window] = jnp.zeros((window, kbuf.shape[1]), BF16)
        vbuf[0:window] = jnp.zeros((window, vbuf.shape[1]), BF16)

    @pl.when(step > 0)
    def _():
        kbuf[0:window] = kbuf[rb:rb + window]
        vbuf[0:window] = vbuf[rb:rb + window]

    cos, sin = cos_ref[...], sin_ref[...]
    k = _rope(k_ref[...], cos, sin, head_dim)
    kr_ref[...] = k
    kbuf[window:window + rb] = k.astype(BF16)
    vbuf[window:window + rb] = v_ref[...].astype(BF16)
    q = _rope(q_ref[...], cos, sin, head_dim)

    n_buf_chunks = (window + rb) // CHUNK
    kblocks = [[_pair_blocks(kbuf[c * CHUNK:(c + 1) * CHUNK, p * LANES:(p + 1) * LANES])
                for c in range(n_buf_chunks)] for p in range(n_pairs)]
    vblocks = [[_pair_blocks(vbuf[c * CHUNK:(c + 1) * CHUNK, p * LANES:(p + 1) * LANES])
                for c in range(n_buf_chunks)] for p in range(n_pairs)]
    band_chunks = window // CHUNK + 1
    lane_key = lax.broadcasted_iota(jnp.int32, (1, LANES), 1) % CHUNK
    for c in range(rb // CHUNK):
        lo_row = c * CHUNK
        key_ok = [step * rb + (c + b) * CHUNK + lane_key >= window for b in range(band_chunks)]
        ks = [jnp.concatenate(kblocks[p][c:c + band_chunks], axis=0) for p in range(n_pairs)]
        vs = [jnp.concatenate(vblocks[p][c:c + band_chunks], axis=0) for p in range(n_pairs)]
        y = _attend(q[lo_row:lo_row + CHUNK], ks, vs, sinks_ref, key_ok, grp)
        y_ref[lo_row:lo_row + CHUNK] = (_rms(y) * g_ref[...]).astype(BF16)


def _swa_banded(proj, sinks, g, n_kv, head_dim, window, col_q):
    b, t, _ = proj.shape
    assert 2 * head_dim == LANES and CHUNK == head_dim
    n_heads = sinks.shape[0]
    att_w = n_heads * head_dim
    kv_w = n_kv * head_dim
    rb = _tile(ATTN_CHUNKS_PER_STEP * CHUNK, t)
    assert window % CHUNK == 0 and t % CHUNK == 0 and rb >= window
    cos, sin = _rope_tables(jnp.arange(t, dtype=jnp.int32), head_dim)
    col_k = (col_q + att_w) // kv_w
    kern = functools.partial(_swa_banded_kernel, grp=n_heads // n_kv, window=window)
    return pl.pallas_call(
        kern,
        grid=(b, t // rb),
        in_specs=[pl.BlockSpec(memory_space=pltpu.SMEM),
                  pl.BlockSpec((None, rb, att_w), lambda i, s: (i, s, col_q // att_w)),
                  pl.BlockSpec((None, rb, kv_w), lambda i, s: (i, s, col_k)),
                  pl.BlockSpec((None, rb, kv_w), lambda i, s: (i, s, col_k + 1)),
                  pl.BlockSpec((rb, LANES), lambda i, s: (s, 0)),
                  pl.BlockSpec((rb, LANES), lambda i, s: (s, 0)),
                  pl.BlockSpec((1, att_w), lambda i, s: (0, 0))],
        out_specs=[pl.BlockSpec((None, rb, att_w), lambda i, s: (i, s, 0)),
                   pl.BlockSpec((None, rb, kv_w), lambda i, s: (i, s, 0))],
        out_shape=[jax.ShapeDtypeStruct((b, t, att_w), BF16),
                   jax.ShapeDtypeStruct((b, t, kv_w), F32)],
        scratch_shapes=[pltpu.VMEM((window + rb, kv_w), BF16),
                        pltpu.VMEM((window + rb, kv_w), BF16)],
        compiler_params=_params(("arbitrary", "arbitrary")),
        name="swa_banded",
    )(sinks, proj, proj, proj, cos, sin, g.reshape(1, att_w))


def _swa_cached_kernel(sinks_ref, q_ref, k_ref, v_ref, ck_ref, cv_ref, cos_ref, sin_ref, g_ref,
                       y_ref, kr_ref, *, grp):
    head_dim = LANES // 2
    t, kv_w = k_ref.shape
    window = ck_ref.shape[0]
    n_pairs = kv_w // LANES
    cos, sin = cos_ref[...], sin_ref[...]
    k = _rope(k_ref[...], cos, sin, head_dim)
    kr_ref[...] = k
    q = _rope(q_ref[...], cos, sin, head_dim)
    n_keys = window + t
    pad = -n_keys % CHUNK
    tail = [jnp.zeros((pad, kv_w), BF16)] if pad else []
    kb = jnp.concatenate([ck_ref[...].astype(BF16), k.astype(BF16)] + tail, axis=0)
    vb = jnp.concatenate([cv_ref[...].astype(BF16), v_ref[...].astype(BF16)] + tail, axis=0)
    ks = [_pair_blocks(kb[:, p * LANES:(p + 1) * LANES]) for p in range(n_pairs)]
    vs = [_pair_blocks(vb[:, p * LANES:(p + 1) * LANES]) for p in range(n_pairs)]
    lane_key = lax.broadcasted_iota(jnp.int32, (1, LANES), 1) % CHUNK
    key_ok = [c * CHUNK + lane_key < n_keys for c in range((n_keys + pad) // CHUNK)]
    y = _attend(q, ks, vs, sinks_ref, key_ok, grp)
    y_ref[...] = (_rms(y) * g_ref[...]).astype(BF16)


def _swa_cached(proj, cache_k, cache_v, sinks, g, n_kv, head_dim, col_q):
    b, t, _ = proj.shape
    assert 2 * head_dim == LANES and CHUNK == head_dim
    window = cache_k.shape[1]
    n_heads = sinks.shape[0]
    att_w = n_heads * head_dim
    kv_w = n_kv * head_dim
    cos, sin = _rope_tables(PAST_LEN + jnp.arange(t, dtype=jnp.int32), head_dim)
    col_k = (col_q + att_w) // kv_w
    kern = functools.partial(_swa_cached_kernel, grp=n_heads // n_kv)
    return pl.pallas_call(
        kern,
        grid=(b,),
        in_specs=[pl.BlockSpec(memory_space=pltpu.SMEM),
                  pl.BlockSpec((None, t, att_w), lambda i: (i, 0, col_q // att_w)),
                  pl.BlockSpec((None, t, kv_w), lambda i: (i, 0, col_k)),
                  pl.BlockSpec((None, t, kv_w), lambda i: (i, 0, col_k + 1)),
                  pl.BlockSpec((None, window, kv_w), lambda i: (i, 0, 0)),
                  pl.BlockSpec((None, window, kv_w), lambda i: (i, 0, 0)),
                  pl.BlockSpec((t, LANES), lambda i: (0, 0)),
                  pl.BlockSpec((t, LANES), lambda i: (0, 0)),
                  pl.BlockSpec((1, att_w), lambda i: (0, 0))],
        out_specs=[pl.BlockSpec((None, t, att_w), lambda i: (i, 0, 0)),
                   pl.BlockSpec((None, t, kv_w), lambda i: (i, 0, 0))],
        out_shape=[jax.ShapeDtypeStruct((b, t, att_w), BF16),
                   jax.ShapeDtypeStruct((b, t, kv_w), F32)],
        compiler_params=_params(("arbitrary",)),
        name="swa_cached",
    )(sinks, proj, proj, proj, cache_k, cache_v, cos, sin, g.reshape(1, att_w))


def _outproj_kernel(yc_ref, ya_ref, wt_ref, wb_ref, b_ref, x_ref, g1_ref, o_ref):
    bb, tt, c = yc_ref.shape
    acc = jnp.dot(yc_ref[...].reshape(bb * tt, c), wt_ref[...], preferred_element_type=F32)
    acc = acc + jnp.dot(ya_ref[...].reshape(bb * tt, c), wb_ref[...], preferred_element_type=F32)
    acc = acc + b_ref[...]
    o_ref[...] = x_ref[...] + g1_ref[...] * acc.reshape(o_ref.shape)


def _outproj(yc, ya, w_bf, b_out, x, mod):
    b, t, d = x.shape
    c = yc.shape[2]
    assert 2 * c == w_bf.shape[0]
    bb, tt = _token_blocks(b, t, ROW_TILE)
    tn = _tile(OUTPROJ_COL_TILE, d)
    return pl.pallas_call(
        _outproj_kernel,
        grid=(b // bb, t // tt, d // tn),
        in_specs=[pl.BlockSpec((bb, tt, c), lambda i, s, j: (i, s, 0)),
                  pl.BlockSpec((bb, tt, c), lambda i, s, j: (i, s, 0)),
                  pl.BlockSpec((c, tn), lambda i, s, j: (0, j)),
                  pl.BlockSpec((c, tn), lambda i, s, j: (1, j)),
                  pl.BlockSpec((1, tn), lambda i, s, j: (0, j)),
                  pl.BlockSpec((bb, tt, tn), lambda i, s, j: (i, s, j)),
                  pl.BlockSpec((bb, 1, tn), lambda i, s, j: (i, 0, 2 * (d // tn) + j))],
        out_specs=pl.BlockSpec((bb, tt, tn), lambda i, s, j: (i, s, j)),
        out_shape=jax.ShapeDtypeStruct((b, t, d), F32),
        compiler_params=_params(("arbitrary", "arbitrary", "arbitrary")),
        name="outproj",
    )(yc, ya, w_bf, w_bf, b_out.reshape(1, d), x, mod)


def _router_kernel(x_ref, sh_ref, sc_ref, g_ref, whi_ref, wlo_ref, br_ref, c0_ref, *rest, has_prev):
    hp_ref, idx_ref, gate_ref, rank_ref, cnt_ref, carry = rest[1:] if has_prev else rest
    bb, tt, d = x_ref.shape
    rows = bb * tt
    n_exp = whi_ref.shape[1]

    @pl.when((pl.program_id(0) == 0) & (pl.program_id(1) == 0))
    def _():
        carry[...] = c0_ref[...]

    y = _rms(x_ref[...]) * g_ref[...]
    h = (y * (1.0 + sc_ref[...]) + sh_ref[...]).reshape(rows, d)
    h_hi = h.astype(BF16)
    h_hi32 = h_hi.astype(F32)
    bits = lax.bitcast_convert_type(h_hi32, U32)
    hp_ref[...] = (bits[:, d // 2:] & jnp.uint32(0xFFFF0000)) | (bits[:, :d // 2] >> 16)

    h_lo = (h - h_hi32).astype(BF16)
    logits = jnp.dot(h_hi, wlo_ref[...], preferred_element_type=F32)
    logits = logits + jnp.dot(h_lo, whi_ref[...], preferred_element_type=F32)
    logits = logits + jnp.dot(h_hi, whi_ref[...], preferred_element_type=F32) + br_ref[...]
    col = lax.broadcasted_iota(jnp.int32, logits.shape, 1).astype(F32)
    work = logits
    vals, idxs = [], []
    for _ in range(TOP_K):
        m = jnp.max(work, axis=-1, keepdims=True)
        sel = jnp.min(jnp.where(work == m, col, float(n_exp)), axis=-1, keepdims=True)
        vals.append(m)
        idxs.append(sel)
        work = jnp.where(col == sel, -jnp.inf, work)
    exps = [jnp.exp(v - vals[0]) for v in vals]
    den = exps[0]
    for e in exps[1:]:
        den = den + e
    idx_ref[...] = jnp.concatenate(idxs, axis=1).astype(jnp.int32)
    gate_ref[...] = jnp.concatenate([e / den for e in exps], axis=1)

    onehots = [(col == sel).astype(F32) for sel in idxs]
    cnt = onehots[0]
    for oh in onehots[1:]:
        cnt = cnt + oh
    earlier = (lax.broadcasted_iota(jnp.int32, (rows, rows), 0)
               > lax.broadcasted_iota(jnp.int32, (rows, rows), 1)).astype(BF16)
    before = jnp.dot(earlier, cnt.astype(BF16), preferred_element_type=F32) + carry[...]
    ranks = [jnp.sum(oh * before, axis=-1, keepdims=True) for oh in onehots]
    rank_ref[...] = jnp.concatenate(ranks, axis=1).astype(jnp.int32)
    carry[...] = carry[...] + jnp.sum(cnt, axis=0, keepdims=True)
    cnt_ref[...] = carry[...]


def _router(x1, mod, g, w_hi, w_lo, b_router, counts_before, hp_prev, tok_offset, n_tok_all):
    b, t, d = x1.shape
    n_exp = w_hi.shape[1]
    bb, tt = _token_blocks(b, t, FFN_ROW_TILE)
    rows = bb * tt
    n_tok = b * t
    assert tok_offset % rows == 0
    blk0 = tok_offset // rows
    flat = lambda i, s: (i * (t // tt) + s, 0)
    has_prev = hp_prev is not None
    in_specs = [pl.BlockSpec((bb, tt, d), lambda i, s: (i, s, 0)),
                pl.BlockSpec((bb, 1, d), lambda i, s: (i, 0, 3)),
                pl.BlockSpec((bb, 1, d), lambda i, s: (i, 0, 4)),
                pl.BlockSpec((1, d), lambda i, s: (0, 0)),
                pl.BlockSpec((d, n_exp), lambda i, s: (0, 0)),
                pl.BlockSpec((d, n_exp), lambda i, s: (0, 0)),
                pl.BlockSpec((1, n_exp), lambda i, s: (0, 0)),
                pl.BlockSpec((1, n_exp), lambda i, s: (0, 0))]
    args = [x1, mod, mod, g.reshape(1, d), w_hi, w_lo, b_router.reshape(1, n_exp), counts_before]
    if has_prev:
        in_specs.append(pl.BlockSpec(memory_space=pl.ANY))
        args.append(hp_prev)
    return pl.pallas_call(
        functools.partial(_router_kernel, has_prev=has_prev),
        grid=(b // bb, t // tt),
        in_specs=in_specs,
        out_specs=[pl.BlockSpec((rows, d // 2), lambda i, s: (blk0 + i * (t // tt) + s, 0)),
                   pl.BlockSpec((rows, TOP_K), flat),
                   pl.BlockSpec((rows, TOP_K), flat),
                   pl.BlockSpec((rows, TOP_K), flat),
                   pl.BlockSpec((1, n_exp), lambda i, s: (0, 0))],
        out_shape=[jax.ShapeDtypeStruct((n_tok_all, d // 2), U32),
                   jax.ShapeDtypeStruct((n_tok, TOP_K), jnp.int32),
                   jax.ShapeDtypeStruct((n_tok, TOP_K), F32),
                   jax.ShapeDtypeStruct((n_tok, TOP_K), jnp.int32),
                   jax.ShapeDtypeStruct((1, n_exp), F32)],
        scratch_shapes=[pltpu.VMEM((1, n_exp), F32)],
        input_output_aliases={len(args) - 1: 0} if has_prev else {},
        compiler_params=_params(("arbitrary", "arbitrary")),
        name="router",
    )(*args)


def _stage_indices(idx_batch_ref, idx_smem, sem_idx):
    cp = pltpu.make_async_copy(idx_batch_ref.at[0, 0], idx_smem, sem_idx)
    cp.start()
    cp.wait()


def _issue_row_copies(idx_smem, base, src_ref, dst_ref, sem):
    def body(g, carry):
        for u in range(ROW_COPY_UNROLL):
            r = g * ROW_COPY_UNROLL + u
            pltpu.make_async_copy(src_ref.at[pl.ds(idx_smem[base + r], 1)], dst_ref.at[pl.ds(r, 1)], sem).start(
                priority=u % 2)
        return carry

    lax.fori_loop(0, dst_ref.shape[0] // ROW_COPY_UNROLL, body, 0)


def _issue_tile(tile, first_tile, idx0_ref, idxn_ref, idx_smem, sem_idx, src_ref, dst_ref, sem):
    n_rows = dst_ref.shape[0]
    batch = idx_smem.shape[0] // n_rows
    rel = tile - first_tile

    @pl.when(rel == 0)
    def _():
        _stage_indices(idx0_ref, idx_smem, sem_idx)

    @pl.when((rel > 0) & (lax.rem(rel, batch) == 0))
    def _():
        _stage_indices(idxn_ref, idx_smem, sem_idx)

    _issue_row_copies(idx_smem, lax.rem(rel, batch) * n_rows, src_ref, dst_ref, sem)


def _wait_row_copies(src_ref, dst_ref, sem):
    pltpu.make_async_copy(src_ref.at[pl.ds(0, dst_ref.shape[0])], dst_ref, sem).wait()


def _gather_kernel(nused_ref, idx0_ref, idxn_ref, src_ref, o_ref, idx_smem, buf, sem_idx, sem_rows):
    i = pl.program_id(0)
    n = nused_ref[0]
    half = src_ref.shape[1]

    @pl.when(i == 0)
    def _():
        _issue_tile(jnp.int32(0), 0, idx0_ref, idxn_ref, idx_smem, sem_idx, src_ref, buf.at[0], sem_rows.at[0])

    @pl.when(i + 1 < n)
    def _():
        nxt = (i + 1) % 2
        _issue_tile(i + 1, 0, idx0_ref, idxn_ref, idx_smem, sem_idx, src_ref, buf.at[nxt], sem_rows.at[nxt])

    @pl.when(i < n)
    def _():
        slot = i % 2
        _wait_row_copies(src_ref, buf.at[slot], sem_rows.at[slot])
        w = buf[slot]
        o_ref[:, :half] = _pair_rows(lax.bitcast_convert_type(w << 16, F32).astype(BF16))
        o_ref[:, half:] = _pair_rows(lax.bitcast_convert_type(w & jnp.uint32(0xFFFF0000), F32).astype(BF16))


def _pair_rows(x):
    return pltpu.bitcast(x, U32)


def _gather_rows(src_packed, row_src, n_used, rows):
    p = row_src.shape[0]
    half = src_packed.shape[1]
    nblk = p // rows
    batch = _tile(INDEX_BATCH_TILES, nblk)
    idx_batches = row_src.reshape(nblk // batch, 1, batch * rows)
    clamp = lambda i, nu: jnp.minimum(i, nu[0] - 1)
    return pl.pallas_call(
        _gather_kernel,
        grid_spec=pltpu.PrefetchScalarGridSpec(
            num_scalar_prefetch=1,
            grid=(nblk,),
            in_specs=[pl.BlockSpec((1, 1, batch * rows), lambda i, nu: (0, 0, 0)),
                      pl.BlockSpec((1, 1, batch * rows), lambda i, nu: (clamp(i + 1, nu) // batch, 0, 0)),
                      pl.BlockSpec(memory_space=pl.ANY)],
            out_specs=pl.BlockSpec((rows // 2, 2 * half), lambda i, nu: (clamp(i, nu), 0)),
            scratch_shapes=[pltpu.SMEM((batch * rows,), jnp.int32),
                            pltpu.VMEM((2, rows, half), U32),
                            pltpu.SemaphoreType.DMA(()),
                            pltpu.SemaphoreType.DMA((2,))]),
        out_shape=jax.ShapeDtypeStruct((p // 2, 2 * half), U32),
        compiler_params=_params(("arbitrary",)),
        name="moe_gather",
    )(n_used, idx_batches, idx_batches, src_packed)


def _stream_expert_tiles(first_ref, ntile_ref, ntot_ref, half_ref, src_hbm, dst_hbm, xbuf, obuf, sem_in, sem_out,
                         compute_first, compute):
    j, e = pl.program_id(0), pl.program_id(1)
    nj, ne = pl.num_programs(0), pl.num_programs(1)
    n_slots = xbuf.shape[0]
    lookahead = n_slots - 2
    in_rows = xbuf.shape[1]
    n_out = obuf.shape[0]
    out_rows = obuf.shape[1]
    tn = obuf.shape[2]
    n_total = ntot_ref[0]
    n_seq = nj * n_total

    def x_copy(t, slot):
        r0 = pl.multiple_of(t * in_rows, in_rows)
        return pltpu.make_async_copy(src_hbm.at[pl.ds(r0, in_rows)], xbuf.at[slot], sem_in.at[slot])

    def x_start(seq):
        x_copy(lax.rem(seq, n_total), lax.rem(seq, n_slots)).start(priority=TILE_COPY_PRIORITY)

    def o_copy(t, slot):
        r0 = pl.multiple_of(t * out_rows, out_rows)
        c0 = pl.multiple_of(j * tn, tn)
        return pltpu.make_async_copy(obuf.at[slot], dst_hbm.at[pl.ds(r0, out_rows), pl.ds(c0, tn)],
                                     sem_out.at[slot])

    @pl.when((j == 0) & (e == 0))
    def _():
        for seq in range(lookahead):
            @pl.when(seq < n_seq)
            def _():
                x_start(jnp.int32(seq))

    def step(t, compute_fn, width, half_last=False):
        seq = j * n_total + t
        in_slots = [lax.rem(seq + k, n_slots) for k in range(width)]
        out_slots = [lax.rem(seq + k, n_out) for k in range(width)]
        for k in range(width):
            x_copy(0, in_slots[k]).wait()

        for k in range(width):
            @pl.when(seq + lookahead + k < n_seq)
            def _():
                x_start(seq + lookahead + k)

        for k in range(width):
            @pl.when(seq + k >= n_out)
            def _():
                o_copy(t, out_slots[k]).wait()

        for k in range(width):
            if half_last and k == width - 1:
                obuf[out_slots[k], 0:out_rows // 2] = compute_fn(xbuf[in_slots[k], 0:in_rows // 2])
            else:
                obuf[out_slots[k]] = compute_fn(xbuf[in_slots[k]])
        for k in range(width):
            o_copy(t + k, out_slots[k]).start()

    t0 = first_ref[e]
    nt = ntile_ref[e]

    @pl.when(nt > 0)
    def _():
        step(t0, compute_first, 1)

    half = (half_ref[e] > 0) & (nt > 1)
    pair_ends = (nt > 1) & (lax.rem(nt - 1, 2) == 0)
    lone_last = (nt > 1) & (lax.rem(nt - 1, 2) == 1)
    n_pairs = jnp.maximum(nt - 1, 0) // 2 - jnp.where(half & pair_ends, 1, 0)

    def body(p, carry):
        step(t0 + 1 + 2 * p, compute, 2)
        return carry

    lax.fori_loop(0, n_pairs, body, 0)

    @pl.when(half & pair_ends)
    def _():
        step(t0 + nt - 2, compute, 2, half_last=True)

    @pl.when(lone_last & jnp.logical_not(half))
    def _():
        step(t0 + nt - 1, compute, 1)

    @pl.when(lone_last & half)
    def _():
        step(t0 + nt - 1, compute, 1, half_last=True)

    @pl.when((j == nj - 1) & (e == ne - 1))
    def _():
        for k in range(n_out):
            @pl.when(n_seq > k)
            def _():
                o_copy(0, lax.rem(n_seq - 1 - k, n_out)).wait()


def _swiglu(g, u):
    g = jnp.minimum(g, SWIGLU_LIMIT)
    u = jnp.clip(u, -SWIGLU_LIMIT, SWIGLU_LIMIT)
    return (u + 1.0) * g * jax.nn.sigmoid(SWIGLU_ALPHA * g)


def _moe_up_kernel(first_ref, ntile_ref, eff_ref, ntot_ref, half_ref, x_hbm, wg_ref, wu_ref, bg_ref, bu_ref,
                   act_hbm, xbuf, obuf, wg_bf, wu_bf, sem_in, sem_out):
    kc = wg_ref.shape[0] // MOE_CAST_CHUNKS

    def compute_first(x_words):
        x = pltpu.bitcast(x_words, BF16)
        g = bg_ref[...]
        u = bu_ref[...]
        for c in range(MOE_CAST_CHUNKS):
            ks = slice(c * kc, (c + 1) * kc)
            wg_c = wg_ref[ks, :].astype(BF16)
            wu_c = wu_ref[ks, :].astype(BF16)
            wg_bf[ks, :] = wg_c
            wu_bf[ks, :] = wu_c
            g = g + jnp.dot(x[:, ks], wg_c, preferred_element_type=F32)
            u = u + jnp.dot(x[:, ks], wu_c, preferred_element_type=F32)
        return _pair_rows(_swiglu(g, u).astype(BF16))

    def compute(x_words):
        x = pltpu.bitcast(x_words, BF16)
        g = jnp.dot(x, wg_bf[...], preferred_element_type=F32) + bg_ref[...]
        u = jnp.dot(x, wu_bf[...], preferred_element_type=F32) + bu_ref[...]
        return _pair_rows(_swiglu(g, u).astype(BF16))

    _stream_expert_tiles(first_ref, ntile_ref, ntot_ref, half_ref, x_hbm, act_hbm, xbuf, obuf, sem_in, sem_out,
                         compute_first, compute)


def _moe_up(xs, w_gate_up, b_gate_up, tile_first, tile_count, eff_expert, n_tiles, tail_half):
    p2, d = xs.shape
    n_exp, _, f2 = w_gate_up.shape
    f = f2 // 2
    tf = _tile(MOE_FF_TILE, f)
    nj = f // tf
    return pl.pallas_call(
        _moe_up_kernel,
        grid_spec=pltpu.PrefetchScalarGridSpec(
            num_scalar_prefetch=5,
            grid=(nj, n_exp),
            in_specs=[pl.BlockSpec(memory_space=pl.ANY),
                      pl.BlockSpec((None, d, tf), lambda j, e, fi, nt, ef, ntot, hf: (ef[e], 0, j)),
                      pl.BlockSpec((None, d, tf), lambda j, e, fi, nt, ef, ntot, hf: (ef[e], 0, nj + j)),
                      pl.BlockSpec((None, 1, tf), lambda j, e, fi, nt, ef, ntot, hf: (ef[e], 0, j)),
                      pl.BlockSpec((None, 1, tf), lambda j, e, fi, nt, ef, ntot, hf: (ef[e], 0, nj + j))],
            out_specs=pl.BlockSpec(memory_space=pl.ANY),
            scratch_shapes=[pltpu.VMEM((MOE_IN_SLOTS, MOE_ROWS // 2, d), U32),
                            pltpu.VMEM((MOE_OUT_SLOTS, MOE_ROWS // 2, tf), U32),
                            pltpu.VMEM((d, tf), BF16),
                            pltpu.VMEM((d, tf), BF16),
                            pltpu.SemaphoreType.DMA((MOE_IN_SLOTS,)),
                            pltpu.SemaphoreType.DMA((MOE_OUT_SLOTS,))]),
        out_shape=jax.ShapeDtypeStruct((p2, f), U32),
        compiler_params=_params(("arbitrary", "arbitrary")),
        name="moe_up",
    )(tile_first, tile_count, eff_expert, n_tiles, tail_half, xs, w_gate_up, w_gate_up,
      b_gate_up.reshape(n_exp, 1, f2), b_gate_up.reshape(n_exp, 1, f2))


def _moe_down_kernel(first_ref, ntile_ref, eff_ref, ntot_ref, half_ref, a_hbm, w_ref, b_ref, y_hbm,
                     xbuf, obuf, w_bf, sem_in, sem_out):
    kc = w_ref.shape[0] // MOE_CAST_CHUNKS

    def compute_first(a_words):
        a = pltpu.bitcast(a_words, BF16)
        y = b_ref[...]
        for c in range(MOE_CAST_CHUNKS):
            ks = slice(c * kc, (c + 1) * kc)
            w_c = w_ref[ks, :].astype(BF16)
            w_bf[ks, :] = w_c
            y = y + jnp.dot(a[:, ks], w_c, preferred_element_type=F32)
        return y

    def compute(a_words):
        return jnp.dot(pltpu.bitcast(a_words, BF16), w_bf[...], preferred_element_type=F32) + b_ref[...]

    _stream_expert_tiles(first_ref, ntile_ref, ntot_ref, half_ref, a_hbm, y_hbm, xbuf, obuf, sem_in, sem_out,
                         compute_first, compute)


def _moe_down(act, w_down, b_down, tile_first, tile_count, eff_expert, n_tiles, tail_half):
    p2, f = act.shape
    n_exp, _, d = w_down.shape
    tn = _tile(MOE_OUT_TILE, d)
    return pl.pallas_call(
        _moe_down_kernel,
        grid_spec=pltpu.PrefetchScalarGridSpec(
            num_scalar_prefetch=5,
            grid=(d // tn, n_exp),
            in_specs=[pl.BlockSpec(memory_space=pl.ANY),
                      pl.BlockSpec((None, f, tn), lambda j, e, fi, nt, ef, ntot, hf: (ef[e], 0, j)),
                      pl.BlockSpec((None, 1, tn), lambda j, e, fi, nt, ef, ntot, hf: (ef[e], 0, j))],
            out_specs=pl.BlockSpec(memory_space=pl.ANY),
            scratch_shapes=[pltpu.VMEM((MOE_IN_SLOTS, MOE_ROWS // 2, f), U32),
                            pltpu.VMEM((MOE_OUT_SLOTS, MOE_ROWS, tn), F32),
                            pltpu.VMEM((f, tn), BF16),
                            pltpu.SemaphoreType.DMA((MOE_IN_SLOTS,)),
                            pltpu.SemaphoreType.DMA((MOE_OUT_SLOTS,))]),
        out_shape=jax.ShapeDtypeStruct((2 * p2, d), F32),
        compiler_params=_params(("arbitrary", "arbitrary")),
        name="moe_down",
    )(tile_first, tile_count, eff_expert, n_tiles, tail_half, act, w_down, b_down.reshape(n_exp, 1, d))


def _combine_kernel(pos0_ref, posn_ref, y_ref, gate_ref, x_ref, g2_ref, gf_ref, o_ref,
                    pos_smem, buf, sem_idx, sem_rows):
    bb, tt, d = x_ref.shape
    rows = bb * tt
    i = pl.program_id(0)

    @pl.when(i == 0)
    def _():
        _issue_tile(jnp.int32(0), 0, pos0_ref, posn_ref, pos_smem, sem_idx, y_ref, buf.at[0], sem_rows.at[0])

    @pl.when(i + 1 < pl.num_programs(0))
    def _():
        nxt = (i + 1) % 2
        _issue_tile(i + 1, 0, pos0_ref, posn_ref, pos_smem, sem_idx, y_ref, buf.at[nxt], sem_rows.at[nxt])

    slot = i % 2
    _wait_row_copies(y_ref, buf.at[slot], sem_rows.at[slot])
    gates = gate_ref[...]
    ffn = gates[:, 0:1] * buf[slot, 0:rows]
    for k in range(1, TOP_K):
        ffn = ffn + gates[:, k:k + 1] * buf[slot, k * rows:(k + 1) * rows]
    x2 = x_ref[...] + g2_ref[...] * ffn.reshape(bb, tt, d)
    o_ref[...] = _rms(x2) * gf_ref[...]


def _combine(y_sorted, pos, gates, x1, mod, gf, tok_offset):
    b, t, d = x1.shape
    bb, tt = _token_blocks(b, t, COMBINE_ROWS)
    rows = bb * tt
    assert tok_offset % rows == 0
    blk0 = tok_offset // rows
    n_tok = pos.shape[0]
    nblk = (b // bb) * (t // tt)
    ts = t // tt
    pos_tiles = pos.reshape(n_tok // rows, rows, TOP_K).transpose(0, 2, 1)[blk0:blk0 + nblk]
    batch = _tile(INDEX_BATCH_TILES, nblk)
    pos_batches = pos_tiles.reshape(nblk // batch, 1, batch * TOP_K * rows)
    return pl.pallas_call(
        _combine_kernel,
        grid=(nblk,),
        in_specs=[pl.BlockSpec((1, 1, batch * TOP_K * rows), lambda i: (0, 0, 0)),
                  pl.BlockSpec((1, 1, batch * TOP_K * rows), lambda i: (jnp.minimum(i + 1, nblk - 1) // batch, 0, 0)),
                  pl.BlockSpec(memory_space=pl.ANY),
                  pl.BlockSpec((rows, TOP_K), lambda i: (blk0 + i, 0)),
                  pl.BlockSpec((bb, tt, d), lambda i: (i // ts, i % ts, 0)),
                  pl.BlockSpec((bb, 1, d), lambda i: (i // ts, 0, 5)),
                  pl.BlockSpec((1, d), lambda i: (0, 0))],
        out_specs=pl.BlockSpec((bb, tt, d), lambda i: (i // ts, i % ts, 0)),
        out_shape=jax.ShapeDtypeStruct((b, t, d), F32),
        scratch_shapes=[pltpu.SMEM((batch * TOP_K * rows,), jnp.int32),
                        pltpu.VMEM((2, TOP_K * rows, d), F32),
                        pltpu.SemaphoreType.DMA(()),
                        pltpu.SemaphoreType.DMA((2,))],
        compiler_params=_params(("arbitrary",)),
        name="moe_combine",
    )(pos_batches, pos_batches, y_sorted, gates, x1, mod, gf.reshape(1, d))


def _routing(top_idx, rank, counts, rows):
    n_tok = top_idx.shape[0]
    n_exp = counts.shape[0]
    n_assign = n_tok * TOP_K
    tile_count = (counts + rows - 1) // rows
    tile_end = jnp.cumsum(tile_count)
    tile_first = tile_end - tile_count
    pos = (tile_first * rows)[top_idx] + rank
    p = n_assign + n_exp * rows
    token_of = jnp.arange(n_assign, dtype=jnp.int32) // TOP_K
    row_token = jnp.zeros((p,), jnp.int32).at[pos.reshape(-1)].set(
        token_of, unique_indices=True, mode="promise_in_bounds")
    ids = jnp.where(tile_count > 0, jnp.arange(n_exp, dtype=jnp.int32), -1)
    eff = lax.cummax(ids)
    eff = jnp.where(eff < 0, jnp.argmax(tile_count > 0).astype(jnp.int32), eff)
    i32 = lambda a: a.astype(jnp.int32)
    tail_half = (tile_count > 1) & (counts - (tile_count - 1) * rows <= rows // 2)
    return i32(pos), row_token, i32(tile_first), i32(tile_count), i32(eff), i32(tile_end[-1:]), i32(tail_half)


def kernel(x_prompt, x_sample, cache_k, cache_v, state_conv, c_prompt, c_sample, w_ada, b_ada, norm_mix_g, norm_ffn_g, norm_final_g, w_in, b_in, conv_w, conv_b, sinks, out_norm_conv_g, out_norm_attn_g, w_out, b_out, w_router, b_router, w_gate_up, b_gate_up, w_down, b_down):
    d = x_prompt.shape[2]
    depth, dec_b, window, n_kv, head_dim = cache_k.shape
    assert depth == 1
    conv_ch = conv_w.shape[1]
    n_exp = w_router.shape[1]
    n_heads = sinks.shape[0]
    bp, tp, _ = x_prompt.shape
    bs, ts, _ = x_sample.shape
    col_q = 3 * conv_ch
    kv_w = n_kv * head_dim
    att_w = n_heads * head_dim

    mod = _adaln(jnp.concatenate([c_prompt, c_sample], axis=0), w_ada, b_ada)
    mod_p = mod[:bp].reshape(bp, 1, 6 * d)
    mod_s = mod[bp:].reshape(bs, 1, 6 * d)
    order = _head_order(n_heads, n_kv)
    w_in_bf = w_in.astype(BF16)
    w_in_bf = lax.dynamic_update_slice(
        w_in_bf, _permute_blocks(w_in_bf[:, col_q:col_q + att_w], order, head_dim, 1, 0), (0, col_q))
    b_in_p = _permute_blocks(b_in, order, head_dim, 0, col_q)
    w_out_bf = w_out.astype(BF16)
    w_out_bf = lax.dynamic_update_slice(
        w_out_bf, _permute_blocks(w_out_bf[conv_ch:], order, head_dim, 0, 0), (conv_ch, 0))
    attn_g = _permute_blocks(out_norm_attn_g, order, head_dim, 0, 0)
    wr_hi = w_router.astype(BF16)
    wr_lo = (w_router - wr_hi.astype(F32)).astype(BF16)
    n_p, n_s = bp * tp, bs * ts

    def mixer(x, modx, conv_prev, ck, cv, counts_before, hp_prev, tok_offset):
        b, t, _ = x.shape
        proj = _inproj(x, modx, norm_mix_g, w_in_bf, b_in_p)
        yc, conv_state = _conv_mix(proj, conv_prev, conv_w, conv_b, out_norm_conv_g)
        if ck is None:
            ya, k_rot = _swa_banded(proj, sinks, attn_g, n_kv, head_dim, window, col_q)
            keep = window
        else:
            ya, k_rot = _swa_cached(proj, ck, cv, sinks, attn_g, n_kv, head_dim, col_q)
            keep = t
        k_rows = k_rot[:, t - keep:].reshape(1, b, keep, n_kv, head_dim)
        v_rows = proj[:, t - keep:, col_q + att_w + kv_w:].reshape(1, b, keep, n_kv, head_dim)
        x1 = _outproj(yc, ya, w_out_bf, b_out, x, modx)
        routed = _router(x1, modx, norm_ffn_g, wr_hi, wr_lo, b_router, counts_before, hp_prev,
                         tok_offset, n_p + n_s)
        return x1, routed, conv_state, k_rows, v_rows

    zero_state = jnp.zeros((bp, 2, conv_ch), F32)
    x1p, (hp_buf, idxp, gatep, rankp, cntp), conv_p, k_prompt, v_prompt = mixer(
        x_prompt, mod_p, zero_state, None, None, jnp.zeros((1, n_exp), F32), None, 0)
    x1s, (h2_packed, idxs, gates_s, ranks, cnt), conv_s, k_sample, v_sample = mixer(
        x_sample, mod_s, state_conv[0],
        cache_k[0].reshape(dec_b, window, kv_w), cache_v[0].reshape(dec_b, window, kv_w), cntp, hp_buf, n_p)

    top_idx = jnp.concatenate([idxp, idxs], axis=0)
    gates = jnp.concatenate([gatep, gates_s], axis=0)
    rank = jnp.concatenate([rankp, ranks], axis=0)
    pos, row_token, tile_first, tile_count, eff, n_tiles, tail_half = _routing(
        top_idx, rank, cnt[0].astype(jnp.int32), MOE_ROWS)
    xs = _gather_rows(h2_packed, row_token, n_tiles, MOE_ROWS)
    act = _moe_up(xs, w_gate_up, b_gate_up, tile_first, tile_count, eff, n_tiles, tail_half)
    y_sorted = _moe_down(act, w_down, b_down, tile_first, tile_count, eff, n_tiles, tail_half)
    y_prompt = _combine(y_sorted, pos, gates, x1p, mod_p, norm_final_g, 0)
    y_sample = _combine(y_sorted, pos, gates, x1s, mod_s, norm_final_g, n_p)
    return (y_prompt, y_sample, k_prompt, v_prompt, conv_p[None], k_sample, v_sample, conv_s[None])
```

```python
import functools

import jax
import jax.numpy as jnp
from jax import lax
from jax.experimental import pallas as pl
from jax.experimental.pallas import tpu as pltpu

F32 = jnp.float32
BF16 = jnp.bfloat16
U32 = jnp.uint32

CHUNK = 64
TOP_K = 4
PAST_LEN = 2048
ROPE_THETA = 10000.0
NORM_EPS = 1e-5
SWIGLU_LIMIT = 7.0
SWIGLU_ALPHA = 1.702

LANES = 128
SUBLANES = 8
VMEM_LIMIT_BYTES = 58 * 1024 * 1024

ROW_TILE = 512
ADA_COL_TILE = 512
INPROJ_COL_TILE = 1024
OUTPROJ_COL_TILE = 1024
CONV_ROW_TILE = 256
ATTN_CHUNKS_PER_STEP = 4
FFN_ROW_TILE = 256
MOE_ROWS = 256
MOE_FF_TILE = 512
MOE_OUT_TILE = 1024
MOE_IN_SLOTS = 5
MOE_OUT_SLOTS = 4
MOE_CAST_CHUNKS = 4
COMBINE_ROWS = 128
ROW_COPY_UNROLL = 8
INDEX_BATCH_TILES = 8
ROW_COPY_SLOTS = 3
TILE_COPY_PRIORITY = 1


def _tile(pref, dim):
    t = min(pref, dim)
    while dim % t:
        t -= 1
    return t


def _params(sem):
    return pltpu.CompilerParams(dimension_semantics=sem, vmem_limit_bytes=VMEM_LIMIT_BYTES)


def _rms(x):
    return x * lax.rsqrt(jnp.mean(x * x, axis=-1, keepdims=True) + NORM_EPS)


def _token_blocks(b, t, rows_pref):
    if t >= rows_pref:
        return 1, _tile(rows_pref, t)
    return _tile(max(rows_pref // t, 1), b), t


def _adaln_kernel(c_ref, w_ref, b_ref, o_ref):
    c = c_ref[...]
    a = (c * jax.nn.sigmoid(c)).astype(BF16)
    o_ref[...] = jnp.dot(a, w_ref[...].astype(BF16), preferred_element_type=F32) + b_ref[...]


def _adaln(c, w_ada, b_ada):
    nb, d = c.shape
    n = w_ada.shape[1]
    tn = _tile(ADA_COL_TILE, n)
    return pl.pallas_call(
        _adaln_kernel,
        grid=(n // tn,),
        in_specs=[pl.BlockSpec((nb, d), lambda j: (0, 0)),
                  pl.BlockSpec((d, tn), lambda j: (0, j)),
                  pl.BlockSpec((1, tn), lambda j: (0, j))],
        out_specs=pl.BlockSpec((nb, tn), lambda j: (0, j)),
        out_shape=jax.ShapeDtypeStruct((nb, n), F32),
        compiler_params=_params(("arbitrary",)),
        name="adaln",
    )(c, w_ada, b_ada.reshape(1, n))


def _inproj_kernel(x_ref, sh_ref, sc_ref, g_ref, w_ref, b_ref, o_ref, h_ref):
    bb, tt, d = x_ref.shape

    @pl.when(pl.program_id(2) == 0)
    def _():
        y = _rms(x_ref[...]) * g_ref[...]
        h = y * (1.0 + sc_ref[...]) + sh_ref[...]
        h_ref[...] = h.reshape(bb * tt, d).astype(BF16)

    acc = jnp.dot(h_ref[...], w_ref[...], preferred_element_type=F32) + b_ref[...]
    o_ref[...] = acc.reshape(o_ref.shape)


def _inproj(x, mod, g, w_bf, b_in):
    b, t, d = x.shape
    n = w_bf.shape[1]
    bb, tt = _token_blocks(b, t, ROW_TILE)
    tn = _tile(INPROJ_COL_TILE, n)
    return pl.pallas_call(
        _inproj_kernel,
        grid=(b // bb, t // tt, n // tn),
        in_specs=[pl.BlockSpec((bb, tt, d), lambda i, s, j: (i, s, 0)),
                  pl.BlockSpec((bb, 1, d), lambda i, s, j: (i, 0, 0)),
                  pl.BlockSpec((bb, 1, d), lambda i, s, j: (i, 0, 1)),
                  pl.BlockSpec((1, d), lambda i, s, j: (0, 0)),
                  pl.BlockSpec((d, tn), lambda i, s, j: (0, j)),
                  pl.BlockSpec((1, tn), lambda i, s, j: (0, j))],
        out_specs=pl.BlockSpec((bb, tt, tn), lambda i, s, j: (i, s, j)),
        out_shape=jax.ShapeDtypeStruct((b, t, n), F32),
        scratch_shapes=[pltpu.VMEM((bb * tt, d), BF16)],
        compiler_params=_params(("arbitrary", "arbitrary", "arbitrary")),
        name="inproj",
    )(x, mod, mod, g.reshape(1, d), w_bf, b_in.reshape(1, n))


def _conv_kernel(bg_ref, cg_ref, xc_ref, cgp_ref, xcp_ref, st_ref, w_ref, b_ref, g_ref, y_ref, so_ref):
    tt = cg_ref.shape[0]
    u = cg_ref[...] * xc_ref[...]
    halo = cgp_ref[...] * xcp_ref[...]
    st = st_ref[...]
    first = pl.program_id(1) == 0
    um2 = jnp.where(first, st[0:1], halo[SUBLANES - 2:SUBLANES - 1])
    um1 = jnp.where(first, st[1:2], halo[SUBLANES - 1:SUBLANES])
    row = lax.broadcasted_iota(jnp.int32, u.shape, 0)
    u1 = jnp.where(row == 0, um1, pltpu.roll(u, 1, 0))
    u2 = jnp.where(row == 0, um2, jnp.where(row == 1, um1, pltpu.roll(u, 2, 0)))
    w = w_ref[...]
    conv = b_ref[...] + u2 * w[0:1]
    conv = conv + u1 * w[1:2]
    conv = conv + u * w[2:3]
    y = _rms(bg_ref[...] * conv) * g_ref[...]
    y_ref[...] = y.astype(BF16)
    so_ref[...] = u[tt - 2:tt]


def _conv_mix(proj, state, conv_w, conv_b, g):
    b, t, _ = proj.shape
    c = conv_w.shape[1]
    assert conv_w.shape[0] == 3
    tt = _tile(CONV_ROW_TILE, t)
    hs = tt // SUBLANES
    halo_map = lambda col: (lambda i, s: (i, jnp.maximum(s * hs - 1, 0), col))
    return pl.pallas_call(
        _conv_kernel,
        grid=(b, t // tt),
        in_specs=[pl.BlockSpec((None, tt, c), lambda i, s: (i, s, 0)),
                  pl.BlockSpec((None, tt, c), lambda i, s: (i, s, 1)),
                  pl.BlockSpec((None, tt, c), lambda i, s: (i, s, 2)),
                  pl.BlockSpec((None, SUBLANES, c), halo_map(1)),
                  pl.BlockSpec((None, SUBLANES, c), halo_map(2)),
                  pl.BlockSpec((None, 2, c), lambda i, s: (i, 0, 0)),
                  pl.BlockSpec((3, c), lambda i, s: (0, 0)),
                  pl.BlockSpec((1, c), lambda i, s: (0, 0)),
                  pl.BlockSpec((1, c), lambda i, s: (0, 0))],
        out_specs=[pl.BlockSpec((None, tt, c), lambda i, s: (i, s, 0)),
                   pl.BlockSpec((None, 2, c), lambda i, s: (i, 0, 0))],
        out_shape=[jax.ShapeDtypeStruct((b, t, c), BF16),
                   jax.ShapeDtypeStruct((b, 2, c), F32)],
        compiler_params=_params(("arbitrary", "arbitrary")),
        name="conv_mix",
    )(proj, proj, proj, proj, proj, state, conv_w, conv_b.reshape(1, c), g.reshape(1, c))


def _head_order(n_heads, n_kv):
    grp = n_heads // n_kv
    assert n_kv % 2 == 0
    return [(2 * p + half) * grp + g for p in range(n_kv // 2) for g in range(grp) for half in (0, 1)]


def _permute_blocks(x, order, width, axis, start):
    n = len(order) * width
    take = lambda lo, hi: lax.slice_in_dim(x, lo, hi, axis=axis)
    parts = [take(0, start)] + [take(start + h * width, start + (h + 1) * width) for h in order]
    parts.append(take(start + n, x.shape[axis]))
    return jnp.concatenate(parts, axis=axis)


def _rope_tables(pos, head_dim):
    half = head_dim // 2
    inv_freq = ROPE_THETA ** (-jnp.arange(half, dtype=F32) / half)
    ang = pos.astype(F32)[:, None] * inv_freq[None, :]
    cos, sin = jnp.cos(ang), jnp.sin(ang)
    reps = LANES // head_dim
    return (jnp.tile(jnp.concatenate([cos, cos], axis=-1), (1, reps)),
            jnp.tile(jnp.concatenate([-sin, sin], axis=-1), (1, reps)))


def _rope(x, cos, sin, head_dim):
    width = x.shape[1]
    half = head_dim // 2
    reps = width // LANES
    cos = jnp.concatenate([cos] * reps, axis=1)
    sin = jnp.concatenate([sin] * reps, axis=1)
    lane = lax.broadcasted_iota(jnp.int32, x.shape, 1)
    swapped = jnp.where(lane % head_dim < half,
                        pltpu.roll(x, width - half, 1), pltpu.roll(x, half, 1))
    return x * cos + swapped * sin


def _pair_blocks(slab):
    keys = slab.shape[0]
    lo = lax.broadcasted_iota(jnp.int32, (CHUNK, LANES), 1) < LANES // 2
    zero = jnp.zeros((CHUNK, LANES), slab.dtype)
    parts = []
    for c in range(keys // CHUNK):
        blk = slab[c * CHUNK:(c + 1) * CHUNK]
        parts += [jnp.where(lo, blk, zero), jnp.where(lo, zero, blk)]
    return jnp.concatenate(parts, axis=0)


def _attend_pair(qs, kblk, vblk, sink_vec, key_ok):
    head_dim = LANES // 2
    n_chunks = kblk.shape[0] // LANES
    lo = lax.broadcasted_iota(jnp.int32, (1, LANES), 1) < head_dim
    s = lax.dot_general(qs.astype(BF16), kblk, (((1,), (1,)), ((), ())),
                        preferred_element_type=F32) * (head_dim ** -0.5)
    tiles = [jnp.where(key_ok[c], s[:, c * LANES:(c + 1) * LANES], -jnp.inf) for c in range(n_chunks)]

    def per_half(t, reduce_fn, fill):
        a = reduce_fn(jnp.where(lo, t, fill), axis=-1, keepdims=True)
        b = reduce_fn(jnp.where(lo, fill, t), axis=-1, keepdims=True)
        return jnp.where(lo, a, b)

    tmax = tiles[0]
    for t in tiles[1:]:
        tmax = jnp.maximum(tmax, t)
    m = jnp.maximum(per_half(tmax, jnp.max, -jnp.inf), sink_vec)
    es = [jnp.exp(t - m) for t in tiles]
    esum = es[0]
    for e in es[1:]:
        esum = esum + e
    den = per_half(esum, jnp.sum, 0.0) + jnp.exp(sink_vec - m)
    e_all = jnp.concatenate([e.astype(BF16) for e in es], axis=1)
    return jnp.dot(e_all, vblk, preferred_element_type=F32) / den


def _attend(q, kslabs, vslabs, sinks_ref, key_ok, grp):
    rows = q.shape[0]
    head_dim = LANES // 2
    lo = lax.broadcasted_iota(jnp.int32, (1, LANES), 1) < head_dim
    outs = []
    for p in range(len(kslabs)):
        qs = jnp.concatenate([q[:, (p * grp + g) * LANES:(p * grp + g + 1) * LANES] for g in range(grp)], axis=0)
        sink_vec = jnp.concatenate(
            [jnp.broadcast_to(jnp.where(lo, sinks_ref[2 * p * grp + g], sinks_ref[(2 * p + 1) * grp + g]),
                              (rows, LANES)) for g in range(grp)], axis=0)
        o = _attend_pair(qs, kslabs[p], vslabs[p], sink_vec, key_ok)
        outs.extend(o[g * rows:(g + 1) * rows] for g in range(grp))
    return jnp.concatenate(outs, axis=1)


def _swa_banded_kernel(sinks_ref, q_ref, k_ref, v_ref, cos_ref, sin_ref, g_ref,
                       y_ref, kr_ref, kbuf, vbuf, *, grp, window):
    rb = q_ref.shape[0]
    head_dim = LANES // 2
    n_pairs = kbuf.shape[1] // LANES
    step = pl.program_id(1)

    @pl.when(step == 0)
    def _():
        kbuf[0:window] = jnp.zeros((window, kbuf.shape[1]), BF16)
        vbuf[0:window] = jnp.zeros((window, vbuf.shape[1]), BF16)

    @pl.when(step > 0)
    def _():
        kbuf[0:window] = kbuf[rb:rb + window]
        vbuf[0:window] = vbuf[rb:rb + window]

    cos, sin = cos_ref[...], sin_ref[...]
    k = _rope(k_ref[...], cos, sin, head_dim)
    kr_ref[...] = k
    kbuf[window:window + rb] = k.astype(BF16)
    vbuf[window:window + rb] = v_ref[...].astype(BF16)
    q = _rope(q_ref[...], cos, sin, head_dim)

    n_buf_chunks = (window + rb) // CHUNK
    kblocks = [[_pair_blocks(kbuf[c * CHUNK:(c + 1) * CHUNK, p * LANES:(p + 1) * LANES])
                for c in range(n_buf_chunks)] for p in range(n_pairs)]
    vblocks = [[_pair_blocks(vbuf[c * CHUNK:(c + 1) * CHUNK, p * LANES:(p + 1) * LANES])
                for c in range(n_buf_chunks)] for p in range(n_pairs)]
    band_chunks = window // CHUNK + 1
    lane_key = lax.broadcasted_iota(jnp.int32, (1, LANES), 1) % CHUNK
    for c in range(rb // CHUNK):
        lo_row = c * CHUNK
        key_ok = [step * rb + (c + b) * CHUNK + lane_key >= window for b in range(band_chunks)]
        ks = [jnp.concatenate(kblocks[p][c:c + band_chunks], axis=0) for p in range(n_pairs)]
        vs = [jnp.concatenate(vblocks[p][c:c + band_chunks], axis=0) for p in range(n_pairs)]
        y = _attend(q[lo_row:lo_row + CHUNK], ks, vs, sinks_ref, key_ok, grp)
        y_ref[lo_row:lo_row + CHUNK] = (_rms(y) * g_ref[...]).astype(BF16)


def _swa_banded(proj, sinks, g, n_kv, head_dim, window, col_q):
    b, t, _ = proj.shape
    assert 2 * head_dim == LANES and CHUNK == head_dim
    n_heads = sinks.shape[0]
    att_w = n_heads * head_dim
    kv_w = n_kv * head_dim
    rb = _tile(ATTN_CHUNKS_PER_STEP * CHUNK, t)
    assert window % CHUNK == 0 and t % CHUNK == 0 and rb >= window
    cos, sin = _rope_tables(jnp.arange(t, dtype=jnp.int32), head_dim)
    col_k = (col_q + att_w) // kv_w
    kern = functools.partial(_swa_banded_kernel, grp=n_heads // n_kv, window=window)
    return pl.pallas_call(
        kern,
        grid=(b, t // rb),
        in_specs=[pl.BlockSpec(memory_space=pltpu.SMEM),
                  pl.BlockSpec((None, rb, att_w), lambda i, s: (i, s, col_q // att_w)),
                  pl.BlockSpec((None, rb, kv_w), lambda i, s: (i, s, col_k)),
                  pl.BlockSpec((None, rb, kv_w), lambda i, s: (i, s, col_k + 1)),
                  pl.BlockSpec((rb, LANES), lambda i, s: (s, 0)),
                  pl.BlockSpec((rb, LANES), lambda i, s: (s, 0)),
                  pl.BlockSpec((1, att_w), lambda i, s: (0, 0))],
        out_specs=[pl.BlockSpec((None, rb, att_w), lambda i, s: (i, s, 0)),
                   pl.BlockSpec((None, rb, kv_w), lambda i, s: (i, s, 0))],
        out_shape=[jax.ShapeDtypeStruct((b, t, att_w), BF16),
                   jax.ShapeDtypeStruct((b, t, kv_w), F32)],
        scratch_shapes=[pltpu.VMEM((window + rb, kv_w), BF16),
                        pltpu.VMEM((window + rb, kv_w), BF16)],
        compiler_params=_params(("arbitrary", "arbitrary")),
        name="swa_banded",
    )(sinks, proj, proj, proj, cos, sin, g.reshape(1, att_w))


def _swa_cached_kernel(sinks_ref, q_ref, k_ref, v_ref, ck_ref, cv_ref, cos_ref, sin_ref, g_ref,
                       y_ref, kr_ref, *, grp):
    head_dim = LANES // 2
    t, kv_w = k_ref.shape
    window = ck_ref.shape[0]
    n_pairs = kv_w // LANES
    cos, sin = cos_ref[...], sin_ref[...]
    k = _rope(k_ref[...], cos, sin, head_dim)
    kr_ref[...] = k
    q = _rope(q_ref[...], cos, sin, head_dim)
    n_keys = window + t
    pad = -n_keys % CHUNK
    tail = [jnp.zeros((pad, kv_w), BF16)] if pad else []
    kb = jnp.concatenate([ck_ref[...].astype(BF16), k.astype(BF16)] + tail, axis=0)
    vb = jnp.concatenate([cv_ref[...].astype(BF16), v_ref[...].astype(BF16)] + tail, axis=0)
    ks = [_pair_blocks(kb[:, p * LANES:(p + 1) * LANES]) for p in range(n_pairs)]
    vs = [_pair_blocks(vb[:, p * LANES:(p + 1) * LANES]) for p in range(n_pairs)]
    lane_key = lax.broadcasted_iota(jnp.int32, (1, LANES), 1) % CHUNK
    key_ok = [c * CHUNK + lane_key < n_keys for c in range((n_keys + pad) // CHUNK)]
    y = _attend(q, ks, vs, sinks_ref, key_ok, grp)
    y_ref[...] = (_rms(y) * g_ref[...]).astype(BF16)


def _swa_cached(proj, cache_k, cache_v, sinks, g, n_kv, head_dim, col_q):
    b, t, _ = proj.shape
    assert 2 * head_dim == LANES and CHUNK == head_dim
    window = cache_k.shape[1]
    n_heads = sinks.shape[0]
    att_w = n_heads * head_dim
    kv_w = n_kv * head_dim
    cos, sin = _rope_tables(PAST_LEN + jnp.arange(t, dtype=jnp.int32), head_dim)
    col_k = (col_q + att_w) // kv_w
    kern = functools.partial(_swa_cached_kernel, grp=n_heads // n_kv)
    return pl.pallas_call(
        kern,
        grid=(b,),
        in_specs=[pl.BlockSpec(memory_space=pltpu.SMEM),
                  pl.BlockSpec((None, t, att_w), lambda i: (i, 0, col_q // att_w)),
                  pl.BlockSpec((None, t, kv_w), lambda i: (i, 0, col_k)),
                  pl.BlockSpec((None, t, kv_w), lambda i: (i, 0, col_k + 1)),
                  pl.BlockSpec((None, window, kv_w), lambda i: (i, 0, 0)),
                  pl.BlockSpec((None, window, kv_w), lambda i: (i, 0, 0)),
                  pl.BlockSpec((t, LANES), lambda i: (0, 0)),
                  pl.BlockSpec((t, LANES), lambda i: (0, 0)),
                  pl.BlockSpec((1, att_w), lambda i: (0, 0))],
        out_specs=[pl.BlockSpec((None, t, att_w), lambda i: (i, 0, 0)),
                   pl.BlockSpec((None, t, kv_w), lambda i: (i, 0, 0))],
        out_shape=[jax.ShapeDtypeStruct((b, t, att_w), BF16),
                   jax.ShapeDtypeStruct((b, t, kv_w), F32)],
        compiler_params=_params(("arbitrary",)),
        name="swa_cached",
    )(sinks, proj, proj, proj, cache_k, cache_v, cos, sin, g.reshape(1, att_w))


def _outproj_kernel(yc_ref, ya_ref, wt_ref, wb_ref, b_ref, x_ref, g1_ref, o_ref):
    bb, tt, c = yc_ref.shape
    acc = jnp.dot(yc_ref[...].reshape(bb * tt, c), wt_ref[...], preferred_element_type=F32)
    acc = acc + jnp.dot(ya_ref[...].reshape(bb * tt, c), wb_ref[...], preferred_element_type=F32)
    acc = acc + b_ref[...]
    o_ref[...] = x_ref[...] + g1_ref[...] * acc.reshape(o_ref.shape)


def _outproj(yc, ya, w_bf, b_out, x, mod):
    b, t, d = x.shape
    c = yc.shape[2]
    assert 2 * c == w_bf.shape[0]
    bb, tt = _token_blocks(b, t, ROW_TILE)
    tn = _tile(OUTPROJ_COL_TILE, d)
    return pl.pallas_call(
        _outproj_kernel,
        grid=(b // bb, t // tt, d // tn),
        in_specs=[pl.BlockSpec((bb, tt, c), lambda i, s, j: (i, s, 0)),
                  pl.BlockSpec((bb, tt, c), lambda i, s, j: (i, s, 0)),
                  pl.BlockSpec((c, tn), lambda i, s, j: (0, j)),
                  pl.BlockSpec((c, tn), lambda i, s, j: (1, j)),
                  pl.BlockSpec((1, tn), lambda i, s, j: (0, j)),
                  pl.BlockSpec((bb, tt, tn), lambda i, s, j: (i, s, j)),
                  pl.BlockSpec((bb, 1, tn), lambda i, s, j: (i, 0, 2 * (d // tn) + j))],
        out_specs=pl.BlockSpec((bb, tt, tn), lambda i, s, j: (i, s, j)),
        out_shape=jax.ShapeDtypeStruct((b, t, d), F32),
        compiler_params=_params(("arbitrary", "arbitrary", "arbitrary")),
        name="outproj",
    )(yc, ya, w_bf, w_bf, b_out.reshape(1, d), x, mod)


def _router_kernel(x_ref, sh_ref, sc_ref, g_ref, whi_ref, wlo_ref, br_ref, c0_ref, *rest, has_prev):
    hp_ref, idx_ref, gate_ref, rank_ref, cnt_ref, carry = rest[1:] if has_prev else rest
    bb, tt, d = x_ref.shape
    rows = bb * tt
    n_exp = whi_ref.shape[1]

    @pl.when((pl.program_id(0) == 0) & (pl.program_id(1) == 0))
    def _():
        carry[...] = c0_ref[...]

    y = _rms(x_ref[...]) * g_ref[...]
    h = (y * (1.0 + sc_ref[...]) + sh_ref[...]).reshape(rows, d)
    h_hi = h.astype(BF16)
    h_hi32 = h_hi.astype(F32)
    bits = lax.bitcast_convert_type(h_hi32, U32)
    hp_ref[...] = (bits[:, d // 2:] & jnp.uint32(0xFFFF0000)) | (bits[:, :d // 2] >> 16)

    h_lo = (h - h_hi32).astype(BF16)
    logits = jnp.dot(h_hi, wlo_ref[...], preferred_element_type=F32)
    logits = logits + jnp.dot(h_lo, whi_ref[...], preferred_element_type=F32)
    logits = logits + jnp.dot(h_hi, whi_ref[...], preferred_element_type=F32) + br_ref[...]
    col = lax.broadcasted_iota(jnp.int32, logits.shape, 1).astype(F32)
    work = logits
    vals, idxs = [], []
    for _ in range(TOP_K):
        m = jnp.max(work, axis=-1, keepdims=True)
        sel = jnp.min(jnp.where(work == m, col, float(n_exp)), axis=-1, keepdims=True)
        vals.append(m)
        idxs.append(sel)
        work = jnp.where(col == sel, -jnp.inf, work)
    exps = [jnp.exp(v - vals[0]) for v in vals]
    den = exps[0]
    for e in exps[1:]:
        den = den + e
    idx_ref[...] = jnp.concatenate(idxs, axis=1).astype(jnp.int32)
    gate_ref[...] = jnp.concatenate([e / den for e in exps], axis=1)

    onehots = [(col == sel).astype(F32) for sel in idxs]
    cnt = onehots[0]
    for oh in onehots[1:]:
        cnt = cnt + oh
    earlier = (lax.broadcasted_iota(jnp.int32, (rows, rows), 0)
               > lax.broadcasted_iota(jnp.int32, (rows, rows), 1)).astype(BF16)
    before = jnp.dot(earlier, cnt.astype(BF16), preferred_element_type=F32) + carry[...]
    ranks = [jnp.sum(oh * before, axis=-1, keepdims=True) for oh in onehots]
    rank_ref[...] = jnp.concatenate(ranks, axis=1).astype(jnp.int32)
    carry[...] = carry[...] + jnp.sum(cnt, axis=0, keepdims=True)
    cnt_ref[...] = carry[...]


def _router(x1, mod, g, w_hi, w_lo, b_router, counts_before, hp_prev, tok_offset, n_tok_all):
    b, t, d = x1.shape
    n_exp = w_hi.shape[1]
    bb, tt = _token_blocks(b, t, FFN_ROW_TILE)
    rows = bb * tt
    n_tok = b * t
    assert tok_offset % rows == 0
    blk0 = tok_offset // rows
    flat = lambda i, s: (i * (t // tt) + s, 0)
    has_prev = hp_prev is not None
    in_specs = [pl.BlockSpec((bb, tt, d), lambda i, s: (i, s, 0)),
                pl.BlockSpec((bb, 1, d), lambda i, s: (i, 0, 3)),
                pl.BlockSpec((bb, 1, d), lambda i, s: (i, 0, 4)),
                pl.BlockSpec((1, d), lambda i, s: (0, 0)),
                pl.BlockSpec((d, n_exp), lambda i, s: (0, 0)),
                pl.BlockSpec((d, n_exp), lambda i, s: (0, 0)),
                pl.BlockSpec((1, n_exp), lambda i, s: (0, 0)),
                pl.BlockSpec((1, n_exp), lambda i, s: (0, 0))]
    args = [x1, mod, mod, g.reshape(1, d), w_hi, w_lo, b_router.reshape(1, n_exp), counts_before]
    if has_prev:
        in_specs.append(pl.BlockSpec(memory_space=pl.ANY))
        args.append(hp_prev)
    return pl.pallas_call(
        functools.partial(_router_kernel, has_prev=has_prev),
        grid=(b // bb, t // tt),
        in_specs=in_specs,
        out_specs=[pl.BlockSpec((rows, d // 2), lambda i, s: (blk0 + i * (t // tt) + s, 0)),
                   pl.BlockSpec((rows, TOP_K), flat),
                   pl.BlockSpec((rows, TOP_K), flat),
                   pl.BlockSpec((rows, TOP_K), flat),
                   pl.BlockSpec((1, n_exp), lambda i, s: (0, 0))],
        out_shape=[jax.ShapeDtypeStruct((n_tok_all, d // 2), U32),
                   jax.ShapeDtypeStruct((n_tok, TOP_K), jnp.int32),
                   jax.ShapeDtypeStruct((n_tok, TOP_K), F32),
                   jax.ShapeDtypeStruct((n_tok, TOP_K), jnp.int32),
                   jax.ShapeDtypeStruct((1, n_exp), F32)],
        scratch_shapes=[pltpu.VMEM((1, n_exp), F32)],
        input_output_aliases={len(args) - 1: 0} if has_prev else {},
        compiler_params=_params(("arbitrary", "arbitrary")),
        name="router",
    )(*args)


def _stage_indices(idx_batch_ref, idx_smem, sem_idx):
    cp = pltpu.make_async_copy(idx_batch_ref.at[0, 0], idx_smem, sem_idx)
    cp.start()
    cp.wait()


def _issue_row_copies(idx_smem, base, src_ref, dst_ref, sem):
    def body(g, carry):
        for u in range(ROW_COPY_UNROLL):
            r = g * ROW_COPY_UNROLL + u
            pltpu.make_async_copy(src_ref.at[pl.ds(idx_smem[base + r], 1)], dst_ref.at[pl.ds(r, 1)], sem).start(
                priority=u % 2)
        return carry

    lax.fori_loop(0, dst_ref.shape[0] // ROW_COPY_UNROLL, body, 0)


def _issue_tile(tile, first_tile, idx0_ref, idxn_ref, idx_smem, sem_idx, src_ref, dst_ref, sem):
    n_rows = dst_ref.shape[0]
    batch = idx_smem.shape[0] // n_rows
    rel = tile - first_tile

    @pl.when(rel == 0)
    def _():
        _stage_indices(idx0_ref, idx_smem, sem_idx)

    @pl.when((rel > 0) & (lax.rem(rel, batch) == 0))
    def _():
        _stage_indices(idxn_ref, idx_smem, sem_idx)

    _issue_row_copies(idx_smem, lax.rem(rel, batch) * n_rows, src_ref, dst_ref, sem)


def _wait_row_copies(src_ref, dst_ref, sem):
    pltpu.make_async_copy(src_ref.at[pl.ds(0, dst_ref.shape[0])], dst_ref, sem).wait()


def _gather_kernel(nused_ref, idx0_ref, idxn_ref, src_ref, o_ref, idx_smem, buf, sem_idx, sem_rows):
    i = pl.program_id(0)
    n = nused_ref[0]
    half = src_ref.shape[1]
    n_slots = buf.shape[0]
    ahead = n_slots - 1

    @pl.when(i == 0)
    def _():
        for k in range(ahead):
            @pl.when(k < n)
            def _():
                _issue_tile(jnp.int32(k), 0, idx0_ref, idxn_ref, idx_smem, sem_idx, src_ref, buf.at[k],
                            sem_rows.at[k])

    @pl.when(i + ahead < n)
    def _():
        nxt = lax.rem(i + ahead, n_slots)
        _issue_tile(i + ahead, 0, idx0_ref, idxn_ref, idx_smem, sem_idx, src_ref, buf.at[nxt], sem_rows.at[nxt])

    @pl.when(i < n)
    def _():
        slot = lax.rem(i, n_slots)
        _wait_row_copies(src_ref, buf.at[slot], sem_rows.at[slot])
        w = buf[slot]
        o_ref[:, :half] = _pair_rows(lax.bitcast_convert_type(w << 16, F32).astype(BF16))
        o_ref[:, half:] = _pair_rows(lax.bitcast_convert_type(w & jnp.uint32(0xFFFF0000), F32).astype(BF16))


def _pair_rows(x):
    return pltpu.bitcast(x, U32)


def _gather_rows(src_packed, row_src, n_used, rows):
    p = row_src.shape[0]
    half = src_packed.shape[1]
    nblk = p // rows
    batch = _tile(INDEX_BATCH_TILES, nblk)
    ahead = ROW_COPY_SLOTS - 1
    assert batch >= ahead
    idx_batches = row_src.reshape(nblk // batch, 1, batch * rows)
    clamp = lambda i, nu: jnp.minimum(i, nu[0] - 1)
    return pl.pallas_call(
        _gather_kernel,
        grid_spec=pltpu.PrefetchScalarGridSpec(
            num_scalar_prefetch=1,
            grid=(nblk,),
            in_specs=[pl.BlockSpec((1, 1, batch * rows), lambda i, nu: (0, 0, 0)),
                      pl.BlockSpec((1, 1, batch * rows), lambda i, nu: (clamp(i + ahead, nu) // batch, 0, 0)),
                      pl.BlockSpec(memory_space=pl.ANY)],
            out_specs=pl.BlockSpec((rows // 2, 2 * half), lambda i, nu: (clamp(i, nu), 0)),
            scratch_shapes=[pltpu.SMEM((batch * rows,), jnp.int32),
                            pltpu.VMEM((ROW_COPY_SLOTS, rows, half), U32),
                            pltpu.SemaphoreType.DMA(()),
                            pltpu.SemaphoreType.DMA((ROW_COPY_SLOTS,))]),
        out_shape=jax.ShapeDtypeStruct((p // 2, 2 * half), U32),
        compiler_params=_params(("arbitrary",)),
        name="moe_gather",
    )(n_used, idx_batches, idx_batches, src_packed)


def _stream_expert_tiles(first_ref, ntile_ref, ntot_ref, half_ref, src_hbm, dst_hbm, xbuf, obuf, sem_in, sem_out,
                         compute_first, compute):
    j, e = pl.program_id(0), pl.program_id(1)
    nj, ne = pl.num_programs(0), pl.num_programs(1)
    n_slots = xbuf.shape[0]
    lookahead = n_slots - 2
    in_rows = xbuf.shape[1]
    n_out = obuf.shape[0]
    out_rows = obuf.shape[1]
    tn = obuf.shape[2]
    n_total = ntot_ref[0]
    n_seq = nj * n_total

    def x_copy(t, slot):
        r0 = pl.multiple_of(t * in_rows, in_rows)
        return pltpu.make_async_copy(src_hbm.at[pl.ds(r0, in_rows)], xbuf.at[slot], sem_in.at[slot])

    def x_start(seq):
        x_copy(lax.rem(seq, n_total), lax.rem(seq, n_slots)).start(priority=TILE_COPY_PRIORITY)

    def o_copy(t, slot):
        r0 = pl.multiple_of(t * out_rows, out_rows)
        c0 = pl.multiple_of(j * tn, tn)
        return pltpu.make_async_copy(obuf.at[slot], dst_hbm.at[pl.ds(r0, out_rows), pl.ds(c0, tn)],
                                     sem_out.at[slot])

    @pl.when((j == 0) & (e == 0))
    def _():
        for seq in range(lookahead):
            @pl.when(seq < n_seq)
            def _():
                x_start(jnp.int32(seq))

    def step(t, compute_fn, width, half_last=False):
        seq = j * n_total + t
        in_slots = [lax.rem(seq + k, n_slots) for k in range(width)]
        out_slots = [lax.rem(seq + k, n_out) for k in range(width)]
        for k in range(width):
            x_copy(0, in_slots[k]).wait()

        for k in range(width):
            @pl.when(seq + lookahead + k < n_seq)
            def _():
                x_start(seq + lookahead + k)

        for k in range(width):
            @pl.when(seq + k >= n_out)
            def _():
                o_copy(t, out_slots[k]).wait()

        for k in range(width):
            if half_last and k == width - 1:
                obuf[out_slots[k], 0:out_rows // 2] = compute_fn(xbuf[in_slots[k], 0:in_rows // 2])
            else:
                obuf[out_slots[k]] = compute_fn(xbuf[in_slots[k]])
        for k in range(width):
            o_copy(t + k, out_slots[k]).start()

    t0 = first_ref[e]
    nt = ntile_ref[e]

    @pl.when(nt > 0)
    def _():
        step(t0, compute_first, 1)

    half = (half_ref[e] > 0) & (nt > 1)
    pair_ends = (nt > 1) & (lax.rem(nt - 1, 2) == 0)
    lone_last = (nt > 1) & (lax.rem(nt - 1, 2) == 1)
    n_pairs = jnp.maximum(nt - 1, 0) // 2 - jnp.where(half & pair_ends, 1, 0)

    def body(p, carry):
        step(t0 + 1 + 2 * p, compute, 2)
        return carry

    lax.fori_loop(0, n_pairs, body, 0)

    @pl.when(half & pair_ends)
    def _():
        step(t0 + nt - 2, compute, 2, half_last=True)

    @pl.when(lone_last & jnp.logical_not(half))
    def _():
        step(t0 + nt - 1, compute, 1)

    @pl.when(lone_last & half)
    def _():
        step(t0 + nt - 1, compute, 1, half_last=True)

    @pl.when((j == nj - 1) & (e == ne - 1))
    def _():
        for k in range(n_out):
            @pl.when(n_seq > k)
            def _():
                o_copy(0, lax.rem(n_seq - 1 - k, n_out)).wait()


def _swiglu(g, u):
    g = jnp.minimum(g, SWIGLU_LIMIT)
    u = jnp.clip(u, -SWIGLU_LIMIT, SWIGLU_LIMIT)
    return (u + 1.0) * g * jax.nn.sigmoid(SWIGLU_ALPHA * g)


def _moe_up_kernel(first_ref, ntile_ref, eff_ref, ntot_ref, half_ref, x_hbm, wg_ref, wu_ref, bg_ref, bu_ref,
                   act_hbm, xbuf, obuf, wg_bf, wu_bf, sem_in, sem_out):
    kc = wg_ref.shape[0] // MOE_CAST_CHUNKS

    def compute_first(x_words):
        x = pltpu.bitcast(x_words, BF16)
        g = bg_ref[...]
        u = bu_ref[...]
        for c in range(MOE_CAST_CHUNKS):
            ks = slice(c * kc, (c + 1) * kc)
            wg_c = wg_ref[ks, :].astype(BF16)
            wu_c = wu_ref[ks, :].astype(BF16)
            wg_bf[ks, :] = wg_c
            wu_bf[ks, :] = wu_c
            g = g + jnp.dot(x[:, ks], wg_c, preferred_element_type=F32)
            u = u + jnp.dot(x[:, ks], wu_c, preferred_element_type=F32)
        return _pair_rows(_swiglu(g, u).astype(BF16))

    def compute(x_words):
        x = pltpu.bitcast(x_words, BF16)
        g = jnp.dot(x, wg_bf[...], preferred_element_type=F32) + bg_ref[...]
        u = jnp.dot(x, wu_bf[...], preferred_element_type=F32) + bu_ref[...]
        return _pair_rows(_swiglu(g, u).astype(BF16))

    _stream_expert_tiles(first_ref, ntile_ref, ntot_ref, half_ref, x_hbm, act_hbm, xbuf, obuf, sem_in, sem_out,
                         compute_first, compute)


def _moe_up(xs, w_gate_up, b_gate_up, tile_first, tile_count, eff_expert, n_tiles, tail_half):
    p2, d = xs.shape
    n_exp, _, f2 = w_gate_up.shape
    f = f2 // 2
    tf = _tile(MOE_FF_TILE, f)
    nj = f // tf
    return pl.pallas_call(
        _moe_up_kernel,
        grid_spec=pltpu.PrefetchScalarGridSpec(
            num_scalar_prefetch=5,
            grid=(nj, n_exp),
            in_specs=[pl.BlockSpec(memory_space=pl.ANY),
                      pl.BlockSpec((None, d, tf), lambda j, e, fi, nt, ef, ntot, hf: (ef[e], 0, j)),
                      pl.BlockSpec((None, d, tf), lambda j, e, fi, nt, ef, ntot, hf: (ef[e], 0, nj + j)),
                      pl.BlockSpec((None, 1, tf), lambda j, e, fi, nt, ef, ntot, hf: (ef[e], 0, j)),
                      pl.BlockSpec((None, 1, tf), lambda j, e, fi, nt, ef, ntot, hf: (ef[e], 0, nj + j))],
            out_specs=pl.BlockSpec(memory_space=pl.ANY),
            scratch_shapes=[pltpu.VMEM((MOE_IN_SLOTS, MOE_ROWS // 2, d), U32),
                            pltpu.VMEM((MOE_OUT_SLOTS, MOE_ROWS // 2, tf), U32),
                            pltpu.VMEM((d, tf), BF16),
                            pltpu.VMEM((d, tf), BF16),
                            pltpu.SemaphoreType.DMA((MOE_IN_SLOTS,)),
                            pltpu.SemaphoreType.DMA((MOE_OUT_SLOTS,))]),
        out_shape=jax.ShapeDtypeStruct((p2, f), U32),
        compiler_params=_params(("arbitrary", "arbitrary")),
        name="moe_up",
    )(tile_first, tile_count, eff_expert, n_tiles, tail_half, xs, w_gate_up, w_gate_up,
      b_gate_up.reshape(n_exp, 1, f2), b_gate_up.reshape(n_exp, 1, f2))


def _moe_down_kernel(first_ref, ntile_ref, eff_ref, ntot_ref, half_ref, a_hbm, w_ref, b_ref, y_hbm,
                     xbuf, obuf, w_bf, sem_in, sem_out):
    kc = w_ref.shape[0] // MOE_CAST_CHUNKS

    def compute_first(a_words):
        a = pltpu.bitcast(a_words, BF16)
        y = b_ref[...]
        for c in range(MOE_CAST_CHUNKS):
            ks = slice(c * kc, (c + 1) * kc)
            w_c = w_ref[ks, :].astype(BF16)
            w_bf[ks, :] = w_c
            y = y + jnp.dot(a[:, ks], w_c, preferred_element_type=F32)
        return y

    def compute(a_words):
        return jnp.dot(pltpu.bitcast(a_words, BF16), w_bf[...], preferred_element_type=F32) + b_ref[...]

    _stream_expert_tiles(first_ref, ntile_ref, ntot_ref, half_ref, a_hbm, y_hbm, xbuf, obuf, sem_in, sem_out,
                         compute_first, compute)


def _moe_down(act, w_down, b_down, tile_first, tile_count, eff_expert, n_tiles, tail_half):
    p2, f = act.shape
    n_exp, _, d = w_down.shape
    tn = _tile(MOE_OUT_TILE, d)
    return pl.pallas_call(
        _moe_down_kernel,
        grid_spec=pltpu.PrefetchScalarGridSpec(
            num_scalar_prefetch=5,
            grid=(d // tn, n_exp),
            in_specs=[pl.BlockSpec(memory_space=pl.ANY),
                      pl.BlockSpec((None, f, tn), lambda j, e, fi, nt, ef, ntot, hf: (ef[e], 0, j)),
                      pl.BlockSpec((None, 1, tn), lambda j, e, fi, nt, ef, ntot, hf: (ef[e], 0, j))],
            out_specs=pl.BlockSpec(memory_space=pl.ANY),
            scratch_shapes=[pltpu.VMEM((MOE_IN_SLOTS, MOE_ROWS // 2, f), U32),
                            pltpu.VMEM((MOE_OUT_SLOTS, MOE_ROWS, tn), F32),
                            pltpu.VMEM((f, tn), BF16),
                            pltpu.SemaphoreType.DMA((MOE_IN_SLOTS,)),
                            pltpu.SemaphoreType.DMA((MOE_OUT_SLOTS,))]),
        out_shape=jax.ShapeDtypeStruct((2 * p2, d), F32),
        compiler_params=_params(("arbitrary", "arbitrary")),
        name="moe_down",
    )(tile_first, tile_count, eff_expert, n_tiles, tail_half, act, w_down, b_down.reshape(n_exp, 1, d))


def _combine_kernel(pos0_ref, posn_ref, y_ref, gate_ref, x_ref, g2_ref, gf_ref, o_ref,
                    pos_smem, buf, sem_idx, sem_rows):
    bb, tt, d = x_ref.shape
    rows = bb * tt
    n_slots = buf.shape[0]
    ahead = n_slots - 1
    i = pl.program_id(0)
    n = pl.num_programs(0)

    @pl.when(i == 0)
    def _():
        for k in range(ahead):
            @pl.when(k < n)
            def _():
                _issue_tile(jnp.int32(k), 0, pos0_ref, posn_ref, pos_smem, sem_idx, y_ref, buf.at[k],
                            sem_rows.at[k])

    @pl.when(i + ahead < n)
    def _():
        nxt = lax.rem(i + ahead, n_slots)
        _issue_tile(i + ahead, 0, pos0_ref, posn_ref, pos_smem, sem_idx, y_ref, buf.at[nxt], sem_rows.at[nxt])

    slot = lax.rem(i, n_slots)
    _wait_row_copies(y_ref, buf.at[slot], sem_rows.at[slot])
    gates = gate_ref[...]
    ffn = gates[:, 0:1] * buf[slot, 0:rows]
    for k in range(1, TOP_K):
        ffn = ffn + gates[:, k:k + 1] * buf[slot, k * rows:(k + 1) * rows]
    x2 = x_ref[...] + g2_ref[...] * ffn.reshape(bb, tt, d)
    o_ref[...] = _rms(x2) * gf_ref[...]


def _combine(y_sorted, pos, gates, x1, mod, gf, tok_offset):
    b, t, d = x1.shape
    bb, tt = _token_blocks(b, t, COMBINE_ROWS)
    rows = bb * tt
    assert tok_offset % rows == 0
    blk0 = tok_offset // rows
    n_tok = pos.shape[0]
    nblk = (b // bb) * (t // tt)
    ts = t // tt
    pos_tiles = pos.reshape(n_tok // rows, rows, TOP_K).transpose(0, 2, 1)[blk0:blk0 + nblk]
    batch = _tile(INDEX_BATCH_TILES, nblk)
    ahead = ROW_COPY_SLOTS - 1
    assert batch >= ahead
    pos_batches = pos_tiles.reshape(nblk // batch, 1, batch * TOP_K * rows)
    return pl.pallas_call(
        _combine_kernel,
        grid=(nblk,),
        in_specs=[pl.BlockSpec((1, 1, batch * TOP_K * rows), lambda i: (0, 0, 0)),
                  pl.BlockSpec((1, 1, batch * TOP_K * rows),
                               lambda i: (jnp.minimum(i + ahead, nblk - 1) // batch, 0, 0)),
                  pl.BlockSpec(memory_space=pl.ANY),
                  pl.BlockSpec((rows, TOP_K), lambda i: (blk0 + i, 0)),
                  pl.BlockSpec((bb, tt, d), lambda i: (i // ts, i % ts, 0)),
                  pl.BlockSpec((bb, 1, d), lambda i: (i // ts, 0, 5)),
                  pl.BlockSpec((1, d), lambda i: (0, 0))],
        out_specs=pl.BlockSpec((bb, tt, d), lambda i: (i // ts, i % ts, 0)),
        out_shape=jax.ShapeDtypeStruct((b, t, d), F32),
        scratch_shapes=[pltpu.SMEM((batch * TOP_K * rows,), jnp.int32),
                        pltpu.VMEM((ROW_COPY_SLOTS, TOP_K * rows, d), F32),
                        pltpu.SemaphoreType.DMA(()),
                        pltpu.SemaphoreType.DMA((ROW_COPY_SLOTS,))],
        compiler_params=_params(("arbitrary",)),
        name="moe_combine",
    )(pos_batches, pos_batches, y_sorted, gates, x1, mod, gf.reshape(1, d))


def _routing(top_idx, rank, counts, rows):
    n_tok = top_idx.shape[0]
    n_exp = counts.shape[0]
    n_assign = n_tok * TOP_K
    tile_count = (counts + rows - 1) // rows
    tile_end = jnp.cumsum(tile_count)
    tile_first = tile_end - tile_count
    pos = (tile_first * rows)[top_idx] + rank
    p = n_assign + n_exp * rows
    token_of = jnp.arange(n_assign, dtype=jnp.int32) // TOP_K
    row_token = jnp.zeros((p,), jnp.int32).at[pos.reshape(-1)].set(
        token_of, unique_indices=True, mode="promise_in_bounds")
    ids = jnp.where(tile_count > 0, jnp.arange(n_exp, dtype=jnp.int32), -1)
    eff = lax.cummax(ids)
    eff = jnp.where(eff < 0, jnp.argmax(tile_count > 0).astype(jnp.int32), eff)
    i32 = lambda a: a.astype(jnp.int32)
    tail_half = (tile_count > 1) & (counts - (tile_count - 1) * rows <= rows // 2)
    return i32(pos), row_token, i32(tile_first), i32(tile_count), i32(eff), i32(tile_end[-1:]), i32(tail_half)


def kernel(x_prompt, x_sample, cache_k, cache_v, state_conv, c_prompt, c_sample, w_ada, b_ada, norm_mix_g, norm_ffn_g, norm_final_g, w_in, b_in, conv_w, conv_b, sinks, out_norm_conv_g, out_norm_attn_g, w_out, b_out, w_router, b_router, w_gate_up, b_gate_up, w_down, b_down):
    d = x_prompt.shape[2]
    depth, dec_b, window, n_kv, head_dim = cache_k.shape
    assert depth == 1
    conv_ch = conv_w.shape[1]
    n_exp = w_router.shape[1]
    n_heads = sinks.shape[0]
    bp, tp, _ = x_prompt.shape
    bs, ts, _ = x_sample.shape
    col_q = 3 * conv_ch
    kv_w = n_kv * head_dim
    att_w = n_heads * head_dim

    mod = _adaln(jnp.concatenate([c_prompt, c_sample], axis=0), w_ada, b_ada)
    mod_p = mod[:bp].reshape(bp, 1, 6 * d)
    mod_s = mod[bp:].reshape(bs, 1, 6 * d)
    order = _head_order(n_heads, n_kv)
    w_in_bf = w_in.astype(BF16)
    w_in_bf = lax.dynamic_update_slice(
        w_in_bf, _permute_blocks(w_in_bf[:, col_q:col_q + att_w], order, head_dim, 1, 0), (0, col_q))
    b_in_p = _permute_blocks(b_in, order, head_dim, 0, col_q)
    w_out_bf = w_out.astype(BF16)
    w_out_bf = lax.dynamic_update_slice(
        w_out_bf, _permute_blocks(w_out_bf[conv_ch:], order, head_dim, 0, 0), (conv_ch, 0))
    attn_g = _permute_blocks(out_norm_attn_g, order, head_dim, 0, 0)
    wr_hi = w_router.astype(BF16)
    wr_lo = (w_router - wr_hi.astype(F32)).astype(BF16)
    n_p, n_s = bp * tp, bs * ts

    def mixer(x, modx, conv_prev, ck, cv, counts_before, hp_prev, tok_offset):
        b, t, _ = x.shape
        proj = _inproj(x, modx, norm_mix_g, w_in_bf, b_in_p)
        yc, conv_state = _conv_mix(proj, conv_prev, conv_w, conv_b, out_norm_conv_g)
        if ck is None:
            ya, k_rot = _swa_banded(proj, sinks, attn_g, n_kv, head_dim, window, col_q)
            keep = window
        else:
            ya, k_rot = _swa_cached(proj, ck, cv, sinks, attn_g, n_kv, head_dim, col_q)
            keep = t
        k_rows = k_rot[:, t - keep:].reshape(1, b, keep, n_kv, head_dim)
        v_rows = proj[:, t - keep:, col_q + att_w + kv_w:].reshape(1, b, keep, n_kv, head_dim)
        x1 = _outproj(yc, ya, w_out_bf, b_out, x, modx)
        routed = _router(x1, modx, norm_ffn_g, wr_hi, wr_lo, b_router, counts_before, hp_prev,
                         tok_offset, n_p + n_s)
        return x1, routed, conv_state, k_rows, v_rows

    zero_state = jnp.zeros((bp, 2, conv_ch), F32)
    x1p, (hp_buf, idxp, gatep, rankp, cntp), conv_p, k_prompt, v_prompt = mixer(
        x_prompt, mod_p, zero_state, None, None, jnp.zeros((1, n_exp), F32), None, 0)
    x1s, (h2_packed, idxs, gates_s, ranks, cnt), conv_s, k_sample, v_sample = mixer(
        x_sample, mod_s, state_conv[0],
        cache_k[0].reshape(dec_b, window, kv_w), cache_v[0].reshape(dec_b, window, kv_w), cntp, hp_buf, n_p)

    top_idx = jnp.concatenate([idxp, idxs], axis=0)
    gates = jnp.concatenate([gatep, gates_s], axis=0)
    rank = jnp.concatenate([rankp, ranks], axis=0)
    pos, row_token, tile_first, tile_count, eff, n_tiles, tail_half = _routing(
        top_idx, rank, cnt[0].astype(jnp.int32), MOE_ROWS)
    xs = _gather_rows(h2_packed, row_token, n_tiles, MOE_ROWS)
    act = _moe_up(xs, w_gate_up, b_gate_up, tile_first, tile_count, eff, n_tiles, tail_half)
    y_sorted = _moe_down(act, w_down, b_down, tile_first, tile_count, eff, n_tiles, tail_half)
    y_prompt = _combine(y_sorted, pos, gates, x1p, mod_p, norm_final_g, 0)
    y_sample = _combine(y_sorted, pos, gates, x1s, mod_s, norm_final_g, n_p)
    return (y_prompt, y_sample, k_prompt, v_prompt, conv_p[None], k_sample, v_sample, conv_s[None])
```

```python
import functools

import jax
import jax.numpy as jnp
from jax import lax
from jax.experimental import pallas as pl
from jax.experimental.pallas import tpu as pltpu

F32 = jnp.float32
BF16 = jnp.bfloat16
U32 = jnp.uint32

CHUNK = 64
TOP_K = 4
PAST_LEN = 2048
ROPE_THETA = 10000.0
NORM_EPS = 1e-5
SWIGLU_LIMIT = 7.0
SWIGLU_ALPHA = 1.702

LANES = 128
SUBLANES = 8
VMEM_LIMIT_BYTES = 58 * 1024 * 1024

ROW_TILE = 512
ADA_COL_TILE = 512
INPROJ_COL_TILE = 1024
OUTPROJ_COL_TILE = 1024
CONV_ROW_TILE = 256
ATTN_CHUNKS_PER_STEP = 4
FFN_ROW_TILE = 256
MOE_ROWS = 256
MOE_FF_TILE = 512
MOE_OUT_TILE = 1024
MOE_IN_SLOTS = 5
MOE_OUT_SLOTS = 4
MOE_CAST_CHUNKS = 4
COMBINE_ROWS = 128
ROW_COPY_UNROLL = 8
INDEX_BATCH_TILES = 8
ROW_COPY_SLOTS = 3
TILE_COPY_PRIORITY = 1


def _tile(pref, dim):
    t = min(pref, dim)
    while dim % t:
        t -= 1
    return t


def _params(sem):
    return pltpu.CompilerParams(dimension_semantics=sem, vmem_limit_bytes=VMEM_LIMIT_BYTES)


def _rms(x):
    return x * lax.rsqrt(jnp.mean(x * x, axis=-1, keepdims=True) + NORM_EPS)


def _token_blocks(b, t, rows_pref):
    if t >= rows_pref:
        return 1, _tile(rows_pref, t)
    return _tile(max(rows_pref // t, 1), b), t


def _adaln_kernel(c_ref, w_ref, b_ref, o_ref):
    c = c_ref[...]
    a = (c * jax.nn.sigmoid(c)).astype(BF16)
    o_ref[...] = jnp.dot(a, w_ref[...].astype(BF16), preferred_element_type=F32) + b_ref[...]


def _adaln(c, w_ada, b_ada):
    nb, d = c.shape
    n = w_ada.shape[1]
    tn = _tile(ADA_COL_TILE, n)
    return pl.pallas_call(
        _adaln_kernel,
        grid=(n // tn,),
        in_specs=[pl.BlockSpec((nb, d), lambda j: (0, 0)),
                  pl.BlockSpec((d, tn), lambda j: (0, j)),
                  pl.BlockSpec((1, tn), lambda j: (0, j))],
        out_specs=pl.BlockSpec((nb, tn), lambda j: (0, j)),
        out_shape=jax.ShapeDtypeStruct((nb, n), F32),
        compiler_params=_params(("arbitrary",)),
        name="adaln",
    )(c, w_ada, b_ada.reshape(1, n))


def _inproj_kernel(x_ref, sh_ref, sc_ref, g_ref, w_ref, b_ref, o_ref, h_ref):
    bb, tt, d = x_ref.shape

    @pl.when(pl.program_id(2) == 0)
    def _():
        y = _rms(x_ref[...]) * g_ref[...]
        h = y * (1.0 + sc_ref[...]) + sh_ref[...]
        h_ref[...] = h.reshape(bb * tt, d).astype(BF16)

    acc = jnp.dot(h_ref[...], w_ref[...], preferred_element_type=F32) + b_ref[...]
    o_ref[...] = acc.reshape(o_ref.shape)


def _inproj(x, mod, g, w_bf, b_in):
    b, t, d = x.shape
    n = w_bf.shape[1]
    bb, tt = _token_blocks(b, t, ROW_TILE)
    tn = _tile(INPROJ_COL_TILE, n)
    return pl.pallas_call(
        _inproj_kernel,
        grid=(b // bb, t // tt, n // tn),
        in_specs=[pl.BlockSpec((bb, tt, d), lambda i, s, j: (i, s, 0)),
                  pl.BlockSpec((bb, 1, d), lambda i, s, j: (i, 0, 0)),
                  pl.BlockSpec((bb, 1, d), lambda i, s, j: (i, 0, 1)),
                  pl.BlockSpec((1, d), lambda i, s, j: (0, 0)),
                  pl.BlockSpec((d, tn), lambda i, s, j: (0, j)),
                  pl.BlockSpec((1, tn), lambda i, s, j: (0, j))],
        out_specs=pl.BlockSpec((bb, tt, tn), lambda i, s, j: (i, s, j)),
        out_shape=jax.ShapeDtypeStruct((b, t, n), F32),
        scratch_shapes=[pltpu.VMEM((bb * tt, d), BF16)],
        compiler_params=_params(("arbitrary", "arbitrary", "arbitrary")),
        name="inproj",
    )(x, mod, mod, g.reshape(1, d), w_bf, b_in.reshape(1, n))


def _conv_kernel(bg_ref, cg_ref, xc_ref, cgp_ref, xcp_ref, st_ref, w_ref, b_ref, g_ref, y_ref, so_ref):
    tt = cg_ref.shape[0]
    u = cg_ref[...] * xc_ref[...]
    halo = cgp_ref[...] * xcp_ref[...]
    st = st_ref[...]
    first = pl.program_id(1) == 0
    um2 = jnp.where(first, st[0:1], halo[SUBLANES - 2:SUBLANES - 1])
    um1 = jnp.where(first, st[1:2], halo[SUBLANES - 1:SUBLANES])
    row = lax.broadcasted_iota(jnp.int32, u.shape, 0)
    u1 = jnp.where(row == 0, um1, pltpu.roll(u, 1, 0))
    u2 = jnp.where(row == 0, um2, jnp.where(row == 1, um1, pltpu.roll(u, 2, 0)))
    w = w_ref[...]
    conv = b_ref[...] + u2 * w[0:1]
    conv = conv + u1 * w[1:2]
    conv = conv + u * w[2:3]
    y = _rms(bg_ref[...] * conv) * g_ref[...]
    y_ref[...] = y.astype(BF16)
    so_ref[...] = u[tt - 2:tt]


def _conv_mix(proj, state, conv_w, conv_b, g):
    b, t, _ = proj.shape
    c = conv_w.shape[1]
    assert conv_w.shape[0] == 3
    tt = _tile(CONV_ROW_TILE, t)
    hs = tt // SUBLANES
    halo_map = lambda col: (lambda i, s: (i, jnp.maximum(s * hs - 1, 0), col))
    return pl.pallas_call(
        _conv_kernel,
        grid=(b, t // tt),
        in_specs=[pl.BlockSpec((None, tt, c), lambda i, s: (i, s, 0)),
                  pl.BlockSpec((None, tt, c), lambda i, s: (i, s, 1)),
                  pl.BlockSpec((None, tt, c), lambda i, s: (i, s, 2)),
                  pl.BlockSpec((None, SUBLANES, c), halo_map(1)),
                  pl.BlockSpec((None, SUBLANES, c), halo_map(2)),
                  pl.BlockSpec((None, 2, c), lambda i, s: (i, 0, 0)),
                  pl.BlockSpec((3, c), lambda i, s: (0, 0)),
                  pl.BlockSpec((1, c), lambda i, s: (0, 0)),
                  pl.BlockSpec((1, c), lambda i, s: (0, 0))],
        out_specs=[pl.BlockSpec((None, tt, c), lambda i, s: (i, s, 0)),
                   pl.BlockSpec((None, 2, c), lambda i, s: (i, 0, 0))],
        out_shape=[jax.ShapeDtypeStruct((b, t, c), BF16),
                   jax.ShapeDtypeStruct((b, 2, c), F32)],
        compiler_params=_params(("arbitrary", "arbitrary")),
        name="conv_mix",
    )(proj, proj, proj, proj, proj, state, conv_w, conv_b.reshape(1, c), g.reshape(1, c))


def _head_order(n_heads, n_kv):
    grp = n_heads // n_kv
    assert n_kv % 2 == 0
    return [(2 * p + half) * grp + g for p in range(n_kv // 2) for g in range(grp) for half in (0, 1)]


def _permute_blocks(x, order, width, axis, start, dtype=None):
    n = len(order) * width
    cast = (lambda a: a) if dtype is None else (lambda a: a.astype(dtype))
    take = lambda lo, hi: cast(lax.slice_in_dim(x, lo, hi, axis=axis))
    parts = [take(0, start)] + [take(start + h * width, start + (h + 1) * width) for h in order]
    parts.append(take(start + n, x.shape[axis]))
    return jnp.concatenate(parts, axis=axis)


def _rope_tables(pos, head_dim):
    half = head_dim // 2
    inv_freq = ROPE_THETA ** (-jnp.arange(half, dtype=F32) / half)
    ang = pos.astype(F32)[:, None] * inv_freq[None, :]
    cos, sin = jnp.cos(ang), jnp.sin(ang)
    reps = LANES // head_dim
    return (jnp.tile(jnp.concatenate([cos, cos], axis=-1), (1, reps)),
            jnp.tile(jnp.concatenate([-sin, sin], axis=-1), (1, reps)))


def _rope(x, cos, sin, head_dim):
    width = x.shape[1]
    half = head_dim // 2
    reps = width // LANES
    cos = jnp.concatenate([cos] * reps, axis=1)
    sin = jnp.concatenate([sin] * reps, axis=1)
    lane = lax.broadcasted_iota(jnp.int32, x.shape, 1)
    swapped = jnp.where(lane % head_dim < half,
                        pltpu.roll(x, width - half, 1), pltpu.roll(x, half, 1))
    return x * cos + swapped * sin


def _pair_blocks(slab):
    keys = slab.shape[0]
    lo = lax.broadcasted_iota(jnp.int32, (CHUNK, LANES), 1) < LANES // 2
    zero = jnp.zeros((CHUNK, LANES), slab.dtype)
    parts = []
    for c in range(keys // CHUNK):
        blk = slab[c * CHUNK:(c + 1) * CHUNK]
        parts += [jnp.where(lo, blk, zero), jnp.where(lo, zero, blk)]
    return jnp.concatenate(parts, axis=0)


def _attend_pair(qs, kblk, vblk, sink_vec, key_ok):
    head_dim = LANES // 2
    n_chunks = kblk.shape[0] // LANES
    lo = lax.broadcasted_iota(jnp.int32, (1, LANES), 1) < head_dim
    s = lax.dot_general(qs.astype(BF16), kblk, (((1,), (1,)), ((), ())),
                        preferred_element_type=F32) * (head_dim ** -0.5)
    tiles = [jnp.where(key_ok[c], s[:, c * LANES:(c + 1) * LANES], -jnp.inf) for c in range(n_chunks)]

    def per_half(t, reduce_fn, fill):
        a = reduce_fn(jnp.where(lo, t, fill), axis=-1, keepdims=True)
        b = reduce_fn(jnp.where(lo, fill, t), axis=-1, keepdims=True)
        return jnp.where(lo, a, b)

    tmax = tiles[0]
    for t in tiles[1:]:
        tmax = jnp.maximum(tmax, t)
    m = jnp.maximum(per_half(tmax, jnp.max, -jnp.inf), sink_vec)
    es = [jnp.exp(t - m) for t in tiles]
    esum = es[0]
    for e in es[1:]:
        esum = esum + e
    den = per_half(esum, jnp.sum, 0.0) + jnp.exp(sink_vec - m)
    e_all = jnp.concatenate([e.astype(BF16) for e in es], axis=1)
    return jnp.dot(e_all, vblk, preferred_element_type=F32) / den


def _attend(q, kslabs, vslabs, sinks_ref, key_ok, grp):
    rows = q.shape[0]
    head_dim = LANES // 2
    lo = lax.broadcasted_iota(jnp.int32, (1, LANES), 1) < head_dim
    outs = []
    for p in range(len(kslabs)):
        qs = jnp.concatenate([q[:, (p * grp + g) * LANES:(p * grp + g + 1) * LANES] for g in range(grp)], axis=0)
        sink_vec = jnp.concatenate(
            [jnp.broadcast_to(jnp.where(lo, sinks_ref[2 * p * grp + g], sinks_ref[(2 * p + 1) * grp + g]),
                              (rows, LANES)) for g in range(grp)], axis=0)
        o = _attend_pair(qs, kslabs[p], vslabs[p], sink_vec, key_ok)
        outs.extend(o[g * rows:(g + 1) * rows] for g in range(grp))
    return jnp.concatenate(outs, axis=1)


def _swa_banded_kernel(sinks_ref, q_ref, k_ref, v_ref, cos_ref, sin_ref, g_ref,
                       y_ref, kr_ref, kbuf, vbuf, *, grp, window):
    rb = q_ref.shape[0]
    head_dim = LANES // 2
    n_pairs = kbuf.shape[1] // LANES
    step = pl.program_id(1)

    @pl.when(step == 0)
    def _():
        kbuf[0:window] = jnp.zeros((window, kbuf.shape[1]), BF16)
        vbuf[0:window] = jnp.zeros((window, vbuf.shape[1]), BF16)

    @pl.when(step > 0)
    def _():
        kbuf[0:window] = kbuf[rb:rb + window]
        vbuf[0:window] = vbuf[rb:rb + window]

    cos, sin = cos_ref[...], sin_ref[...]
    k = _rope(k_ref[...], cos, sin, head_dim)
    kr_ref[...] = k
    kbuf[window:window + rb] = k.astype(BF16)
    vbuf[window:window + rb] = v_ref[...].astype(BF16)
    q = _rope(q_ref[...], cos, sin, head_dim)

    n_buf_chunks = (window + rb) // CHUNK
    kblocks = [[_pair_blocks(kbuf[c * CHUNK:(c + 1) * CHUNK, p * LANES:(p + 1) * LANES])
                for c in range(n_buf_chunks)] for p in range(n_pairs)]
    vblocks = [[_pair_blocks(vbuf[c * CHUNK:(c + 1) * CHUNK, p * LANES:(p + 1) * LANES])
                for c in range(n_buf_chunks)] for p in range(n_pairs)]
    band_chunks = window // CHUNK + 1
    lane_key = lax.broadcasted_iota(jnp.int32, (1, LANES), 1) % CHUNK
    for c in range(rb // CHUNK):
        lo_row = c * CHUNK
        key_ok = [step * rb + (c + b) * CHUNK + lane_key >= window for b in range(band_chunks)]
        ks = [jnp.concatenate(kblocks[p][c:c + band_chunks], axis=0) for p in range(n_pairs)]
        vs = [jnp.concatenate(vblocks[p][c:c + band_chunks], axis=0) for p in range(n_pairs)]
        y = _attend(q[lo_row:lo_row + CHUNK], ks, vs, sinks_ref, key_ok, grp)
        y_ref[lo_row:lo_row + CHUNK] = (_rms(y) * g_ref[...]).astype(BF16)


def _swa_banded(proj, sinks, g, n_kv, head_dim, window, col_q):
    b, t, _ = proj.shape
    assert 2 * head_dim == LANES and CHUNK == head_dim
    n_heads = sinks.shape[0]
    att_w = n_heads * head_dim
    kv_w = n_kv * head_dim
    rb = _tile(ATTN_CHUNKS_PER_STEP * CHUNK, t)
    assert window % CHUNK == 0 and t % CHUNK == 0 and rb >= window
    cos, sin = _rope_tables(jnp.arange(t, dtype=jnp.int32), head_dim)
    col_k = (col_q + att_w) // kv_w
    kern = functools.partial(_swa_banded_kernel, grp=n_heads // n_kv, window=window)
    return pl.pallas_call(
        kern,
        grid=(b, t // rb),
        in_specs=[pl.BlockSpec(memory_space=pltpu.SMEM),
                  pl.BlockSpec((None, rb, att_w), lambda i, s: (i, s, col_q // att_w)),
                  pl.BlockSpec((None, rb, kv_w), lambda i, s: (i, s, col_k)),
                  pl.BlockSpec((None, rb, kv_w), lambda i, s: (i, s, col_k + 1)),
                  pl.BlockSpec((rb, LANES), lambda i, s: (s, 0)),
                  pl.BlockSpec((rb, LANES), lambda i, s: (s, 0)),
                  pl.BlockSpec((1, att_w), lambda i, s: (0, 0))],
        out_specs=[pl.BlockSpec((None, rb, att_w), lambda i, s: (i, s, 0)),
                   pl.BlockSpec((None, rb, kv_w), lambda i, s: (i, s, 0))],
        out_shape=[jax.ShapeDtypeStruct((b, t, att_w), BF16),
                   jax.ShapeDtypeStruct((b, t, kv_w), F32)],
        scratch_shapes=[pltpu.VMEM((window + rb, kv_w), BF16),
                        pltpu.VMEM((window + rb, kv_w), BF16)],
        compiler_params=_params(("arbitrary", "arbitrary")),
        name="swa_banded",
    )(sinks, proj, proj, proj, cos, sin, g.reshape(1, att_w))


def _swa_cached_kernel(sinks_ref, q_ref, k_ref, v_ref, ck_ref, cv_ref, cos_ref, sin_ref, g_ref,
                       y_ref, kr_ref, *, grp):
    head_dim = LANES // 2
    t, kv_w = k_ref.shape
    window = ck_ref.shape[0]
    n_pairs = kv_w // LANES
    cos, sin = cos_ref[...], sin_ref[...]
    k = _rope(k_ref[...], cos, sin, head_dim)
    kr_ref[...] = k
    q = _rope(q_ref[...], cos, sin, head_dim)
    n_keys = window + t
    pad = -n_keys % CHUNK
    tail = [jnp.zeros((pad, kv_w), BF16)] if pad else []
    kb = jnp.concatenate([ck_ref[...].astype(BF16), k.astype(BF16)] + tail, axis=0)
    vb = jnp.concatenate([cv_ref[...].astype(BF16), v_ref[...].astype(BF16)] + tail, axis=0)
    ks = [_pair_blocks(kb[:, p * LANES:(p + 1) * LANES]) for p in range(n_pairs)]
    vs = [_pair_blocks(vb[:, p * LANES:(p + 1) * LANES]) for p in range(n_pairs)]
    lane_key = lax.broadcasted_iota(jnp.int32, (1, LANES), 1) % CHUNK
    key_ok = [c * CHUNK + lane_key < n_keys for c in range((n_keys + pad) // CHUNK)]
    y = _attend(q, ks, vs, sinks_ref, key_ok, grp)
    y_ref[...] = (_rms(y) * g_ref[...]).astype(BF16)


def _swa_cached(proj, cache_k, cache_v, sinks, g, n_kv, head_dim, col_q):
    b, t, _ = proj.shape
    assert 2 * head_dim == LANES and CHUNK == head_dim
    window = cache_k.shape[1]
    n_heads = sinks.shape[0]
    att_w = n_heads * head_dim
    kv_w = n_kv * head_dim
    cos, sin = _rope_tables(PAST_LEN + jnp.arange(t, dtype=jnp.int32), head_dim)
    col_k = (col_q + att_w) // kv_w
    kern = functools.partial(_swa_cached_kernel, grp=n_heads // n_kv)
    return pl.pallas_call(
        kern,
        grid=(b,),
        in_specs=[pl.BlockSpec(memory_space=pltpu.SMEM),
                  pl.BlockSpec((None, t, att_w), lambda i: (i, 0, col_q // att_w)),
                  pl.BlockSpec((None, t, kv_w), lambda i: (i, 0, col_k)),
                  pl.BlockSpec((None, t, kv_w), lambda i: (i, 0, col_k + 1)),
                  pl.BlockSpec((None, window, kv_w), lambda i: (i, 0, 0)),
                  pl.BlockSpec((None, window, kv_w), lambda i: (i, 0, 0)),
                  pl.BlockSpec((t, LANES), lambda i: (0, 0)),
                  pl.BlockSpec((t, LANES), lambda i: (0, 0)),
                  pl.BlockSpec((1, att_w), lambda i: (0, 0))],
        out_specs=[pl.BlockSpec((None, t, att_w), lambda i: (i, 0, 0)),
                   pl.BlockSpec((None, t, kv_w), lambda i: (i, 0, 0))],
        out_shape=[jax.ShapeDtypeStruct((b, t, att_w), BF16),
                   jax.ShapeDtypeStruct((b, t, kv_w), F32)],
        compiler_params=_params(("arbitrary",)),
        name="swa_cached",
    )(sinks, proj, proj, proj, cache_k, cache_v, cos, sin, g.reshape(1, att_w))


def _outproj_kernel(yc_ref, ya_ref, wt_ref, wb_ref, b_ref, x_ref, g1_ref, o_ref):
    bb, tt, c = yc_ref.shape
    acc = jnp.dot(yc_ref[...].reshape(bb * tt, c), wt_ref[...], preferred_element_type=F32)
    acc = acc + jnp.dot(ya_ref[...].reshape(bb * tt, c), wb_ref[...], preferred_element_type=F32)
    acc = acc + b_ref[...]
    o_ref[...] = x_ref[...] + g1_ref[...] * acc.reshape(o_ref.shape)


def _outproj(yc, ya, w_bf, b_out, x, mod):
    b, t, d = x.shape
    c = yc.shape[2]
    assert 2 * c == w_bf.shape[0]
    bb, tt = _token_blocks(b, t, ROW_TILE)
    tn = _tile(OUTPROJ_COL_TILE, d)
    return pl.pallas_call(
        _outproj_kernel,
        grid=(b // bb, t // tt, d // tn),
        in_specs=[pl.BlockSpec((bb, tt, c), lambda i, s, j: (i, s, 0)),
                  pl.BlockSpec((bb, tt, c), lambda i, s, j: (i, s, 0)),
                  pl.BlockSpec((c, tn), lambda i, s, j: (0, j)),
                  pl.BlockSpec((c, tn), lambda i, s, j: (1, j)),
                  pl.BlockSpec((1, tn), lambda i, s, j: (0, j)),
                  pl.BlockSpec((bb, tt, tn), lambda i, s, j: (i, s, j)),
                  pl.BlockSpec((bb, 1, tn), lambda i, s, j: (i, 0, 2 * (d // tn) + j))],
        out_specs=pl.BlockSpec((bb, tt, tn), lambda i, s, j: (i, s, j)),
        out_shape=jax.ShapeDtypeStruct((b, t, d), F32),
        compiler_params=_params(("arbitrary", "arbitrary", "arbitrary")),
        name="outproj",
    )(yc, ya, w_bf, w_bf, b_out.reshape(1, d), x, mod)


def _router_kernel(x_ref, sh_ref, sc_ref, g_ref, whi_ref, wlo_ref, br_ref, c0_ref, *rest, has_prev):
    hp_ref, idx_ref, gate_ref, rank_ref, cnt_ref, carry = rest[1:] if has_prev else rest
    bb, tt, d = x_ref.shape
    rows = bb * tt
    n_exp = whi_ref.shape[1]

    @pl.when((pl.program_id(0) == 0) & (pl.program_id(1) == 0))
    def _():
        carry[...] = c0_ref[...]

    y = _rms(x_ref[...]) * g_ref[...]
    h = (y * (1.0 + sc_ref[...]) + sh_ref[...]).reshape(rows, d)
    h_hi = h.astype(BF16)
    h_hi32 = h_hi.astype(F32)
    bits = lax.bitcast_convert_type(h_hi32, U32)
    hp_ref[...] = (bits[:, d // 2:] & jnp.uint32(0xFFFF0000)) | (bits[:, :d // 2] >> 16)

    h_lo = (h - h_hi32).astype(BF16)
    logits = jnp.dot(h_hi, wlo_ref[...], preferred_element_type=F32)
    logits = logits + jnp.dot(h_lo, whi_ref[...], preferred_element_type=F32)
    logits = logits + jnp.dot(h_hi, whi_ref[...], preferred_element_type=F32) + br_ref[...]
    col = lax.broadcasted_iota(jnp.int32, logits.shape, 1).astype(F32)
    work = logits
    vals, idxs = [], []
    for _ in range(TOP_K):
        m = jnp.max(work, axis=-1, keepdims=True)
        sel = jnp.min(jnp.where(work == m, col, float(n_exp)), axis=-1, keepdims=True)
        vals.append(m)
        idxs.append(sel)
        work = jnp.where(col == sel, -jnp.inf, work)
    exps = [jnp.exp(v - vals[0]) for v in vals]
    den = exps[0]
    for e in exps[1:]:
        den = den + e
    idx_ref[...] = jnp.concatenate(idxs, axis=1).astype(jnp.int32)
    gate_ref[...] = jnp.concatenate([e / den for e in exps], axis=1)

    onehots = [(col == sel).astype(F32) for sel in idxs]
    cnt = onehots[0]
    for oh in onehots[1:]:
        cnt = cnt + oh
    earlier = (lax.broadcasted_iota(jnp.int32, (rows, rows), 0)
               > lax.broadcasted_iota(jnp.int32, (rows, rows), 1)).astype(BF16)
    before = jnp.dot(earlier, cnt.astype(BF16), preferred_element_type=F32) + carry[...]
    ranks = [jnp.sum(oh * before, axis=-1, keepdims=True) for oh in onehots]
    rank_ref[...] = jnp.concatenate(ranks, axis=1).astype(jnp.int32)
    carry[...] = carry[...] + jnp.sum(cnt, axis=0, keepdims=True)
    cnt_ref[...] = carry[...]


def _router(x1, mod, g, w_hi, w_lo, b_router, counts_before, hp_prev, tok_offset, n_tok_all):
    b, t, d = x1.shape
    n_exp = w_hi.shape[1]
    bb, tt = _token_blocks(b, t, FFN_ROW_TILE)
    rows = bb * tt
    n_tok = b * t
    assert tok_offset % rows == 0
    blk0 = tok_offset // rows
    flat = lambda i, s: (i * (t // tt) + s, 0)
    has_prev = hp_prev is not None
    in_specs = [pl.BlockSpec((bb, tt, d), lambda i, s: (i, s, 0)),
                pl.BlockSpec((bb, 1, d), lambda i, s: (i, 0, 3)),
                pl.BlockSpec((bb, 1, d), lambda i, s: (i, 0, 4)),
                pl.BlockSpec((1, d), lambda i, s: (0, 0)),
                pl.BlockSpec((d, n_exp), lambda i, s: (0, 0)),
                pl.BlockSpec((d, n_exp), lambda i, s: (0, 0)),
                pl.BlockSpec((1, n_exp), lambda i, s: (0, 0)),
                pl.BlockSpec((1, n_exp), lambda i, s: (0, 0))]
    args = [x1, mod, mod, g.reshape(1, d), w_hi, w_lo, b_router.reshape(1, n_exp), counts_before]
    if has_prev:
        in_specs.append(pl.BlockSpec(memory_space=pl.ANY))
        args.append(hp_prev)
    return pl.pallas_call(
        functools.partial(_router_kernel, has_prev=has_prev),
        grid=(b // bb, t // tt),
        in_specs=in_specs,
        out_specs=[pl.BlockSpec((rows, d // 2), lambda i, s: (blk0 + i * (t // tt) + s, 0)),
                   pl.BlockSpec((rows, TOP_K), flat),
                   pl.BlockSpec((rows, TOP_K), flat),
                   pl.BlockSpec((rows, TOP_K), flat),
                   pl.BlockSpec((1, n_exp), lambda i, s: (0, 0))],
        out_shape=[jax.ShapeDtypeStruct((n_tok_all, d // 2), U32),
                   jax.ShapeDtypeStruct((n_tok, TOP_K), jnp.int32),
                   jax.ShapeDtypeStruct((n_tok, TOP_K), F32),
                   jax.ShapeDtypeStruct((n_tok, TOP_K), jnp.int32),
                   jax.ShapeDtypeStruct((1, n_exp), F32)],
        scratch_shapes=[pltpu.VMEM((1, n_exp), F32)],
        input_output_aliases={len(args) - 1: 0} if has_prev else {},
        compiler_params=_params(("arbitrary", "arbitrary")),
        name="router",
    )(*args)


def _stage_indices(idx_batch_ref, idx_smem, sem_idx):
    cp = pltpu.make_async_copy(idx_batch_ref.at[0, 0], idx_smem, sem_idx)
    cp.start()
    cp.wait()


def _issue_row_copies(idx_smem, base, src_ref, dst_ref, sem):
    def body(g, carry):
        for u in range(ROW_COPY_UNROLL):
            r = g * ROW_COPY_UNROLL + u
            pltpu.make_async_copy(src_ref.at[pl.ds(idx_smem[base + r], 1)], dst_ref.at[pl.ds(r, 1)], sem).start(
                priority=u % 2)
        return carry

    lax.fori_loop(0, dst_ref.shape[0] // ROW_COPY_UNROLL, body, 0)


def _issue_tile(tile, first_tile, idx0_ref, idxn_ref, idx_smem, sem_idx, src_ref, dst_ref, sem):
    n_rows = dst_ref.shape[0]
    batch = idx_smem.shape[0] // n_rows
    rel = tile - first_tile

    @pl.when(rel == 0)
    def _():
        _stage_indices(idx0_ref, idx_smem, sem_idx)

    @pl.when((rel > 0) & (lax.rem(rel, batch) == 0))
    def _():
        _stage_indices(idxn_ref, idx_smem, sem_idx)

    _issue_row_copies(idx_smem, lax.rem(rel, batch) * n_rows, src_ref, dst_ref, sem)


def _wait_row_copies(src_ref, dst_ref, sem):
    pltpu.make_async_copy(src_ref.at[pl.ds(0, dst_ref.shape[0])], dst_ref, sem).wait()


def _gather_kernel(nused_ref, idx0_ref, idxn_ref, src_ref, o_ref, idx_smem, buf, sem_idx, sem_rows):
    i = pl.program_id(0)
    n = nused_ref[0]
    half = src_ref.shape[1]
    n_slots = buf.shape[0]
    ahead = n_slots - 1

    @pl.when(i == 0)
    def _():
        for k in range(ahead):
            @pl.when(k < n)
            def _():
                _issue_tile(jnp.int32(k), 0, idx0_ref, idxn_ref, idx_smem, sem_idx, src_ref, buf.at[k],
                            sem_rows.at[k])

    @pl.when(i + ahead < n)
    def _():
        nxt = lax.rem(i + ahead, n_slots)
        _issue_tile(i + ahead, 0, idx0_ref, idxn_ref, idx_smem, sem_idx, src_ref, buf.at[nxt], sem_rows.at[nxt])

    @pl.when(i < n)
    def _():
        slot = lax.rem(i, n_slots)
        _wait_row_copies(src_ref, buf.at[slot], sem_rows.at[slot])
        w = buf[slot]
        o_ref[:, :half] = _pair_rows(lax.bitcast_convert_type(w << 16, F32).astype(BF16))
        o_ref[:, half:] = _pair_rows(lax.bitcast_convert_type(w & jnp.uint32(0xFFFF0000), F32).astype(BF16))


def _pair_rows(x):
    return pltpu.bitcast(x, U32)


def _gather_rows(src_packed, row_src, n_used, rows):
    p = row_src.shape[0]
    half = src_packed.shape[1]
    nblk = p // rows
    batch = _tile(INDEX_BATCH_TILES, nblk)
    ahead = ROW_COPY_SLOTS - 1
    assert batch >= ahead
    idx_batches = row_src.reshape(nblk // batch, 1, batch * rows)
    clamp = lambda i, nu: jnp.minimum(i, nu[0] - 1)
    return pl.pallas_call(
        _gather_kernel,
        grid_spec=pltpu.PrefetchScalarGridSpec(
            num_scalar_prefetch=1,
            grid=(nblk,),
            in_specs=[pl.BlockSpec((1, 1, batch * rows), lambda i, nu: (0, 0, 0)),
                      pl.BlockSpec((1, 1, batch * rows), lambda i, nu: (clamp(i + ahead, nu) // batch, 0, 0)),
                      pl.BlockSpec(memory_space=pl.ANY)],
            out_specs=pl.BlockSpec((rows // 2, 2 * half), lambda i, nu: (clamp(i, nu), 0)),
            scratch_shapes=[pltpu.SMEM((batch * rows,), jnp.int32),
                            pltpu.VMEM((ROW_COPY_SLOTS, rows, half), U32),
                            pltpu.SemaphoreType.DMA(()),
                            pltpu.SemaphoreType.DMA((ROW_COPY_SLOTS,))]),
        out_shape=jax.ShapeDtypeStruct((p // 2, 2 * half), U32),
        compiler_params=_params(("arbitrary",)),
        name="moe_gather",
    )(n_used, idx_batches, idx_batches, src_packed)


def _stream_expert_tiles(first_ref, ntile_ref, ntot_ref, half_ref, src_hbm, dst_hbm, xbuf, obuf, sem_in, sem_out,
                         compute_first, compute):
    j, e = pl.program_id(0), pl.program_id(1)
    nj, ne = pl.num_programs(0), pl.num_programs(1)
    n_slots = xbuf.shape[0]
    lookahead = n_slots - 2
    in_rows = xbuf.shape[1]
    n_out = obuf.shape[0]
    out_rows = obuf.shape[1]
    tn = obuf.shape[2]
    n_total = ntot_ref[0]
    n_seq = nj * n_total

    def x_copy(t, slot):
        r0 = pl.multiple_of(t * in_rows, in_rows)
        return pltpu.make_async_copy(src_hbm.at[pl.ds(r0, in_rows)], xbuf.at[slot], sem_in.at[slot])

    def x_start(seq):
        x_copy(lax.rem(seq, n_total), lax.rem(seq, n_slots)).start(priority=TILE_COPY_PRIORITY)

    def o_copy(t, slot):
        r0 = pl.multiple_of(t * out_rows, out_rows)
        c0 = pl.multiple_of(j * tn, tn)
        return pltpu.make_async_copy(obuf.at[slot], dst_hbm.at[pl.ds(r0, out_rows), pl.ds(c0, tn)],
                                     sem_out.at[slot])

    @pl.when((j == 0) & (e == 0))
    def _():
        for seq in range(lookahead):
            @pl.when(seq < n_seq)
            def _():
                x_start(jnp.int32(seq))

    def step(t, compute_fn, width, half_last=False):
        seq = j * n_total + t
        in_slots = [lax.rem(seq + k, n_slots) for k in range(width)]
        out_slots = [lax.rem(seq + k, n_out) for k in range(width)]
        for k in range(width):
            x_copy(0, in_slots[k]).wait()

        for k in range(width):
            @pl.when(seq + lookahead + k < n_seq)
            def _():
                x_start(seq + lookahead + k)

        for k in range(width):
            @pl.when(seq + k >= n_out)
            def _():
                o_copy(t, out_slots[k]).wait()

        for k in range(width):
            if half_last and k == width - 1:
                obuf[out_slots[k], 0:out_rows // 2] = compute_fn(xbuf[in_slots[k], 0:in_rows // 2])
            else:
                obuf[out_slots[k]] = compute_fn(xbuf[in_slots[k]])
        for k in range(width):
            o_copy(t + k, out_slots[k]).start()

    t0 = first_ref[e]
    nt = ntile_ref[e]

    @pl.when(nt > 0)
    def _():
        step(t0, compute_first, 1)

    half = (half_ref[e] > 0) & (nt > 1)
    pair_ends = (nt > 1) & (lax.rem(nt - 1, 2) == 0)
    lone_last = (nt > 1) & (lax.rem(nt - 1, 2) == 1)
    n_pairs = jnp.maximum(nt - 1, 0) // 2 - jnp.where(half & pair_ends, 1, 0)

    def body(p, carry):
        step(t0 + 1 + 2 * p, compute, 2)
        return carry

    lax.fori_loop(0, n_pairs, body, 0)

    @pl.when(half & pair_ends)
    def _():
        step(t0 + nt - 2, compute, 2, half_last=True)

    @pl.when(lone_last & jnp.logical_not(half))
    def _():
        step(t0 + nt - 1, compute, 1)

    @pl.when(lone_last & half)
    def _():
        step(t0 + nt - 1, compute, 1, half_last=True)

    @pl.when((j == nj - 1) & (e == ne - 1))
    def _():
        for k in range(n_out):
            @pl.when(n_seq > k)
            def _():
                o_copy(0, lax.rem(n_seq - 1 - k, n_out)).wait()


def _swiglu(g, u):
    g = jnp.minimum(g, SWIGLU_LIMIT)
    u = jnp.clip(u, -SWIGLU_LIMIT, SWIGLU_LIMIT)
    return (u + 1.0) * g * jax.nn.sigmoid(SWIGLU_ALPHA * g)


def _moe_up_kernel(first_ref, ntile_ref, eff_ref, ntot_ref, half_ref, x_hbm, wg_ref, wu_ref, bg_ref, bu_ref,
                   act_hbm, xbuf, obuf, wg_bf, wu_bf, sem_in, sem_out):
    kc = wg_ref.shape[0] // MOE_CAST_CHUNKS

    def compute_first(x_words):
        x = pltpu.bitcast(x_words, BF16)
        g = bg_ref[...]
        u = bu_ref[...]
        for c in range(MOE_CAST_CHUNKS):
            ks = slice(c * kc, (c + 1) * kc)
            wg_c = wg_ref[ks, :].astype(BF16)
            wu_c = wu_ref[ks, :].astype(BF16)
            wg_bf[ks, :] = wg_c
            wu_bf[ks, :] = wu_c
            g = g + jnp.dot(x[:, ks], wg_c, preferred_element_type=F32)
            u = u + jnp.dot(x[:, ks], wu_c, preferred_element_type=F32)
        return _pair_rows(_swiglu(g, u).astype(BF16))

    def compute(x_words):
        x = pltpu.bitcast(x_words, BF16)
        g = jnp.dot(x, wg_bf[...], preferred_element_type=F32) + bg_ref[...]
        u = jnp.dot(x, wu_bf[...], preferred_element_type=F32) + bu_ref[...]
        return _pair_rows(_swiglu(g, u).astype(BF16))

    _stream_expert_tiles(first_ref, ntile_ref, ntot_ref, half_ref, x_hbm, act_hbm, xbuf, obuf, sem_in, sem_out,
                         compute_first, compute)


def _moe_up(xs, w_gate_up, b_gate_up, tile_first, tile_count, eff_expert, n_tiles, tail_half):
    p2, d = xs.shape
    n_exp, _, f2 = w_gate_up.shape
    f = f2 // 2
    tf = _tile(MOE_FF_TILE, f)
    nj = f // tf
    return pl.pallas_call(
        _moe_up_kernel,
        grid_spec=pltpu.PrefetchScalarGridSpec(
            num_scalar_prefetch=5,
            grid=(nj, n_exp),
            in_specs=[pl.BlockSpec(memory_space=pl.ANY),
                      pl.BlockSpec((None, d, tf), lambda j, e, fi, nt, ef, ntot, hf: (ef[e], 0, j)),
                      pl.BlockSpec((None, d, tf), lambda j, e, fi, nt, ef, ntot, hf: (ef[e], 0, nj + j)),
                      pl.BlockSpec((None, 1, tf), lambda j, e, fi, nt, ef, ntot, hf: (ef[e], 0, j)),
                      pl.BlockSpec((None, 1, tf), lambda j, e, fi, nt, ef, ntot, hf: (ef[e], 0, nj + j))],
            out_specs=pl.BlockSpec(memory_space=pl.ANY),
            scratch_shapes=[pltpu.VMEM((MOE_IN_SLOTS, MOE_ROWS // 2, d), U32),
                            pltpu.VMEM((MOE_OUT_SLOTS, MOE_ROWS // 2, tf), U32),
                            pltpu.VMEM((d, tf), BF16),
                            pltpu.VMEM((d, tf), BF16),
                            pltpu.SemaphoreType.DMA((MOE_IN_SLOTS,)),
                            pltpu.SemaphoreType.DMA((MOE_OUT_SLOTS,))]),
        out_shape=jax.ShapeDtypeStruct((p2, f), U32),
        compiler_params=_params(("arbitrary", "arbitrary")),
        name="moe_up",
    )(tile_first, tile_count, eff_expert, n_tiles, tail_half, xs, w_gate_up, w_gate_up,
      b_gate_up.reshape(n_exp, 1, f2), b_gate_up.reshape(n_exp, 1, f2))


def _moe_down_kernel(first_ref, ntile_ref, eff_ref, ntot_ref, half_ref, a_hbm, w_ref, b_ref, y_hbm,
                     xbuf, obuf, w_bf, sem_in, sem_out):
    kc = w_ref.shape[0] // MOE_CAST_CHUNKS

    def compute_first(a_words):
        a = pltpu.bitcast(a_words, BF16)
        y = b_ref[...]
        for c in range(MOE_CAST_CHUNKS):
            ks = slice(c * kc, (c + 1) * kc)
            w_c = w_ref[ks, :].astype(BF16)
            w_bf[ks, :] = w_c
            y = y + jnp.dot(a[:, ks], w_c, preferred_element_type=F32)
        return y

    def compute(a_words):
        return jnp.dot(pltpu.bitcast(a_words, BF16), w_bf[...], preferred_element_type=F32) + b_ref[...]

    _stream_expert_tiles(first_ref, ntile_ref, ntot_ref, half_ref, a_hbm, y_hbm, xbuf, obuf, sem_in, sem_out,
                         compute_first, compute)


def _moe_down(act, w_down, b_down, tile_first, tile_count, eff_expert, n_tiles, tail_half):
    p2, f = act.shape
    n_exp, _, d = w_down.shape
    tn = _tile(MOE_OUT_TILE, d)
    return pl.pallas_call(
        _moe_down_kernel,
        grid_spec=pltpu.PrefetchScalarGridSpec(
            num_scalar_prefetch=5,
            grid=(d // tn, n_exp),
            in_specs=[pl.BlockSpec(memory_space=pl.ANY),
                      pl.BlockSpec((None, f, tn), lambda j, e, fi, nt, ef, ntot, hf: (ef[e], 0, j)),
                      pl.BlockSpec((None, 1, tn), lambda j, e, fi, nt, ef, ntot, hf: (ef[e], 0, j))],
            out_specs=pl.BlockSpec(memory_space=pl.ANY),
            scratch_shapes=[pltpu.VMEM((MOE_IN_SLOTS, MOE_ROWS // 2, f), U32),
                            pltpu.VMEM((MOE_OUT_SLOTS, MOE_ROWS, tn), F32),
                            pltpu.VMEM((f, tn), BF16),
                            pltpu.SemaphoreType.DMA((MOE_IN_SLOTS,)),
                            pltpu.SemaphoreType.DMA((MOE_OUT_SLOTS,))]),
        out_shape=jax.ShapeDtypeStruct((2 * p2, d), F32),
        compiler_params=_params(("arbitrary", "arbitrary")),
        name="moe_down",
    )(tile_first, tile_count, eff_expert, n_tiles, tail_half, act, w_down, b_down.reshape(n_exp, 1, d))


def _combine_kernel(pos0_ref, posn_ref, y_ref, gate_ref, x_ref, g2_ref, gf_ref, o_ref,
                    pos_smem, buf, sem_idx, sem_rows):
    bb, tt, d = x_ref.shape
    rows = bb * tt
    n_slots = buf.shape[0]
    ahead = n_slots - 1
    i = pl.program_id(0)
    n = pl.num_programs(0)

    @pl.when(i == 0)
    def _():
        for k in range(ahead):
            @pl.when(k < n)
            def _():
                _issue_tile(jnp.int32(k), 0, pos0_ref, posn_ref, pos_smem, sem_idx, y_ref, buf.at[k],
                            sem_rows.at[k])

    @pl.when(i + ahead < n)
    def _():
        nxt = lax.rem(i + ahead, n_slots)
        _issue_tile(i + ahead, 0, pos0_ref, posn_ref, pos_smem, sem_idx, y_ref, buf.at[nxt], sem_rows.at[nxt])

    slot = lax.rem(i, n_slots)
    _wait_row_copies(y_ref, buf.at[slot], sem_rows.at[slot])
    gates = gate_ref[...]
    ffn = gates[:, 0:1] * buf[slot, 0:rows]
    for k in range(1, TOP_K):
        ffn = ffn + gates[:, k:k + 1] * buf[slot, k * rows:(k + 1) * rows]
    x2 = x_ref[...] + g2_ref[...] * ffn.reshape(bb, tt, d)
    o_ref[...] = _rms(x2) * gf_ref[...]


def _combine(y_sorted, pos, gates, x1, mod, gf, tok_offset):
    b, t, d = x1.shape
    bb, tt = _token_blocks(b, t, COMBINE_ROWS)
    rows = bb * tt
    assert tok_offset % rows == 0
    blk0 = tok_offset // rows
    n_tok = pos.shape[0]
    nblk = (b // bb) * (t // tt)
    ts = t // tt
    pos_tiles = pos.reshape(n_tok // rows, rows, TOP_K).transpose(0, 2, 1)[blk0:blk0 + nblk]
    batch = _tile(INDEX_BATCH_TILES, nblk)
    ahead = ROW_COPY_SLOTS - 1
    assert batch >= ahead
    pos_batches = pos_tiles.reshape(nblk // batch, 1, batch * TOP_K * rows)
    return pl.pallas_call(
        _combine_kernel,
        grid=(nblk,),
        in_specs=[pl.BlockSpec((1, 1, batch * TOP_K * rows), lambda i: (0, 0, 0)),
                  pl.BlockSpec((1, 1, batch * TOP_K * rows),
                               lambda i: (jnp.minimum(i + ahead, nblk - 1) // batch, 0, 0)),
                  pl.BlockSpec(memory_space=pl.ANY),
                  pl.BlockSpec((rows, TOP_K), lambda i: (blk0 + i, 0)),
                  pl.BlockSpec((bb, tt, d), lambda i: (i // ts, i % ts, 0)),
                  pl.BlockSpec((bb, 1, d), lambda i: (i // ts, 0, 5)),
                  pl.BlockSpec((1, d), lambda i: (0, 0))],
        out_specs=pl.BlockSpec((bb, tt, d), lambda i: (i // ts, i % ts, 0)),
        out_shape=jax.ShapeDtypeStruct((b, t, d), F32),
        scratch_shapes=[pltpu.SMEM((batch * TOP_K * rows,), jnp.int32),
                        pltpu.VMEM((ROW_COPY_SLOTS, TOP_K * rows, d), F32),
                        pltpu.SemaphoreType.DMA(()),
                        pltpu.SemaphoreType.DMA((ROW_COPY_SLOTS,))],
        compiler_params=_params(("arbitrary",)),
        name="moe_combine",
    )(pos_batches, pos_batches, y_sorted, gates, x1, mod, gf.reshape(1, d))


def _routing(top_idx, rank, counts, rows):
    n_tok = top_idx.shape[0]
    n_exp = counts.shape[0]
    n_assign = n_tok * TOP_K
    tile_count = (counts + rows - 1) // rows
    tile_end = jnp.cumsum(tile_count)
    tile_first = tile_end - tile_count
    expert_ids = jnp.arange(n_exp, dtype=top_idx.dtype)
    row_start = jnp.sum(jnp.where(top_idx[..., None] == expert_ids, tile_first * rows, 0), axis=-1)
    pos = row_start + rank
    p = n_assign + n_exp * rows
    token_of = jnp.arange(n_assign, dtype=jnp.int32) // TOP_K
    row_token = jnp.zeros((p,), jnp.int32).at[pos.reshape(-1)].set(
        token_of, unique_indices=True, mode="promise_in_bounds")
    ids = jnp.where(tile_count > 0, jnp.arange(n_exp, dtype=jnp.int32), -1)
    eff = lax.cummax(ids)
    eff = jnp.where(eff < 0, jnp.argmax(tile_count > 0).astype(jnp.int32), eff)
    i32 = lambda a: a.astype(jnp.int32)
    tail_half = (tile_count > 1) & (counts - (tile_count - 1) * rows <= rows // 2)
    return i32(pos), row_token, i32(tile_first), i32(tile_count), i32(eff), i32(tile_end[-1:]), i32(tail_half)


def kernel(x_prompt, x_sample, cache_k, cache_v, state_conv, c_prompt, c_sample, w_ada, b_ada, norm_mix_g, norm_ffn_g, norm_final_g, w_in, b_in, conv_w, conv_b, sinks, out_norm_conv_g, out_norm_attn_g, w_out, b_out, w_router, b_router, w_gate_up, b_gate_up, w_down, b_down):
    d = x_prompt.shape[2]
    depth, dec_b, window, n_kv, head_dim = cache_k.shape
    assert depth == 1
    conv_ch = conv_w.shape[1]
    n_exp = w_router.shape[1]
    n_heads = sinks.shape[0]
    bp, tp, _ = x_prompt.shape
    bs, ts, _ = x_sample.shape
    col_q = 3 * conv_ch
    kv_w = n_kv * head_dim
    att_w = n_heads * head_dim

    mod = _adaln(jnp.concatenate([c_prompt, c_sample], axis=0), w_ada, b_ada)
    mod_p = mod[:bp].reshape(bp, 1, 6 * d)
    mod_s = mod[bp:].reshape(bs, 1, 6 * d)
    order = _head_order(n_heads, n_kv)
    w_in_bf = _permute_blocks(w_in, order, head_dim, 1, col_q, BF16)
    b_in_p = _permute_blocks(b_in, order, head_dim, 0, col_q)
    w_out_bf = _permute_blocks(w_out, order, head_dim, 0, conv_ch, BF16)
    attn_g = _permute_blocks(out_norm_attn_g, order, head_dim, 0, 0)
    wr_hi = w_router.astype(BF16)
    wr_lo = (w_router - wr_hi.astype(F32)).astype(BF16)
    n_p, n_s = bp * tp, bs * ts

    def mixer(x, modx, conv_prev, ck, cv, counts_before, hp_prev, tok_offset):
        b, t, _ = x.shape
        proj = _inproj(x, modx, norm_mix_g, w_in_bf, b_in_p)
        yc, conv_state = _conv_mix(proj, conv_prev, conv_w, conv_b, out_norm_conv_g)
        if ck is None:
            ya, k_rot = _swa_banded(proj, sinks, attn_g, n_kv, head_dim, window, col_q)
            keep = window
        else:
            ya, k_rot = _swa_cached(proj, ck, cv, sinks, attn_g, n_kv, head_dim, col_q)
            keep = t
        k_rows = k_rot[:, t - keep:].reshape(1, b, keep, n_kv, head_dim)
        v_rows = proj[:, t - keep:, col_q + att_w + kv_w:].reshape(1, b, keep, n_kv, head_dim)
        x1 = _outproj(yc, ya, w_out_bf, b_out, x, modx)
        routed = _router(x1, modx, norm_ffn_g, wr_hi, wr_lo, b_router, counts_before, hp_prev,
                         tok_offset, n_p + n_s)
        return x1, routed, conv_state, k_rows, v_rows

    zero_state = jnp.zeros((bp, 2, conv_ch), F32)
    x1p, (hp_buf, idxp, gatep, rankp, cntp), conv_p, k_prompt, v_prompt = mixer(
        x_prompt, mod_p, zero_state, None, None, jnp.zeros((1, n_exp), F32), None, 0)
    x1s, (h2_packed, idxs, gates_s, ranks, cnt), conv_s, k_sample, v_sample = mixer(
        x_sample, mod_s, state_conv[0],
        cache_k[0].reshape(dec_b, window, kv_w), cache_v[0].reshape(dec_b, window, kv_w), cntp, hp_buf, n_p)

    top_idx = jnp.concatenate([idxp, idxs], axis=0)
    gates = jnp.concatenate([gatep, gates_s], axis=0)
    rank = jnp.concatenate([rankp, ranks], axis=0)
    pos, row_token, tile_first, tile_count, eff, n_tiles, tail_half = _routing(
        top_idx, rank, cnt[0].astype(jnp.int32), MOE_ROWS)
    xs = _gather_rows(h2_packed, row_token, n_tiles, MOE_ROWS)
    act = _moe_up(xs, w_gate_up, b_gate_up, tile_first, tile_count, eff, n_tiles, tail_half)
    y_sorted = _moe_down(act, w_down, b_down, tile_first, tile_count, eff, n_tiles, tail_half)
    y_prompt = _combine(y_sorted, pos, gates, x1p, mod_p, norm_final_g, 0)
    y_sample = _combine(y_sorted, pos, gates, x1s, mod_s, norm_final_g, n_p)
    return (y_prompt, y_sample, k_prompt, v_prompt, conv_p[None], k_sample, v_sample, conv_s[None])
```

```python
import functools

import jax
import jax.numpy as jnp
from jax import lax
from jax.experimental import pallas as pl
from jax.experimental.pallas import tpu as pltpu

F32 = jnp.float32
BF16 = jnp.bfloat16
U32 = jnp.uint32

CHUNK = 64
TOP_K = 4
PAST_LEN = 2048
ROPE_THETA = 10000.0
NORM_EPS = 1e-5
SWIGLU_LIMIT = 7.0
SWIGLU_ALPHA = 1.702

LANES = 128
SUBLANES = 8
VMEM_LIMIT_BYTES = 58 * 1024 * 1024

ROW_TILE = 512
ADA_COL_TILE = 512
INPROJ_COL_TILE = 1024
OUTPROJ_COL_TILE = 1024
CONV_ROW_TILE = 256
ATTN_CHUNKS_PER_STEP = 4
FFN_ROW_TILE = 256
MOE_ROWS = 256
MOE_FF_TILE = 512
MOE_OUT_TILE = 1024
MOE_IN_SLOTS = 5
MOE_OUT_SLOTS = 4
MOE_CAST_CHUNKS = 4
COMBINE_ROWS = 128
ROW_COPY_UNROLL = 8
INDEX_BATCH_TILES = 8
ROW_COPY_SLOTS = 3
TILE_COPY_PRIORITY = 1


def _tile(pref, dim):
    t = min(pref, dim)
    while dim % t:
        t -= 1
    return t


def _params(sem):
    return pltpu.CompilerParams(dimension_semantics=sem, vmem_limit_bytes=VMEM_LIMIT_BYTES)


def _rms(x):
    return x * lax.rsqrt(jnp.mean(x * x, axis=-1, keepdims=True) + NORM_EPS)


def _token_blocks(b, t, rows_pref):
    if t >= rows_pref:
        return 1, _tile(rows_pref, t)
    return _tile(max(rows_pref // t, 1), b), t


def _adaln_kernel(c_ref, w_ref, b_ref, o_ref):
    c = c_ref[...]
    a = (c * jax.nn.sigmoid(c)).astype(BF16)
    o_ref[...] = jnp.dot(a, w_ref[...].astype(BF16), preferred_element_type=F32) + b_ref[...]


def _adaln(c, w_ada, b_ada):
    nb, d = c.shape
    n = w_ada.shape[1]
    tn = _tile(ADA_COL_TILE, n)
    return pl.pallas_call(
        _adaln_kernel,
        grid=(n // tn,),
        in_specs=[pl.BlockSpec((nb, d), lambda j: (0, 0)),
                  pl.BlockSpec((d, tn), lambda j: (0, j)),
                  pl.BlockSpec((1, tn), lambda j: (0, j))],
        out_specs=pl.BlockSpec((nb, tn), lambda j: (0, j)),
        out_shape=jax.ShapeDtypeStruct((nb, n), F32),
        compiler_params=_params(("arbitrary",)),
        name="adaln",
    )(c, w_ada, b_ada.reshape(1, n))


def _inproj_kernel(x_ref, sh_ref, sc_ref, g_ref, w_ref, b_ref, o_ref, h_ref):
    bb, tt, d = x_ref.shape

    @pl.when(pl.program_id(2) == 0)
    def _():
        y = _rms(x_ref[...]) * g_ref[...]
        h = y * (1.0 + sc_ref[...]) + sh_ref[...]
        h_ref[...] = h.reshape(bb * tt, d).astype(BF16)

    acc = jnp.dot(h_ref[...], w_ref[...], preferred_element_type=F32) + b_ref[...]
    o_ref[...] = acc.reshape(o_ref.shape)


def _inproj(x, mod, g, w_bf, b_in):
    b, t, d = x.shape
    n = w_bf.shape[1]
    bb, tt = _token_blocks(b, t, ROW_TILE)
    tn = _tile(INPROJ_COL_TILE, n)
    return pl.pallas_call(
        _inproj_kernel,
        grid=(b // bb, t // tt, n // tn),
        in_specs=[pl.BlockSpec((bb, tt, d), lambda i, s, j: (i, s, 0)),
                  pl.BlockSpec((bb, 1, d), lambda i, s, j: (i, 0, 0)),
                  pl.BlockSpec((bb, 1, d), lambda i, s, j: (i, 0, 1)),
                  pl.BlockSpec((1, d), lambda i, s, j: (0, 0)),
                  pl.BlockSpec((d, tn), lambda i, s, j: (0, j)),
                  pl.BlockSpec((1, tn), lambda i, s, j: (0, j))],
        out_specs=pl.BlockSpec((bb, tt, tn), lambda i, s, j: (i, s, j)),
        out_shape=jax.ShapeDtypeStruct((b, t, n), F32),
        scratch_shapes=[pltpu.VMEM((bb * tt, d), BF16)],
        compiler_params=_params(("arbitrary", "arbitrary", "arbitrary")),
        name="inproj",
    )(x, mod, mod, g.reshape(1, d), w_bf, b_in.reshape(1, n))


def _conv_kernel(bg_ref, cg_ref, xc_ref, cgp_ref, xcp_ref, st_ref, w_ref, b_ref, g_ref, y_ref, so_ref):
    tt = cg_ref.shape[0]
    u = cg_ref[...] * xc_ref[...]
    halo = cgp_ref[...] * xcp_ref[...]
    st = st_ref[...]
    first = pl.program_id(1) == 0
    um2 = jnp.where(first, st[0:1], halo[SUBLANES - 2:SUBLANES - 1])
    um1 = jnp.where(first, st[1:2], halo[SUBLANES - 1:SUBLANES])
    row = lax.broadcasted_iota(jnp.int32, u.shape, 0)
    u1 = jnp.where(row == 0, um1, pltpu.roll(u, 1, 0))
    u2 = jnp.where(row == 0, um2, jnp.where(row == 1, um1, pltpu.roll(u, 2, 0)))
    w = w_ref[...]
    conv = b_ref[...] + u2 * w[0:1]
    conv = conv + u1 * w[1:2]
    conv = conv + u * w[2:3]
    y = _rms(bg_ref[...] * conv) * g_ref[...]
    y_ref[...] = y.astype(BF16)
    so_ref[...] = u[tt - 2:tt]


def _conv_mix(proj, state, conv_w, conv_b, g):
    b, t, _ = proj.shape
    c = conv_w.shape[1]
    assert conv_w.shape[0] == 3
    tt = _tile(CONV_ROW_TILE, t)
    hs = tt // SUBLANES
    halo_map = lambda col: (lambda i, s: (i, jnp.maximum(s * hs - 1, 0), col))
    return pl.pallas_call(
        _conv_kernel,
        grid=(b, t // tt),
        in_specs=[pl.BlockSpec((None, tt, c), lambda i, s: (i, s, 0)),
                  pl.BlockSpec((None, tt, c), lambda i, s: (i, s, 1)),
                  pl.BlockSpec((None, tt, c), lambda i, s: (i, s, 2)),
                  pl.BlockSpec((None, SUBLANES, c), halo_map(1)),
                  pl.BlockSpec((None, SUBLANES, c), halo_map(2)),
                  pl.BlockSpec((None, 2, c), lambda i, s: (i, 0, 0)),
                  pl.BlockSpec((3, c), lambda i, s: (0, 0)),
                  pl.BlockSpec((1, c), lambda i, s: (0, 0)),
                  pl.BlockSpec((1, c), lambda i, s: (0, 0))],
        out_specs=[pl.BlockSpec((None, tt, c), lambda i, s: (i, s, 0)),
                   pl.BlockSpec((None, 2, c), lambda i, s: (i, 0, 0))],
        out_shape=[jax.ShapeDtypeStruct((b, t, c), BF16),
                   jax.ShapeDtypeStruct((b, 2, c), F32)],
        compiler_params=_params(("arbitrary", "arbitrary")),
        name="conv_mix",
    )(proj, proj, proj, proj, proj, state, conv_w, conv_b.reshape(1, c), g.reshape(1, c))


def _head_order(n_heads, n_kv):
    grp = n_heads // n_kv
    assert n_kv % 2 == 0
    return [(2 * p + half) * grp + g for p in range(n_kv // 2) for g in range(grp) for half in (0, 1)]


def _permute_blocks(x, order, width, axis, start):
    n = len(order) * width
    take = lambda lo, hi: lax.slice_in_dim(x, lo, hi, axis=axis)
    parts = [take(0, start)] + [take(start + h * width, start + (h + 1) * width) for h in order]
    parts.append(take(start + n, x.shape[axis]))
    return jnp.concatenate(parts, axis=axis)


def _rope_tables(pos, head_dim):
    half = head_dim // 2
    inv_freq = ROPE_THETA ** (-jnp.arange(half, dtype=F32) / half)
    ang = pos.astype(F32)[:, None] * inv_freq[None, :]
    cos, sin = jnp.cos(ang), jnp.sin(ang)
    reps = LANES // head_dim
    return (jnp.tile(jnp.concatenate([cos, cos], axis=-1), (1, reps)),
            jnp.tile(jnp.concatenate([-sin, sin], axis=-1), (1, reps)))


def _rope(x, cos, sin, head_dim):
    width = x.shape[1]
    half = head_dim // 2
    reps = width // LANES
    cos = jnp.concatenate([cos] * reps, axis=1)
    sin = jnp.concatenate([sin] * reps, axis=1)
    lane = lax.broadcasted_iota(jnp.int32, x.shape, 1)
    swapped = jnp.where(lane % head_dim < half,
                        pltpu.roll(x, width - half, 1), pltpu.roll(x, half, 1))
    return x * cos + swapped * sin


def _pair_blocks(slab):
    keys = slab.shape[0]
    lo = lax.broadcasted_iota(jnp.int32, (CHUNK, LANES), 1) < LANES // 2
    zero = jnp.zeros((CHUNK, LANES), slab.dtype)
    parts = []
    for c in range(keys // CHUNK):
        blk = slab[c * CHUNK:(c + 1) * CHUNK]
        parts += [jnp.where(lo, blk, zero), jnp.where(lo, zero, blk)]
    return jnp.concatenate(parts, axis=0)


def _attend_pair(qs, kblk, vblk, sink_vec, key_ok, half_sum):
    head_dim = LANES // 2
    n_chunks = kblk.shape[0] // LANES
    lo = lax.broadcasted_iota(jnp.int32, (1, LANES), 1) < head_dim
    s = lax.dot_general(qs.astype(BF16), kblk, (((1,), (1,)), ((), ())),
                        preferred_element_type=F32) * (head_dim ** -0.5)
    tiles = [jnp.where(key_ok[c], s[:, c * LANES:(c + 1) * LANES], -jnp.inf) for c in range(n_chunks)]

    def per_half(t, reduce_fn, fill):
        a = reduce_fn(jnp.where(lo, t, fill), axis=-1, keepdims=True)
        b = reduce_fn(jnp.where(lo, fill, t), axis=-1, keepdims=True)
        return jnp.where(lo, a, b)

    tmax = tiles[0]
    for t in tiles[1:]:
        tmax = jnp.maximum(tmax, t)
    m = jnp.maximum(per_half(tmax, jnp.max, -jnp.inf), sink_vec)
    es = [jnp.exp(t - m) for t in tiles]
    e_all = jnp.concatenate([e.astype(BF16) for e in es], axis=1)
    both = jnp.dot(e_all, jnp.concatenate([vblk, half_sum], axis=1), preferred_element_type=F32)
    return both[:, :LANES] / (both[:, LANES:] + jnp.exp(sink_vec - m))


def _attend(q, kslabs, vslabs, sinks_ref, key_ok, grp):
    rows = q.shape[0]
    head_dim = LANES // 2
    lo = lax.broadcasted_iota(jnp.int32, (1, LANES), 1) < head_dim
    n_cols = kslabs[0].shape[0]
    col_lo = lax.broadcasted_iota(jnp.int32, (n_cols, LANES), 0) % LANES < head_dim
    half_sum = (col_lo == (lax.broadcasted_iota(jnp.int32, (n_cols, LANES), 1) < head_dim)).astype(BF16)
    outs = []
    for p in range(len(kslabs)):
        qs = jnp.concatenate([q[:, (p * grp + g) * LANES:(p * grp + g + 1) * LANES] for g in range(grp)], axis=0)
        sink_vec = jnp.concatenate(
            [jnp.broadcast_to(jnp.where(lo, sinks_ref[2 * p * grp + g], sinks_ref[(2 * p + 1) * grp + g]),
                              (rows, LANES)) for g in range(grp)], axis=0)
        o = _attend_pair(qs, kslabs[p], vslabs[p], sink_vec, key_ok, half_sum)
        outs.extend(o[g * rows:(g + 1) * rows] for g in range(grp))
    return jnp.concatenate(outs, axis=1)


def _swa_banded_kernel(sinks_ref, q_ref, k_ref, v_ref, cos_ref, sin_ref, g_ref,
                       y_ref, kr_ref, kbuf, vbuf, *, grp, window):
    rb = q_ref.shape[0]
    head_dim = LANES // 2
    n_pairs = kbuf.shape[1] // LANES
    step = pl.program_id(1)

    @pl.when(step == 0)
    def _():
        kbuf[0:window] = jnp.zeros((window, kbuf.shape[1]), BF16)
        vbuf[0:window] = jnp.zeros((window, vbuf.shape[1]), BF16)

    @pl.when(step > 0)
    def _():
        kbuf[0:window] = kbuf[rb:rb + window]
        vbuf[0:window] = vbuf[rb:rb + window]

    cos, sin = cos_ref[...], sin_ref[...]
    k = _rope(k_ref[...], cos, sin, head_dim)
    kr_ref[...] = k
    kbuf[window:window + rb] = k.astype(BF16)
    vbuf[window:window + rb] = v_ref[...].astype(BF16)
    q = _rope(q_ref[...], cos, sin, head_dim)

    n_buf_chunks = (window + rb) // CHUNK
    kblocks = [[_pair_blocks(kbuf[c * CHUNK:(c + 1) * CHUNK, p * LANES:(p + 1) * LANES])
                for c in range(n_buf_chunks)] for p in range(n_pairs)]
    vblocks = [[_pair_blocks(vbuf[c * CHUNK:(c + 1) * CHUNK, p * LANES:(p + 1) * LANES])
                for c in range(n_buf_chunks)] for p in range(n_pairs)]
    band_chunks = window // CHUNK + 1
    lane_key = lax.broadcasted_iota(jnp.int32, (1, LANES), 1) % CHUNK
    for c in range(rb // CHUNK):
        lo_row = c * CHUNK
        key_ok = [step * rb + (c + b) * CHUNK + lane_key >= window for b in range(band_chunks)]
        ks = [jnp.concatenate(kblocks[p][c:c + band_chunks], axis=0) for p in range(n_pairs)]
        vs = [jnp.concatenate(vblocks[p][c:c + band_chunks], axis=0) for p in range(n_pairs)]
        y = _attend(q[lo_row:lo_row + CHUNK], ks, vs, sinks_ref, key_ok, grp)
        y_ref[lo_row:lo_row + CHUNK] = (_rms(y) * g_ref[...]).astype(BF16)


def _swa_banded(proj, sinks, g, n_kv, head_dim, window, col_q):
    b, t, _ = proj.shape
    assert 2 * head_dim == LANES and CHUNK == head_dim
    n_heads = sinks.shape[0]
    att_w = n_heads * head_dim
    kv_w = n_kv * head_dim
    rb = _tile(ATTN_CHUNKS_PER_STEP * CHUNK, t)
    assert window % CHUNK == 0 and t % CHUNK == 0 and rb >= window
    cos, sin = _rope_tables(jnp.arange(t, dtype=jnp.int32), head_dim)
    col_k = (col_q + att_w) // kv_w
    kern = functools.partial(_swa_banded_kernel, grp=n_heads // n_kv, window=window)
    return pl.pallas_call(
        kern,
        grid=(b, t // rb),
        in_specs=[pl.BlockSpec(memory_space=pltpu.SMEM),
                  pl.BlockSpec((None, rb, att_w), lambda i, s: (i, s, col_q // att_w)),
                  pl.BlockSpec((None, rb, kv_w), lambda i, s: (i, s, col_k)),
                  pl.BlockSpec((None, rb, kv_w), lambda i, s: (i, s, col_k + 1)),
                  pl.BlockSpec((rb, LANES), lambda i, s: (s, 0)),
                  pl.BlockSpec((rb, LANES), lambda i, s: (s, 0)),
                  pl.BlockSpec((1, att_w), lambda i, s: (0, 0))],
        out_specs=[pl.BlockSpec((None, rb, att_w), lambda i, s: (i, s, 0)),
                   pl.BlockSpec((None, rb, kv_w), lambda i, s: (i, s, 0))],
        out_shape=[jax.ShapeDtypeStruct((b, t, att_w), BF16),
                   jax.ShapeDtypeStruct((b, t, kv_w), F32)],
        scratch_shapes=[pltpu.VMEM((window + rb, kv_w), BF16),
                        pltpu.VMEM((window + rb, kv_w), BF16)],
        compiler_params=_params(("arbitrary", "arbitrary")),
        name="swa_banded",
    )(sinks, proj, proj, proj, cos, sin, g.reshape(1, att_w))


def _swa_cached_kernel(sinks_ref, q_ref, k_ref, v_ref, ck_ref, cv_ref, cos_ref, sin_ref, g_ref,
                       y_ref, kr_ref, *, grp):
    head_dim = LANES // 2
    t, kv_w = k_ref.shape
    window = ck_ref.shape[0]
    n_pairs = kv_w // LANES
    cos, sin = cos_ref[...], sin_ref[...]
    k = _rope(k_ref[...], cos, sin, head_dim)
    kr_ref[...] = k
    q = _rope(q_ref[...], cos, sin, head_dim)
    n_keys = window + t
    pad = -n_keys % CHUNK
    tail = [jnp.zeros((pad, kv_w), BF16)] if pad else []
    kb = jnp.concatenate([ck_ref[...].astype(BF16), k.astype(BF16)] + tail, axis=0)
    vb = jnp.concatenate([cv_ref[...].astype(BF16), v_ref[...].astype(BF16)] + tail, axis=0)
    ks = [_pair_blocks(kb[:, p * LANES:(p + 1) * LANES]) for p in range(n_pairs)]
    vs = [_pair_blocks(vb[:, p * LANES:(p + 1) * LANES]) for p in range(n_pairs)]
    lane_key = lax.broadcasted_iota(jnp.int32, (1, LANES), 1) % CHUNK
    key_ok = [c * CHUNK + lane_key < n_keys for c in range((n_keys + pad) // CHUNK)]
    y = _attend(q, ks, vs, sinks_ref, key_ok, grp)
    y_ref[...] = (_rms(y) * g_ref[...]).astype(BF16)


def _swa_cached(proj, cache_k, cache_v, sinks, g, n_kv, head_dim, col_q):
    b, t, _ = proj.shape
    assert 2 * head_dim == LANES and CHUNK == head_dim
    window = cache_k.shape[1]
    n_heads = sinks.shape[0]
    att_w = n_heads * head_dim
    kv_w = n_kv * head_dim
    cos, sin = _rope_tables(PAST_LEN + jnp.arange(t, dtype=jnp.int32), head_dim)
    col_k = (col_q + att_w) // kv_w
    kern = functools.partial(_swa_cached_kernel, grp=n_heads // n_kv)
    return pl.pallas_call(
        kern,
        grid=(b,),
        in_specs=[pl.BlockSpec(memory_space=pltpu.SMEM),
                  pl.BlockSpec((None, t, att_w), lambda i: (i, 0, col_q // att_w)),
                  pl.BlockSpec((None, t, kv_w), lambda i: (i, 0, col_k)),
                  pl.BlockSpec((None, t, kv_w), lambda i: (i, 0, col_k + 1)),
                  pl.BlockSpec((None, window, kv_w), lambda i: (i, 0, 0)),
                  pl.BlockSpec((None, window, kv_w), lambda i: (i, 0, 0)),
                  pl.BlockSpec((t, LANES), lambda i: (0, 0)),
                  pl.BlockSpec((t, LANES), lambda i: (0, 0)),
                  pl.BlockSpec((1, att_w), lambda i: (0, 0))],
        out_specs=[pl.BlockSpec((None, t, att_w), lambda i: (i, 0, 0)),
                   pl.BlockSpec((None, t, kv_w), lambda i: (i, 0, 0))],
        out_shape=[jax.ShapeDtypeStruct((b, t, att_w), BF16),
                   jax.ShapeDtypeStruct((b, t, kv_w), F32)],
        compiler_params=_params(("arbitrary",)),
        name="swa_cached",
    )(sinks, proj, proj, proj, cache_k, cache_v, cos, sin, g.reshape(1, att_w))


def _outproj_kernel(yc_ref, ya_ref, wt_ref, wb_ref, b_ref, x_ref, g1_ref, o_ref):
    bb, tt, c = yc_ref.shape
    acc = jnp.dot(yc_ref[...].reshape(bb * tt, c), wt_ref[...], preferred_element_type=F32)
    acc = acc + jnp.dot(ya_ref[...].reshape(bb * tt, c), wb_ref[...], preferred_element_type=F32)
    acc = acc + b_ref[...]
    o_ref[...] = x_ref[...] + g1_ref[...] * acc.reshape(o_ref.shape)


def _outproj(yc, ya, w_bf, b_out, x, mod):
    b, t, d = x.shape
    c = yc.shape[2]
    assert 2 * c == w_bf.shape[0]
    bb, tt = _token_blocks(b, t, ROW_TILE)
    tn = _tile(OUTPROJ_COL_TILE, d)
    return pl.pallas_call(
        _outproj_kernel,
        grid=(b // bb, t // tt, d // tn),
        in_specs=[pl.BlockSpec((bb, tt, c), lambda i, s, j: (i, s, 0)),
                  pl.BlockSpec((bb, tt, c), lambda i, s, j: (i, s, 0)),
                  pl.BlockSpec((c, tn), lambda i, s, j: (0, j)),
                  pl.BlockSpec((c, tn), lambda i, s, j: (1, j)),
                  pl.BlockSpec((1, tn), lambda i, s, j: (0, j)),
                  pl.BlockSpec((bb, tt, tn), lambda i, s, j: (i, s, j)),
                  pl.BlockSpec((bb, 1, tn), lambda i, s, j: (i, 0, 2 * (d // tn) + j))],
        out_specs=pl.BlockSpec((bb, tt, tn), lambda i, s, j: (i, s, j)),
        out_shape=jax.ShapeDtypeStruct((b, t, d), F32),
        compiler_params=_params(("arbitrary", "arbitrary", "arbitrary")),
        name="outproj",
    )(yc, ya, w_bf, w_bf, b_out.reshape(1, d), x, mod)


def _router_kernel(x_ref, sh_ref, sc_ref, g_ref, whi_ref, wlo_ref, br_ref, c0_ref, *rest, has_prev):
    hp_ref, idx_ref, gate_ref, rank_ref, cnt_ref, carry = rest[1:] if has_prev else rest
    bb, tt, d = x_ref.shape
    rows = bb * tt
    n_exp = whi_ref.shape[1]

    @pl.when((pl.program_id(0) == 0) & (pl.program_id(1) == 0))
    def _():
        carry[...] = c0_ref[...]

    y = _rms(x_ref[...]) * g_ref[...]
    h = (y * (1.0 + sc_ref[...]) + sh_ref[...]).reshape(rows, d)
    h_hi = h.astype(BF16)
    h_hi32 = h_hi.astype(F32)
    bits = lax.bitcast_convert_type(h_hi32, U32)
    hp_ref[...] = (bits[:, d // 2:] & jnp.uint32(0xFFFF0000)) | (bits[:, :d // 2] >> 16)

    h_lo = (h - h_hi32).astype(BF16)
    logits = jnp.dot(h_hi, wlo_ref[...], preferred_element_type=F32)
    logits = logits + jnp.dot(h_lo, whi_ref[...], preferred_element_type=F32)
    logits = logits + jnp.dot(h_hi, whi_ref[...], preferred_element_type=F32) + br_ref[...]
    col = lax.broadcasted_iota(jnp.int32, logits.shape, 1).astype(F32)
    work = logits
    vals, idxs = [], []
    for _ in range(TOP_K):
        m = jnp.max(work, axis=-1, keepdims=True)
        sel = jnp.min(jnp.where(work == m, col, float(n_exp)), axis=-1, keepdims=True)
        vals.append(m)
        idxs.append(sel)
        work = jnp.where(col == sel, -jnp.inf, work)
    exps = [jnp.exp(v - vals[0]) for v in vals]
    den = exps[0]
    for e in exps[1:]:
        den = den + e
    idx_ref[...] = jnp.concatenate(idxs, axis=1).astype(jnp.int32)
    gate_ref[...] = jnp.concatenate([e / den for e in exps], axis=1)

    onehots = [(col == sel).astype(F32) for sel in idxs]
    cnt = onehots[0]
    for oh in onehots[1:]:
        cnt = cnt + oh
    earlier = (lax.broadcasted_iota(jnp.int32, (rows, rows), 0)
               > lax.broadcasted_iota(jnp.int32, (rows, rows), 1)).astype(BF16)
    before = jnp.dot(earlier, cnt.astype(BF16), preferred_element_type=F32) + carry[...]
    ranks = [jnp.sum(oh * before, axis=-1, keepdims=True) for oh in onehots]
    rank_ref[...] = jnp.concatenate(ranks, axis=1).astype(jnp.int32)
    carry[...] = carry[...] + jnp.sum(cnt, axis=0, keepdims=True)
    cnt_ref[...] = carry[...]


def _router(x1, mod, g, w_hi, w_lo, b_router, counts_before, hp_prev, tok_offset, n_tok_all):
    b, t, d = x1.shape
    n_exp = w_hi.shape[1]
    bb, tt = _token_blocks(b, t, FFN_ROW_TILE)
    rows = bb * tt
    n_tok = b * t
    assert tok_offset % rows == 0
    blk0 = tok_offset // rows
    flat = lambda i, s: (i * (t // tt) + s, 0)
    has_prev = hp_prev is not None
    in_specs = [pl.BlockSpec((bb, tt, d), lambda i, s: (i, s, 0)),
                pl.BlockSpec((bb, 1, d), lambda i, s: (i, 0, 3)),
                pl.BlockSpec((bb, 1, d), lambda i, s: (i, 0, 4)),
                pl.BlockSpec((1, d), lambda i, s: (0, 0)),
                pl.BlockSpec((d, n_exp), lambda i, s: (0, 0)),
                pl.BlockSpec((d, n_exp), lambda i, s: (0, 0)),
                pl.BlockSpec((1, n_exp), lambda i, s: (0, 0)),
                pl.BlockSpec((1, n_exp), lambda i, s: (0, 0))]
    args = [x1, mod, mod, g.reshape(1, d), w_hi, w_lo, b_router.reshape(1, n_exp), counts_before]
    if has_prev:
        in_specs.append(pl.BlockSpec(memory_space=pl.ANY))
        args.append(hp_prev)
    return pl.pallas_call(
        functools.partial(_router_kernel, has_prev=has_prev),
        grid=(b // bb, t // tt),
        in_specs=in_specs,
        out_specs=[pl.BlockSpec((rows, d // 2), lambda i, s: (blk0 + i * (t // tt) + s, 0)),
                   pl.BlockSpec((rows, TOP_K), flat),
                   pl.BlockSpec((rows, TOP_K), flat),
                   pl.BlockSpec((rows, TOP_K), flat),
                   pl.BlockSpec((1, n_exp), lambda i, s: (0, 0))],
        out_shape=[jax.ShapeDtypeStruct((n_tok_all, d // 2), U32),
                   jax.ShapeDtypeStruct((n_tok, TOP_K), jnp.int32),
                   jax.ShapeDtypeStruct((n_tok, TOP_K), F32),
                   jax.ShapeDtypeStruct((n_tok, TOP_K), jnp.int32),
                   jax.ShapeDtypeStruct((1, n_exp), F32)],
        scratch_shapes=[pltpu.VMEM((1, n_exp), F32)],
        input_output_aliases={len(args) - 1: 0} if has_prev else {},
        compiler_params=_params(("arbitrary", "arbitrary")),
        name="router",
    )(*args)


def _stage_indices(idx_batch_ref, idx_smem, sem_idx):
    cp = pltpu.make_async_copy(idx_batch_ref.at[0, 0], idx_smem, sem_idx)
    cp.start()
    cp.wait()


def _issue_row_copies(idx_smem, base, src_ref, dst_ref, sem):
    def body(g, carry):
        for u in range(ROW_COPY_UNROLL):
            r = g * ROW_COPY_UNROLL + u
            pltpu.make_async_copy(src_ref.at[pl.ds(idx_smem[base + r], 1)], dst_ref.at[pl.ds(r, 1)], sem).start(
                priority=u % 2)
        return carry

    lax.fori_loop(0, dst_ref.shape[0] // ROW_COPY_UNROLL, body, 0)


def _issue_tile(tile, first_tile, idx0_ref, idxn_ref, idx_smem, sem_idx, src_ref, dst_ref, sem):
    n_rows = dst_ref.shape[0]
    batch = idx_smem.shape[0] // n_rows
    rel = tile - first_tile

    @pl.when(rel == 0)
    def _():
        _stage_indices(idx0_ref, idx_smem, sem_idx)

    @pl.when((rel > 0) & (lax.rem(rel, batch) == 0))
    def _():
        _stage_indices(idxn_ref, idx_smem, sem_idx)

    _issue_row_copies(idx_smem, lax.rem(rel, batch) * n_rows, src_ref, dst_ref, sem)


def _wait_row_copies(src_ref, dst_ref, sem):
    pltpu.make_async_copy(src_ref.at[pl.ds(0, dst_ref.shape[0])], dst_ref, sem).wait()


def _gather_kernel(nused_ref, idx0_ref, idxn_ref, src_ref, o_ref, idx_smem, buf, sem_idx, sem_rows):
    i = pl.program_id(0)
    n = nused_ref[0]
    half = src_ref.shape[1]
    n_slots = buf.shape[0]
    ahead = n_slots - 1

    @pl.when(i == 0)
    def _():
        for k in range(ahead):
            @pl.when(k < n)
            def _():
                _issue_tile(jnp.int32(k), 0, idx0_ref, idxn_ref, idx_smem, sem_idx, src_ref, buf.at[k],
                            sem_rows.at[k])

    @pl.when(i + ahead < n)
    def _():
        nxt = lax.rem(i + ahead, n_slots)
        _issue_tile(i + ahead, 0, idx0_ref, idxn_ref, idx_smem, sem_idx, src_ref, buf.at[nxt], sem_rows.at[nxt])

    @pl.when(i < n)
    def _():
        slot = lax.rem(i, n_slots)
        _wait_row_copies(src_ref, buf.at[slot], sem_rows.at[slot])
        w = buf[slot]
        o_ref[:, :half] = _pair_rows(lax.bitcast_convert_type(w << 16, F32).astype(BF16))
        o_ref[:, half:] = _pair_rows(lax.bitcast_convert_type(w & jnp.uint32(0xFFFF0000), F32).astype(BF16))


def _pair_rows(x):
    return pltpu.bitcast(x, U32)


def _gather_rows(src_packed, row_src, n_used, rows):
    p = row_src.shape[0]
    half = src_packed.shape[1]
    nblk = p // rows
    batch = _tile(INDEX_BATCH_TILES, nblk)
    ahead = ROW_COPY_SLOTS - 1
    assert batch >= ahead
    idx_batches = row_src.reshape(nblk // batch, 1, batch * rows)
    clamp = lambda i, nu: jnp.minimum(i, nu[0] - 1)
    return pl.pallas_call(
        _gather_kernel,
        grid_spec=pltpu.PrefetchScalarGridSpec(
            num_scalar_prefetch=1,
            grid=(nblk,),
            in_specs=[pl.BlockSpec((1, 1, batch * rows), lambda i, nu: (0, 0, 0)),
                      pl.BlockSpec((1, 1, batch * rows), lambda i, nu: (clamp(i + ahead, nu) // batch, 0, 0)),
                      pl.BlockSpec(memory_space=pl.ANY)],
            out_specs=pl.BlockSpec((rows // 2, 2 * half), lambda i, nu: (clamp(i, nu), 0)),
            scratch_shapes=[pltpu.SMEM((batch * rows,), jnp.int32),
                            pltpu.VMEM((ROW_COPY_SLOTS, rows, half), U32),
                            pltpu.SemaphoreType.DMA(()),
                            pltpu.SemaphoreType.DMA((ROW_COPY_SLOTS,))]),
        out_shape=jax.ShapeDtypeStruct((p // 2, 2 * half), U32),
        compiler_params=_params(("arbitrary",)),
        name="moe_gather",
    )(n_used, idx_batches, idx_batches, src_packed)


def _stream_expert_tiles(first_ref, ntile_ref, ntot_ref, half_ref, src_hbm, dst_hbm, xbuf, obuf, sem_in, sem_out,
                         compute_first, compute):
    j, e = pl.program_id(0), pl.program_id(1)
    nj, ne = pl.num_programs(0), pl.num_programs(1)
    n_slots = xbuf.shape[0]
    lookahead = n_slots - 2
    in_rows = xbuf.shape[1]
    n_out = obuf.shape[0]
    out_rows = obuf.shape[1]
    tn = obuf.shape[2]
    n_total = ntot_ref[0]
    n_seq = nj * n_total

    def x_copy(t, slot):
        r0 = pl.multiple_of(t * in_rows, in_rows)
        return pltpu.make_async_copy(src_hbm.at[pl.ds(r0, in_rows)], xbuf.at[slot], sem_in.at[slot])

    def x_start(seq):
        x_copy(lax.rem(seq, n_total), lax.rem(seq, n_slots)).start(priority=TILE_COPY_PRIORITY)

    def o_copy(t, slot):
        r0 = pl.multiple_of(t * out_rows, out_rows)
        c0 = pl.multiple_of(j * tn, tn)
        return pltpu.make_async_copy(obuf.at[slot], dst_hbm.at[pl.ds(r0, out_rows), pl.ds(c0, tn)],
                                     sem_out.at[slot])

    @pl.when((j == 0) & (e == 0))
    def _():
        for seq in range(lookahead):
            @pl.when(seq < n_seq)
            def _():
                x_start(jnp.int32(seq))

    def step(t, compute_fn, width, half_last=False):
        seq = j * n_total + t
        in_slots = [lax.rem(seq + k, n_slots) for k in range(width)]
        out_slots = [lax.rem(seq + k, n_out) for k in range(width)]
        for k in range(width):
            x_copy(0, in_slots[k]).wait()

        for k in range(width):
            @pl.when(seq + lookahead + k < n_seq)
            def _():
                x_start(seq + lookahead + k)

        for k in range(width):
            @pl.when(seq + k >= n_out)
            def _():
                o_copy(t, out_slots[k]).wait()

        for k in range(width):
            if half_last and k == width - 1:
                obuf[out_slots[k], 0:out_rows // 2] = compute_fn(xbuf[in_slots[k], 0:in_rows // 2])
            else:
                obuf[out_slots[k]] = compute_fn(xbuf[in_slots[k]])
        for k in range(width):
            o_copy(t + k, out_slots[k]).start()

    t0 = first_ref[e]
    nt = ntile_ref[e]

    @pl.when(nt > 0)
    def _():
        step(t0, compute_first, 1)

    half = (half_ref[e] > 0) & (nt > 1)
    pair_ends = (nt > 1) & (lax.rem(nt - 1, 2) == 0)
    lone_last = (nt > 1) & (lax.rem(nt - 1, 2) == 1)
    n_pairs = jnp.maximum(nt - 1, 0) // 2 - jnp.where(half & pair_ends, 1, 0)

    def body(p, carry):
        step(t0 + 1 + 2 * p, compute, 2)
        return carry

    lax.fori_loop(0, n_pairs, body, 0)

    @pl.when(half & pair_ends)
    def _():
        step(t0 + nt - 2, compute, 2, half_last=True)

    @pl.when(lone_last & jnp.logical_not(half))
    def _():
        step(t0 + nt - 1, compute, 1)

    @pl.when(lone_last & half)
    def _():
        step(t0 + nt - 1, compute, 1, half_last=True)

    @pl.when((j == nj - 1) & (e == ne - 1))
    def _():
        for k in range(n_out):
            @pl.when(n_seq > k)
            def _():
                o_copy(0, lax.rem(n_seq - 1 - k, n_out)).wait()


def _swiglu(g, u):
    g = jnp.minimum(g, SWIGLU_LIMIT)
    u = jnp.clip(u, -SWIGLU_LIMIT, SWIGLU_LIMIT)
    return (u + 1.0) * g * jax.nn.sigmoid(SWIGLU_ALPHA * g)


def _moe_up_kernel(first_ref, ntile_ref, eff_ref, ntot_ref, half_ref, x_hbm, wg_ref, wu_ref, bg_ref, bu_ref,
                   act_hbm, xbuf, obuf, wg_bf, wu_bf, sem_in, sem_out):
    kc = wg_ref.shape[0] // MOE_CAST_CHUNKS

    def compute_first(x_words):
        x = pltpu.bitcast(x_words, BF16)
        g = bg_ref[...]
        u = bu_ref[...]
        for c in range(MOE_CAST_CHUNKS):
            ks = slice(c * kc, (c + 1) * kc)
            wg_c = wg_ref[ks, :].astype(BF16)
            wu_c = wu_ref[ks, :].astype(BF16)
            wg_bf[ks, :] = wg_c
            wu_bf[ks, :] = wu_c
            g = g + jnp.dot(x[:, ks], wg_c, preferred_element_type=F32)
            u = u + jnp.dot(x[:, ks], wu_c, preferred_element_type=F32)
        return _pair_rows(_swiglu(g, u).astype(BF16))

    def compute(x_words):
        x = pltpu.bitcast(x_words, BF16)
        g = jnp.dot(x, wg_bf[...], preferred_element_type=F32) + bg_ref[...]
        u = jnp.dot(x, wu_bf[...], preferred_element_type=F32) + bu_ref[...]
        return _pair_rows(_swiglu(g, u).astype(BF16))

    _stream_expert_tiles(first_ref, ntile_ref, ntot_ref, half_ref, x_hbm, act_hbm, xbuf, obuf, sem_in, sem_out,
                         compute_first, compute)


def _moe_up(xs, w_gate_up, b_gate_up, tile_first, tile_count, eff_expert, n_tiles, tail_half):
    p2, d = xs.shape
    n_exp, _, f2 = w_gate_up.shape
    f = f2 // 2
    tf = _tile(MOE_FF_TILE, f)
    nj = f // tf
    return pl.pallas_call(
        _moe_up_kernel,
        grid_spec=pltpu.PrefetchScalarGridSpec(
            num_scalar_prefetch=5,
            grid=(nj, n_exp),
            in_specs=[pl.BlockSpec(memory_space=pl.ANY),
                      pl.BlockSpec((None, d, tf), lambda j, e, fi, nt, ef, ntot, hf: (ef[e], 0, j)),
                      pl.BlockSpec((None, d, tf), lambda j, e, fi, nt, ef, ntot, hf: (ef[e], 0, nj + j)),
                      pl.BlockSpec((None, 1, tf), lambda j, e, fi, nt, ef, ntot, hf: (ef[e], 0, j)),
                      pl.BlockSpec((None, 1, tf), lambda j, e, fi, nt, ef, ntot, hf: (ef[e], 0, nj + j))],
            out_specs=pl.BlockSpec(memory_space=pl.ANY),
            scratch_shapes=[pltpu.VMEM((MOE_IN_SLOTS, MOE_ROWS // 2, d), U32),
                            pltpu.VMEM((MOE_OUT_SLOTS, MOE_ROWS // 2, tf), U32),
                            pltpu.VMEM((d, tf), BF16),
                            pltpu.VMEM((d, tf), BF16),
                            pltpu.SemaphoreType.DMA((MOE_IN_SLOTS,)),
                            pltpu.SemaphoreType.DMA((MOE_OUT_SLOTS,))]),
        out_shape=jax.ShapeDtypeStruct((p2, f), U32),
        compiler_params=_params(("arbitrary", "arbitrary")),
        name="moe_up",
    )(tile_first, tile_count, eff_expert, n_tiles, tail_half, xs, w_gate_up, w_gate_up,
      b_gate_up.reshape(n_exp, 1, f2), b_gate_up.reshape(n_exp, 1, f2))


def _moe_down_kernel(first_ref, ntile_ref, eff_ref, ntot_ref, half_ref, a_hbm, w_ref, b_ref, y_hbm,
                     xbuf, obuf, w_bf, sem_in, sem_out):
    kc = w_ref.shape[0] // MOE_CAST_CHUNKS

    def compute_first(a_words):
        a = pltpu.bitcast(a_words, BF16)
        y = b_ref[...]
        for c in range(MOE_CAST_CHUNKS):
            ks = slice(c * kc, (c + 1) * kc)
            w_c = w_ref[ks, :].astype(BF16)
            w_bf[ks, :] = w_c
            y = y + jnp.dot(a[:, ks], w_c, preferred_element_type=F32)
        return y

    def compute(a_words):
        return jnp.dot(pltpu.bitcast(a_words, BF16), w_bf[...], preferred_element_type=F32) + b_ref[...]

    _stream_expert_tiles(first_ref, ntile_ref, ntot_ref, half_ref, a_hbm, y_hbm, xbuf, obuf, sem_in, sem_out,
                         compute_first, compute)


def _moe_down(act, w_down, b_down, tile_first, tile_count, eff_expert, n_tiles, tail_half):
    p2, f = act.shape
    n_exp, _, d = w_down.shape
    tn = _tile(MOE_OUT_TILE, d)
    return pl.pallas_call(
        _moe_down_kernel,
        grid_spec=pltpu.PrefetchScalarGridSpec(
            num_scalar_prefetch=5,
            grid=(d // tn, n_exp),
            in_specs=[pl.BlockSpec(memory_space=pl.ANY),
                      pl.BlockSpec((None, f, tn), lambda j, e, fi, nt, ef, ntot, hf: (ef[e], 0, j)),
                      pl.BlockSpec((None, 1, tn), lambda j, e, fi, nt, ef, ntot, hf: (ef[e], 0, j))],
            out_specs=pl.BlockSpec(memory_space=pl.ANY),
            scratch_shapes=[pltpu.VMEM((MOE_IN_SLOTS, MOE_ROWS // 2, f), U32),
                            pltpu.VMEM((MOE_OUT_SLOTS, MOE_ROWS, tn), F32),
                            pltpu.VMEM((f, tn), BF16),
                            pltpu.SemaphoreType.DMA((MOE_IN_SLOTS,)),
                            pltpu.SemaphoreType.DMA((MOE_OUT_SLOTS,))]),
        out_shape=jax.ShapeDtypeStruct((2 * p2, d), F32),
        compiler_params=_params(("arbitrary", "arbitrary")),
        name="moe_down",
    )(tile_first, tile_count, eff_expert, n_tiles, tail_half, act, w_down, b_down.reshape(n_exp, 1, d))


def _combine_kernel(pos0_ref, posn_ref, y_ref, gate_ref, x_ref, g2_ref, gf_ref, o_ref,
                    pos_smem, buf, sem_idx, sem_rows):
    bb, tt, d = x_ref.shape
    rows = bb * tt
    n_slots = buf.shape[0]
    ahead = n_slots - 1
    i = pl.program_id(0)
    n = pl.num_programs(0)

    @pl.when(i == 0)
    def _():
        for k in range(ahead):
            @pl.when(k < n)
            def _():
                _issue_tile(jnp.int32(k), 0, pos0_ref, posn_ref, pos_smem, sem_idx, y_ref, buf.at[k],
                            sem_rows.at[k])

    @pl.when(i + ahead < n)
    def _():
        nxt = lax.rem(i + ahead, n_slots)
        _issue_tile(i + ahead, 0, pos0_ref, posn_ref, pos_smem, sem_idx, y_ref, buf.at[nxt], sem_rows.at[nxt])

    slot = lax.rem(i, n_slots)
    _wait_row_copies(y_ref, buf.at[slot], sem_rows.at[slot])
    gates = gate_ref[...]
    ffn = gates[:, 0:1] * buf[slot, 0:rows]
    for k in range(1, TOP_K):
        ffn = ffn + gates[:, k:k + 1] * buf[slot, k * rows:(k + 1) * rows]
    x2 = x_ref[...] + g2_ref[...] * ffn.reshape(bb, tt, d)
    o_ref[...] = _rms(x2) * gf_ref[...]


def _combine(y_sorted, pos, gates, x1, mod, gf, tok_offset):
    b, t, d = x1.shape
    bb, tt = _token_blocks(b, t, COMBINE_ROWS)
    rows = bb * tt
    assert tok_offset % rows == 0
    blk0 = tok_offset // rows
    n_tok = pos.shape[0]
    nblk = (b // bb) * (t // tt)
    ts = t // tt
    pos_tiles = pos.reshape(n_tok // rows, rows, TOP_K).transpose(0, 2, 1)[blk0:blk0 + nblk]
    batch = _tile(INDEX_BATCH_TILES, nblk)
    ahead = ROW_COPY_SLOTS - 1
    assert batch >= ahead
    pos_batches = pos_tiles.reshape(nblk // batch, 1, batch * TOP_K * rows)
    return pl.pallas_call(
        _combine_kernel,
        grid=(nblk,),
        in_specs=[pl.BlockSpec((1, 1, batch * TOP_K * rows), lambda i: (0, 0, 0)),
                  pl.BlockSpec((1, 1, batch * TOP_K * rows),
                               lambda i: (jnp.minimum(i + ahead, nblk - 1) // batch, 0, 0)),
                  pl.BlockSpec(memory_space=pl.ANY),
                  pl.BlockSpec((rows, TOP_K), lambda i: (blk0 + i, 0)),
                  pl.BlockSpec((bb, tt, d), lambda i: (i // ts, i % ts, 0)),
                  pl.BlockSpec((bb, 1, d), lambda i: (i // ts, 0, 5)),
                  pl.BlockSpec((1, d), lambda i: (0, 0))],
        out_specs=pl.BlockSpec((bb, tt, d), lambda i: (i // ts, i % ts, 0)),
        out_shape=jax.ShapeDtypeStruct((b, t, d), F32),
        scratch_shapes=[pltpu.SMEM((batch * TOP_K * rows,), jnp.int32),
                        pltpu.VMEM((ROW_COPY_SLOTS, TOP_K * rows, d), F32),
                        pltpu.SemaphoreType.DMA(()),
                        pltpu.SemaphoreType.DMA((ROW_COPY_SLOTS,))],
        compiler_params=_params(("arbitrary",)),
        name="moe_combine",
    )(pos_batches, pos_batches, y_sorted, gates, x1, mod, gf.reshape(1, d))


def _routing(top_idx, rank, counts, rows):
    n_tok = top_idx.shape[0]
    n_exp = counts.shape[0]
    n_assign = n_tok * TOP_K
    tile_count = (counts + rows - 1) // rows
    tile_end = jnp.cumsum(tile_count)
    tile_first = tile_end - tile_count
    pos = (tile_first * rows)[top_idx] + rank
    p = n_assign + n_exp * rows
    token_of = jnp.arange(n_assign, dtype=jnp.int32) // TOP_K
    row_token = jnp.zeros((p,), jnp.int32).at[pos.reshape(-1)].set(
        token_of, unique_indices=True, mode="promise_in_bounds")
    ids = jnp.where(tile_count > 0, jnp.arange(n_exp, dtype=jnp.int32), -1)
    eff = lax.cummax(ids)
    eff = jnp.where(eff < 0, jnp.argmax(tile_count > 0).astype(jnp.int32), eff)
    i32 = lambda a: a.astype(jnp.int32)
    tail_half = (tile_count > 1) & (counts - (tile_count - 1) * rows <= rows // 2)
    return i32(pos), row_token, i32(tile_first), i32(tile_count), i32(eff), i32(tile_end[-1:]), i32(tail_half)


def kernel(x_prompt, x_sample, cache_k, cache_v, state_conv, c_prompt, c_sample, w_ada, b_ada, norm_mix_g, norm_ffn_g, norm_final_g, w_in, b_in, conv_w, conv_b, sinks, out_norm_conv_g, out_norm_attn_g, w_out, b_out, w_router, b_router, w_gate_up, b_gate_up, w_down, b_down):
    d = x_prompt.shape[2]
    depth, dec_b, window, n_kv, head_dim = cache_k.shape
    assert depth == 1
    conv_ch = conv_w.shape[1]
    n_exp = w_router.shape[1]
    n_heads = sinks.shape[0]
    bp, tp, _ = x_prompt.shape
    bs, ts, _ = x_sample.shape
    col_q = 3 * conv_ch
    kv_w = n_kv * head_dim
    att_w = n_heads * head_dim

    mod = _adaln(jnp.concatenate([c_prompt, c_sample], axis=0), w_ada, b_ada)
    mod_p = mod[:bp].reshape(bp, 1, 6 * d)
    mod_s = mod[bp:].reshape(bs, 1, 6 * d)
    order = _head_order(n_heads, n_kv)
    w_in_bf = w_in.astype(BF16)
    w_in_bf = lax.dynamic_update_slice(
        w_in_bf, _permute_blocks(w_in_bf[:, col_q:col_q + att_w], order, head_dim, 1, 0), (0, col_q))
    b_in_p = _permute_blocks(b_in, order, head_dim, 0, col_q)
    w_out_bf = w_out.astype(BF16)
    w_out_bf = lax.dynamic_update_slice(
        w_out_bf, _permute_blocks(w_out_bf[conv_ch:], order, head_dim, 0, 0), (conv_ch, 0))
    attn_g = _permute_blocks(out_norm_attn_g, order, head_dim, 0, 0)
    wr_hi = w_router.astype(BF16)
    wr_lo = (w_router - wr_hi.astype(F32)).astype(BF16)
    n_p, n_s = bp * tp, bs * ts

    def mixer(x, modx, conv_prev, ck, cv, counts_before, hp_prev, tok_offset):
        b, t, _ = x.shape
        proj = _inproj(x, modx, norm_mix_g, w_in_bf, b_in_p)
        yc, conv_state = _conv_mix(proj, conv_prev, conv_w, conv_b, out_norm_conv_g)
        if ck is None:
            ya, k_rot = _swa_banded(proj, sinks, attn_g, n_kv, head_dim, window, col_q)
            keep = window
        else:
            ya, k_rot = _swa_cached(proj, ck, cv, sinks, attn_g, n_kv, head_dim, col_q)
            keep = t
        k_rows = k_rot[:, t - keep:].reshape(1, b, keep, n_kv, head_dim)
        v_rows = proj[:, t - keep:, col_q + att_w + kv_w:].reshape(1, b, keep, n_kv, head_dim)
        x1 = _outproj(yc, ya, w_out_bf, b_out, x, modx)
        routed = _router(x1, modx, norm_ffn_g, wr_hi, wr_lo, b_router, counts_before, hp_prev,
                         tok_offset, n_p + n_s)
        return x1, routed, conv_state, k_rows, v_rows

    zero_state = jnp.zeros((bp, 2, conv_ch), F32)
    x1p, (hp_buf, idxp, gatep, rankp, cntp), conv_p, k_prompt, v_prompt = mixer(
        x_prompt, mod_p, zero_state, None, None, jnp.zeros((1, n_exp), F32), None, 0)
    x1s, (h2_packed, idxs, gates_s, ranks, cnt), conv_s, k_sample, v_sample = mixer(
        x_sample, mod_s, state_conv[0],
        cache_k[0].reshape(dec_b, window, kv_w), cache_v[0].reshape(dec_b, window, kv_w), cntp, hp_buf, n_p)

    top_idx = jnp.concatenate([idxp, idxs], axis=0)
    gates = jnp.concatenate([gatep, gates_s], axis=0)
    rank = jnp.concatenate([rankp, ranks], axis=0)
    pos, row_token, tile_first, tile_count, eff, n_tiles, tail_half = _routing(
        top_idx, rank, cnt[0].astype(jnp.int32), MOE_ROWS)
    xs = _gather_rows(h2_packed, row_token, n_tiles, MOE_ROWS)
    act = _moe_up(xs, w_gate_up, b_gate_up, tile_first, tile_count, eff, n_tiles, tail_half)
    y_sorted = _moe_down(act, w_down, b_down, tile_first, tile_count, eff, n_tiles, tail_half)
    y_prompt = _combine(y_sorted, pos, gates, x1p, mod_p, norm_final_g, 0)
    y_sample = _combine(y_sorted, pos, gates, x1s, mod_s, norm_final_g, n_p)
    return (y_prompt, y_sample, k_prompt, v_prompt, conv_p[None], k_sample, v_sample, conv_s[None])
```
